```python
import jax, jax.numpy as jnp
from jax import lax
import numpy as np

D_MODEL = 1024
BATCH = 8
SEQ = 2048
DEPTH = 2

CHUNK = 64
Q_BLOCK = 128
ROPE_THETA = 500000.0
EPS = 1e-6

A_HEADS = 4
A_HEAD_DIM = 128
A_ROT = A_HEAD_DIM // 4
IDX_HEADS = 8
IDX_DIM = 64
IDX_ROT = IDX_DIM // 4
TOPK_MAX = 256
B_HEADS = 4
B_NOPE = 128
B_ROPE = 64
B_V = 128
B_Q_RANK = 256
B_KV_RANK = 128
C_HEADS = 4
C_HEAD_DIM = 128
C_LEFT_CHUNKS = 8
C_BAND = (C_LEFT_CHUNKS + 1) * CHUNK
REL_CLIP = 128
N_BRANCH = 3
A_QW = A_HEADS * A_HEAD_DIM
B_OUT = B_HEADS * B_V
C_W = C_HEADS * C_HEAD_DIM
IN_SIZES = (A_QW, A_HEAD_DIM, A_HEAD_DIM, IDX_HEADS * IDX_DIM, IDX_DIM, IDX_HEADS,
            B_Q_RANK, B_KV_RANK, B_ROPE, C_W, C_W, C_W, N_BRANCH * D_MODEL)
N_IN = (A_QW + 2 * A_HEAD_DIM + IDX_HEADS * IDX_DIM + IDX_DIM + IDX_HEADS
        + B_Q_RANK + B_KV_RANK + B_ROPE + 3 * C_W + N_BRANCH * D_MODEL)
N_GROUPS = 4
EXPERTS_PER_GROUP = 8
N_EXPERTS = N_GROUPS * EXPERTS_PER_GROUP
TOP_K_EXPERT = 2
D_EXPERT = 256

kernel_name = "hybrid_chunk_causal_dsa_mla_chunkattn_hmoe"


def rmsnorm(x, g):
    xf = x.astype(jnp.float32)
    y = xf * lax.rsqrt(jnp.mean(xf * xf, axis=-1, keepdims=True) + EPS)
    return (y * g.astype(jnp.float32)).astype(x.dtype)


def rope_tables(seq, rot):
    pos = jnp.arange(seq, dtype=jnp.float32)
    inv = ROPE_THETA ** (-jnp.arange(0, rot, 2, dtype=jnp.float32) / rot)
    ang = pos[:, None] * inv[None, :]
    return jnp.cos(ang), jnp.sin(ang)


def apply_rope(x, cos, sin):
    if x.ndim == 4:
        cos, sin = cos[:, None, :], sin[:, None, :]
    xf = x.astype(jnp.float32)
    x1, x2 = jnp.split(xf, 2, axis=-1)
    out = jnp.concatenate([x1 * cos - x2 * sin, x1 * sin + x2 * cos], axis=-1)
    return out.astype(x.dtype)


def partial_rope(x, cos, sin, rot):
    return jnp.concatenate([apply_rope(x[..., :rot], cos, sin), x[..., rot:]], axis=-1)


def split_cols(z, sizes):
    outs, off = [], 0
    for s in sizes:
        outs.append(z[..., off:off + s])
        off += s
    return outs


def to_blocks(t, nb):
    b = t.shape[0]
    return jnp.moveaxis(t.reshape((b, nb, Q_BLOCK) + t.shape[2:]), 1, 0)


def from_blocks(t):
    t = jnp.moveaxis(t, 0, 1)
    return t.reshape(t.shape[0], t.shape[1] * t.shape[2], -1)


def dsa_mixer(q, k, v, iq, ik, iw):
    bsz, seq = q.shape[0], q.shape[1]
    topk = min(TOPK_MAX, seq // 4)
    nb = seq // Q_BLOCK
    kchunk = jnp.arange(seq) // CHUNK
    qchunk = kchunk.reshape(nb, Q_BLOCK)
    iw = iw.astype(jnp.float32) * (IDX_HEADS ** -0.5)

    def one_block(args):
        qb, iqb, iwb, qc = args
        rel = jax.nn.relu(jnp.einsum('bqhd,bkd->bqhk', iqb, ik).astype(jnp.float32) * (IDX_DIM ** -0.5))
        score = jnp.einsum('bqhk,bqh->bqk', rel, iwb)
        allowed = kchunk[None, :] <= qc[:, None]
        score = jnp.where(allowed[None], score, -jnp.inf)
        _, idx = lax.top_k(score, topk)
        sel_ok = kchunk[idx] <= qc[None, :, None]
        kg = jax.vmap(lambda kk, ii: kk[ii])(k, idx)
        vg = jax.vmap(lambda vv, ii: vv[ii])(v, idx)
        logits = jnp.einsum('bqhd,bqkd->bqhk', qb, kg).astype(jnp.float32) * (A_HEAD_DIM ** -0.5)
        logits = jnp.where(sel_ok[:, :, None, :], logits, -jnp.inf)
        p = jax.nn.softmax(logits, axis=-1).astype(vg.dtype)
        return jnp.einsum('bqhk,bqkd->bqhd', p, vg)

    out = lax.map(one_block, (to_blocks(q, nb), to_blocks(iq, nb), to_blocks(iw, nb), qchunk))
    return from_blocks(out)


def mla_mixer(q_nope, q_rope, k_nope, k_rope, v):
    seq = q_nope.shape[1]
    nb = seq // Q_BLOCK
    kchunk = jnp.arange(seq) // CHUNK
    qchunk = kchunk.reshape(nb, Q_BLOCK)
    scale = (B_NOPE + B_ROPE) ** -0.5

    def one_block(args):
        qn, qr, qc = args
        s = (jnp.einsum('bqhd,bkhd->bhqk', qn, k_nope)
             + jnp.einsum('bqhr,bkr->bhqk', qr, k_rope)).astype(jnp.float32) * scale
        mask = kchunk[None, :] <= qc[:, None]
        s = jnp.where(mask[None, None], s, -jnp.inf)
        p = jax.nn.softmax(s, axis=-1).astype(v.dtype)
        return jnp.einsum('bhqk,bkhd->bqhd', p, v)

    out = lax.map(one_block, (to_blocks(q_nope, nb), to_blocks(q_rope, nb), qchunk))
    return from_blocks(out)


def chunk_band_mixer(q, k, v, rel_table):
    bsz, seq = q.shape[0], q.shape[1]
    n_chunks = seq // CHUNK
    shp = (bsz, n_chunks, CHUNK, C_HEADS, C_HEAD_DIM)
    qc, kc, vc = q.reshape(shp), k.reshape(shp), v.reshape(shp)
    pad = ((0, 0), (C_LEFT_CHUNKS, 0), (0, 0), (0, 0), (0, 0))
    kp, vp = jnp.pad(kc, pad), jnp.pad(vc, pad)
    band_idx = jnp.arange(n_chunks)[:, None] + jnp.arange(C_LEFT_CHUNKS + 1)[None, :]
    kb = kp[:, band_idx].reshape(bsz, n_chunks, C_BAND, C_HEADS, C_HEAD_DIM)
    vb = vp[:, band_idx].reshape(bsz, n_chunks, C_BAND, C_HEADS, C_HEAD_DIM)
    valid = jnp.repeat(band_idx >= C_LEFT_CHUNKS, CHUNK, axis=1)
    qi = jnp.arange(CHUNK)
    kj = jnp.arange(C_BAND)
    dist = C_LEFT_CHUNKS * CHUNK + qi[:, None] - kj[None, :]
    bias = rel_table[:, jnp.clip(dist, -REL_CLIP, REL_CLIP) + REL_CLIP].astype(jnp.float32)
    s = jnp.einsum('bcqhd,bckhd->bchqk', qc, kb).astype(jnp.float32) * (C_HEAD_DIM ** -0.5) + bias[None, None]
    s = jnp.where(valid[None, :, None, None, :], s, -jnp.inf)
    p = jax.nn.softmax(s, axis=-1).astype(vb.dtype)
    out = jnp.einsum('bchqk,bckhd->bcqhd', p, vb)
    return out.reshape(bsz, seq, C_W)


def hier_moe(h, w_group, b_group, w_router, b_router, w_gate, w_up, w_down):
    bsz, seq, d = h.shape
    t = h.reshape(-1, d)
    p_grp = jax.nn.softmax((t @ w_group).astype(jnp.float32) + b_group.astype(jnp.float32), axis=-1)
    pg, gsel = lax.top_k(p_grp, 1)
    e_logits = ((t @ w_router).astype(jnp.float32) + b_router.astype(jnp.float32)).reshape(-1, N_GROUPS, EXPERTS_PER_GROUP)
    e_in = jnp.take_along_axis(e_logits, gsel[:, :, None], axis=1)[:, 0]
    pe, esel = lax.top_k(jax.nn.softmax(e_in, axis=-1), TOP_K_EXPERT)
    wts = pg * pe / jnp.sum(pe, axis=-1, keepdims=True)
    gidx = gsel * EXPERTS_PER_GROUP + esel
    combine = jnp.sum(jax.nn.one_hot(gidx, N_EXPERTS, dtype=jnp.float32) * wts[..., None], axis=1)
    a = jnp.einsum('td,edf->tef', t, w_gate)
    u = jnp.einsum('td,edf->tef', t, w_up)
    hh = jax.nn.silu(a) * u * combine[:, :, None].astype(t.dtype)
    y = jnp.einsum('tef,efd->td', hh, w_down)
    return y.reshape(bsz, seq, d)


def setup_inputs(seed: int = 0) -> dict:
    key = jax.random.key(seed)
    ks = jax.random.split(key, 24)
    L = DEPTH
    f32 = jnp.float32

    def nrm(k, shape, fan):
        return jax.random.normal(k, shape, f32) * (fan ** -0.5)

    def gain(k, shape):
        return 1.0 + 0.05 * jax.random.normal(k, shape, f32)

    return {
        "x": jax.random.normal(ks[0], (BATCH, SEQ, D_MODEL), f32),
        "attn_norm_g": gain(ks[1], (L, D_MODEL)),
        "w_in": nrm(ks[2], (L, D_MODEL, N_IN), D_MODEL),
        "b_q_norm_g": gain(ks[3], (L, B_Q_RANK)),
        "b_w_uq": nrm(ks[4], (L, B_Q_RANK, B_HEADS * (B_NOPE + B_ROPE)), B_Q_RANK),
        "b_kv_norm_g": gain(ks[5], (L, B_KV_RANK)),
        "b_w_ukv": nrm(ks[6], (L, B_KV_RANK, B_HEADS * (B_NOPE + B_V)), B_KV_RANK),
        "c_rel_bias": 0.2 * jax.random.normal(ks[7], (L, C_HEADS, 2 * REL_CLIP + 1), f32),
        "w_proj_a": nrm(ks[8], (L, A_QW, D_MODEL), A_QW),
        "w_proj_b": nrm(ks[9], (L, B_OUT, D_MODEL), B_OUT),
        "w_proj_c": nrm(ks[10], (L, C_W, D_MODEL), C_W),
        "w_out": nrm(ks[11], (L, D_MODEL, D_MODEL), D_MODEL),
        "ffn_norm_g": gain(ks[12], (L, D_MODEL)),
        "w_group": nrm(ks[13], (L, D_MODEL, N_GROUPS), D_MODEL),
        "b_group": 0.01 * jax.random.normal(ks[14], (L, N_GROUPS), f32),
        "w_router": nrm(ks[15], (L, D_MODEL, N_EXPERTS), D_MODEL),
        "b_router": 0.01 * jax.random.normal(ks[16], (L, N_EXPERTS), f32),
        "w_gate": nrm(ks[17], (L, N_EXPERTS, D_MODEL, D_EXPERT), D_MODEL),
        "w_up": nrm(ks[18], (L, N_EXPERTS, D_MODEL, D_EXPERT), D_MODEL),
        "w_down": nrm(ks[19], (L, N_EXPERTS, D_EXPERT, D_MODEL), D_EXPERT),
        "final_norm_g": gain(ks[20], (D_MODEL,)),
    }


def reference(x, attn_norm_g, w_in, b_q_norm_g, b_w_uq, b_kv_norm_g, b_w_ukv, c_rel_bias,
              w_proj_a, w_proj_b, w_proj_c, w_out, ffn_norm_g, w_group, b_group,
              w_router, b_router, w_gate, w_up, w_down, final_norm_g):
    bsz, seq, d = x.shape
    cos_a, sin_a = rope_tables(seq, A_ROT)
    cos_i, sin_i = rope_tables(seq, IDX_ROT)
    cos_b, sin_b = rope_tables(seq, B_ROPE)
    for l in range(DEPTH):
        h = rmsnorm(x, attn_norm_g[l])
        z = h @ w_in[l]
        (aq, ak, av, aiq, aik, aiw, bcq, bckv, bkr, cq, ck, cv, gl) = split_cols(z, IN_SIZES)
        aq = partial_rope(aq.reshape(bsz, seq, A_HEADS, A_HEAD_DIM), cos_a, sin_a, A_ROT)
        ak = partial_rope(ak, cos_a, sin_a, A_ROT)
        aiq = partial_rope(aiq.reshape(bsz, seq, IDX_HEADS, IDX_DIM), cos_i, sin_i, IDX_ROT)
        aik = partial_rope(aik, cos_i, sin_i, IDX_ROT)
        o_a = dsa_mixer(aq, ak, av, aiq, aik, aiw)
        qb = (rmsnorm(bcq, b_q_norm_g[l]) @ b_w_uq[l]).reshape(bsz, seq, B_HEADS, B_NOPE + B_ROPE)
        q_nope = qb[..., :B_NOPE]
        q_rope = apply_rope(qb[..., B_NOPE:], cos_b, sin_b)
        kvb = (rmsnorm(bckv, b_kv_norm_g[l]) @ b_w_ukv[l]).reshape(bsz, seq, B_HEADS, B_NOPE + B_V)
        k_nope, vb = kvb[..., :B_NOPE], kvb[..., B_NOPE:]
        k_rope = apply_rope(bkr, cos_b, sin_b)
        o_b = mla_mixer(q_nope, q_rope, k_nope, k_rope, vb)
        hs = (bsz, seq, C_HEADS, C_HEAD_DIM)
        o_c = chunk_band_mixer(cq.reshape(hs), ck.reshape(hs), cv.reshape(hs), c_rel_bias[l])
        g = jax.nn.sigmoid(gl.astype(jnp.float32)).reshape(bsz, seq, N_BRANCH, d).astype(x.dtype)
        mix = (g[:, :, 0] * (o_a @ w_proj_a[l]) + g[:, :, 1] * (o_b @ w_proj_b[l])
               + g[:, :, 2] * (o_c @ w_proj_c[l]))
        x = x + mix @ w_out[l]
        x = x + hier_moe(rmsnorm(x, ffn_norm_g[l]), w_group[l], b_group[l], w_router[l], b_router[l],
                         w_gate[l], w_up[l], w_down[l])
    return rmsnorm(x, final_norm_g)
```

```python
import functools

import numpy as np
import jax
import jax.numpy as jnp
from jax import lax
from jax.experimental import pallas as pl
from jax.experimental.pallas import tpu as pltpu

F32 = jnp.float32
BF16 = jnp.bfloat16

LANES = 128
D_MODEL = 1024
SEQ = 2048
CHUNK = 64
Q_BLOCK = 128
ROPE_THETA = 500000.0
EPS = 1e-6

A_HEADS = 4
A_HEAD_DIM = 128
A_ROT = 32
IDX_HEADS = 8
IDX_DIM = 64
IDX_ROT = 16
TOPK = 256
B_HEADS = 4
B_NOPE = 128
B_ROPE = 64
B_V = 128
B_Q_RANK = 256
B_KV_RANK = 128
C_HEADS = 4
C_HEAD_DIM = 128
C_LEFT_CHUNKS = 8
REL_CLIP = 128
N_GROUPS = 4
EXPERTS_PER_GROUP = 8
N_EXPERTS = 32
D_EXPERT = 256

C_KEY_BLOCKS = C_LEFT_CHUNKS * CHUNK // Q_BLOCK + 1
N_QB = SEQ // Q_BLOCK
KV_VARIANTS = 4
KV_STEP = SEQ // KV_VARIANTS

VMEM_LIMIT = 56 * 1024 * 1024

INT_MIN = -2 ** 31
NEG_INF_KEY = int(np.array(0x807FFFFF, np.uint32).view(np.int32))

NT_DIMS = (((1,), (1,)), ((), ()))


def _nt_dot(a, b):
    return lax.dot_general(a, b, NT_DIMS, preferred_element_type=F32)


def _cparams(*sem):
    return pltpu.CompilerParams(dimension_semantics=sem, vmem_limit_bytes=VMEM_LIMIT)


def _rope_table(rot, period, active_lanes=LANES):
    half = rot // 2
    pos = jnp.arange(SEQ, dtype=F32)
    inv = ROPE_THETA ** (-jnp.arange(0, rot, 2, dtype=F32) / rot)
    ang = pos[:, None] * inv[None, :]
    cos, sin = jnp.cos(ang), jnp.sin(ang)
    lane = np.arange(LANES)
    p = lane % period
    first = (p < half) & (lane < active_lanes)
    second = (p >= half) & (p < rot) & (lane < active_lanes)
    idx = np.where(first, p, np.where(second, p - half, 0))
    c = jnp.where(first | second, cos[:, idx], 1.0)
    s_prev = jnp.where(second, sin[:, idx], 0.0)
    s_next = jnp.where(first, -sin[:, idx], 0.0)
    return jnp.stack([c, s_prev, s_next]).astype(F32)


def _norm_proj_kernel(x_ref, g_ref, w_ref, *refs, tile_tab, halves):
    n_tabs = len(halves)
    tab_refs, o_ref = refs[:n_tabs], refs[n_tabs]
    x = x_ref[...].astype(F32)
    ms = jnp.mean(x * x, axis=-1, keepdims=True)
    h = (x * lax.rsqrt(ms + EPS) * g_ref[...]).astype(BF16)
    z = jnp.dot(h, w_ref[...], preferred_element_type=F32)
    for c, t in enumerate(tile_tab):
        zt = z[:, c * LANES:(c + 1) * LANES]
        if t >= 0:
            tab, half = tab_refs[t], halves[t]
            zt = (zt * tab[0] + pltpu.roll(zt, half, 1) * tab[1]
                  + pltpu.roll(zt, LANES - half, 1) * tab[2])
        o_ref[:, c * LANES:(c + 1) * LANES] = zt.astype(o_ref.dtype)


def _norm_proj(x, g, w, out_dtype, *, tm, x_cols=None, tabs=(), halves=(), tile_tab=None):
    t = x.shape[0]
    k, n = w.shape
    col_blk = 0 if x_cols is None else x_cols
    if tile_tab is None:
        tile_tab = (-1,) * (n // LANES)
    seq_tiles = SEQ // tm
    in_specs = [
        pl.BlockSpec((tm, k), lambda i: (i, col_blk)),
        pl.BlockSpec((1, k), lambda i: (0, 0)),
        pl.BlockSpec((k, n), lambda i: (0, 0)),
    ] + [pl.BlockSpec((3, tm, LANES), lambda i: (0, i % seq_tiles, 0)) for _ in tabs]
    return pl.pallas_call(
        functools.partial(_norm_proj_kernel, tile_tab=tuple(tile_tab), halves=tuple(halves)),
        grid=(t // tm,),
        in_specs=in_specs,
        out_specs=pl.BlockSpec((tm, n), lambda i: (i, 0)),
        out_shape=jax.ShapeDtypeStruct((t, n), out_dtype),
        compiler_params=_cparams("parallel"),
    )(x, g.reshape(1, k), w, *tabs)


def _chunk_causal_mask(j, n_keys):
    qpos = j * Q_BLOCK + lax.broadcasted_iota(jnp.int32, (Q_BLOCK, n_keys), 0)
    kpos = lax.broadcasted_iota(jnp.int32, (Q_BLOCK, n_keys), 1)
    return (kpos >> 6) <= (qpos >> 6)


def _dsa_body(q_ref, kv_ref, iq_ref, ikw_ref, o_ref, bias_ref, n_keys):
    j = pl.program_id(1)
    row0 = pl.multiple_of(j * Q_BLOCK, Q_BLOCK)
    iq = iq_ref[...].astype(BF16)
    ik = ikw_ref[0:n_keys, 0:IDX_DIM].astype(BF16)
    iw = ikw_ref[pl.ds(row0, Q_BLOCK), IDX_DIM:IDX_DIM + IDX_HEADS] * (IDX_HEADS ** -0.5)
    score = jnp.zeros((Q_BLOCK, n_keys), F32)
    for h in range(IDX_HEADS):
        r = _nt_dot(iq[:, h * IDX_DIM:(h + 1) * IDX_DIM], ik)
        score = score + jnp.maximum(r * (IDX_DIM ** -0.5), 0.0) * iw[:, h:h + 1]
    allowed = _chunk_causal_mask(j, n_keys)
    score = jnp.where(score == 0.0, 0.0, score)
    score = jnp.where(allowed, score, -jnp.inf)
    bits = lax.bitcast_convert_type(score, jnp.int32)
    key = bits ^ ((bits >> 31) & 0x7FFFFFFF)

    def count_ge(cand):
        return jnp.sum(jnp.where(key >= cand, 1.0, 0.0), axis=1, keepdims=True)

    base = jnp.where(count_ge(jnp.zeros((Q_BLOCK, 1), jnp.int32)) >= TOPK, 0, INT_MIN).astype(jnp.int32)

    def search(i, base):
        cand = base | jnp.left_shift(jnp.int32(1), 30 - i)
        return jnp.where(count_ge(cand) >= TOPK, cand, base)

    thr = lax.fori_loop(0, 31, search, base)
    gt = key > thr
    eq = key == thr
    cnt_gt = jnp.sum(jnp.where(gt, 1.0, 0.0), axis=1, keepdims=True)
    cnt_eq = jnp.sum(jnp.where(eq, 1.0, 0.0), axis=1, keepdims=True)
    tie_rows = (cnt_gt + cnt_eq > TOPK) & (thr > NEG_INF_KEY)
    has_tie = jnp.max(jnp.where(tie_rows, 1.0, 0.0)) > 0.0

    @pl.when(jnp.logical_not(has_tie))
    def _():
        bias_ref[:, 0:n_keys] = jnp.where((gt | eq) & allowed, 0.0, -jnp.inf)

    @pl.when(has_tie)
    def _():
        need = TOPK - cnt_gt
        a = lax.broadcasted_iota(jnp.int32, (LANES, LANES), 0)
        b = lax.broadcasted_iota(jnp.int32, (LANES, LANES), 1)
        upper = jnp.where(a < b, 1.0, 0.0).astype(BF16)
        eqf = jnp.where(eq, 1.0, 0.0)
        carry = jnp.zeros((Q_BLOCK, 1), F32)
        for c in range(n_keys // LANES):
            sl = slice(c * LANES, (c + 1) * LANES)
            e = eqf[:, sl]
            before = jnp.dot(e.astype(BF16), upper, preferred_element_type=F32) + carry
            keep = gt[:, sl] | (eq[:, sl] & (before < need))
            bias_ref[:, sl] = jnp.where(keep & allowed[:, sl], 0.0, -jnp.inf)
            carry = carry + jnp.sum(e, axis=1, keepdims=True)

    q = q_ref[...]
    qs = jnp.concatenate([q[:, h * A_HEAD_DIM:(h + 1) * A_HEAD_DIM] for h in range(A_HEADS)], axis=0)
    k = kv_ref[0:n_keys, 0:A_HEAD_DIM]
    v = kv_ref[0:n_keys, A_HEAD_DIM:2 * A_HEAD_DIM]
    bias = bias_ref[:, 0:n_keys]
    for h in range(A_HEADS):
        logits = _nt_dot(qs[h * Q_BLOCK:(h + 1) * Q_BLOCK], k) * (A_HEAD_DIM ** -0.5) + bias
        m = jnp.max(logits, axis=-1, keepdims=True)
        p = jnp.exp(logits - m)
        l = jnp.sum(p, axis=-1, keepdims=True)
        o = jnp.dot(p.astype(BF16), v, preferred_element_type=F32) / l
        o_ref[:, h * A_HEAD_DIM:(h + 1) * A_HEAD_DIM] = o.astype(o_ref.dtype)


def _dsa_kernel(q_ref, kv_ref, iq_ref, ikw_ref, o_ref, bias_ref):
    j = pl.program_id(1)
    for v in range(KV_VARIANTS):
        @pl.when(j // (N_QB // KV_VARIANTS) == v)
        def _(v=v):
            _dsa_body(q_ref, kv_ref, iq_ref, ikw_ref, o_ref, bias_ref, KV_STEP * (v + 1))


def _dsa_attention(za, zi, bsz):
    return pl.pallas_call(
        _dsa_kernel,
        grid=(bsz, N_QB),
        in_specs=[
            pl.BlockSpec((None, Q_BLOCK, 512), lambda b, j: (b, j, 0)),
            pl.BlockSpec((None, SEQ, 256), lambda b, j: (b, 0, 2)),
            pl.BlockSpec((None, Q_BLOCK, 512), lambda b, j: (b, j, 0)),
            pl.BlockSpec((None, SEQ, LANES), lambda b, j: (b, 0, 4)),
        ],
        out_specs=pl.BlockSpec((None, Q_BLOCK, 512), lambda b, j: (b, j, 0)),
        out_shape=jax.ShapeDtypeStruct((bsz, SEQ, 512), BF16),
        scratch_shapes=[pltpu.VMEM((Q_BLOCK, SEQ), F32)],
        compiler_params=_cparams("parallel", "arbitrary"),
    )(za, za, zi, zi)


def _mla_body(q_ref, kv_ref, zb_ref, o_ref, n_keys):
    j = pl.program_id(1)
    mask = _chunk_causal_mask(j, n_keys)
    kr = zb_ref[0:n_keys, 0:B_ROPE].astype(BF16)
    scale = (B_NOPE + B_ROPE) ** -0.5
    for h in range(B_HEADS):
        qn = q_ref[:, h * B_NOPE:(h + 1) * B_NOPE]
        qr = q_ref[:, B_HEADS * B_NOPE + h * B_ROPE:B_HEADS * B_NOPE + (h + 1) * B_ROPE]
        kn = kv_ref[0:n_keys, h * B_NOPE:(h + 1) * B_NOPE]
        vv = kv_ref[0:n_keys, B_HEADS * B_NOPE + h * B_V:B_HEADS * B_NOPE + (h + 1) * B_V]
        s = (_nt_dot(qn, kn) + _nt_dot(qr, kr)) * scale
        s = jnp.where(mask, s, -jnp.inf)
        m = jnp.max(s, axis=-1, keepdims=True)
        p = jnp.exp(s - m)
        l = jnp.sum(p, axis=-1, keepdims=True)
        o = jnp.dot(p.astype(BF16), vv, preferred_element_type=F32) / l
        o_ref[:, h * B_V:(h + 1) * B_V] = o.astype(o_ref.dtype)


def _mla_kernel(q_ref, kv_ref, zb_ref, o_ref):
    j = pl.program_id(1)
    for v in range(KV_VARIANTS):
        @pl.when(j // (N_QB // KV_VARIANTS) == v)
        def _(v=v):
            _mla_body(q_ref, kv_ref, zb_ref, o_ref, KV_STEP * (v + 1))


def _mla_attention(qb, kvb, zb, bsz):
    return pl.pallas_call(
        _mla_kernel,
        grid=(bsz, N_QB),
        in_specs=[
            pl.BlockSpec((None, Q_BLOCK, 768), lambda b, j: (b, j, 0)),
            pl.BlockSpec((None, SEQ, 1024), lambda b, j: (b, 0, 0)),
            pl.BlockSpec((None, SEQ, LANES), lambda b, j: (b, 0, 3)),
        ],
        out_specs=pl.BlockSpec((None, Q_BLOCK, 512), lambda b, j: (b, j, 0)),
        out_shape=jax.ShapeDtypeStruct((bsz, SEQ, 512), BF16),
        compiler_params=_cparams("parallel", "arbitrary"),
    )(qb, kvb, zb)


def _band_bias(rel_table):
    qi = np.arange(Q_BLOCK)[:, None]
    kj = np.arange(Q_BLOCK)[None, :]
    out = []
    for d in range(C_KEY_BLOCKS):
        dist = d * Q_BLOCK + qi - kj
        cdiff = 2 * d + qi // CHUNK - kj // CHUNK
        valid = (cdiff >= 0) & (cdiff <= C_LEFT_CHUNKS)
        bias = rel_table[:, np.clip(dist, -REL_CLIP, REL_CLIP) + REL_CLIP].astype(F32)
        out.append(jnp.where(valid[None], bias, -jnp.inf))
    return jnp.stack(out, axis=1)


def _band_kernel(q_ref, kv_ref, bias_ref, o_ref):
    j = pl.program_id(1)
    scale = C_HEAD_DIM ** -0.5
    for h in range(C_HEADS):
        q = q_ref[:, h * C_HEAD_DIM:(h + 1) * C_HEAD_DIM]
        ss, vs = [], []
        for d in range(C_KEY_BLOCKS):
            kb = j - d
            row0 = pl.multiple_of(jnp.maximum(kb, 0) * Q_BLOCK, Q_BLOCK)
            k = kv_ref[pl.ds(row0, Q_BLOCK), h * C_HEAD_DIM:(h + 1) * C_HEAD_DIM]
            vs.append(kv_ref[pl.ds(row0, Q_BLOCK), (C_HEADS + h) * C_HEAD_DIM:(C_HEADS + h + 1) * C_HEAD_DIM])
            s = _nt_dot(q, k) * scale + bias_ref[h, d]
            ss.append(jnp.where(kb >= 0, s, -jnp.inf))
        m = ss[0].max(axis=-1, keepdims=True)
        for s in ss[1:]:
            m = jnp.maximum(m, s.max(axis=-1, keepdims=True))
        l = jnp.zeros((Q_BLOCK, 1), F32)
        o = jnp.zeros((Q_BLOCK, C_HEAD_DIM), F32)
        for s, v in zip(ss, vs):
            p = jnp.exp(s - m)
            l = l + jnp.sum(p, axis=-1, keepdims=True)
            o = o + jnp.dot(p.astype(BF16), v, preferred_element_type=F32)
        o_ref[:, h * C_HEAD_DIM:(h + 1) * C_HEAD_DIM] = (o / l).astype(o_ref.dtype)


def _band_attention(zc, bias, bsz):
    return pl.pallas_call(
        _band_kernel,
        grid=(bsz, N_QB),
        in_specs=[
            pl.BlockSpec((None, Q_BLOCK, 512), lambda b, j: (b, j, 2)),
            pl.BlockSpec((None, SEQ, 1024), lambda b, j: (b, 0, 0)),
            pl.BlockSpec((C_HEADS, C_KEY_BLOCKS, Q_BLOCK, Q_BLOCK), lambda b, j: (0, 0, 0, 0)),
        ],
        out_specs=pl.BlockSpec((None, Q_BLOCK, 512), lambda b, j: (b, j, 0)),
        out_shape=jax.ShapeDtypeStruct((bsz, SEQ, 512), BF16),
        compiler_params=_cparams("parallel", "arbitrary"),
    )(zc, zc, bias)


def _merge_kernel(x_ref, g_ref, wgl_ref, oa_ref, ob_ref, oc_ref, wpa_ref, wpb_ref, wpc_ref, wout_ref, o_ref):
    x = x_ref[...]
    ms = jnp.mean(x * x, axis=-1, keepdims=True)
    h = (x * lax.rsqrt(ms + EPS) * g_ref[...]).astype(BF16)
    mix = jnp.zeros(x.shape, F32)
    for i, (o_in, wp) in enumerate(((oa_ref, wpa_ref), (ob_ref, wpb_ref), (oc_ref, wpc_ref))):
        gl = jnp.dot(h, wgl_ref[:, i * D_MODEL:(i + 1) * D_MODEL], preferred_element_type=F32)
        gate = jax.nn.sigmoid(gl)
        mix = mix + gate * jnp.dot(o_in[...], wp[...], preferred_element_type=F32)
    o_ref[...] = x + jnp.dot(mix.astype(BF16), wout_ref[...], preferred_element_type=F32)


def _merge(x, g, wgl, oa, ob, oc, wpa, wpb, wpc, wout, *, tm):
    t, d = x.shape
    row = lambda i: (i, 0)
    fixed = lambda i: (0, 0)
    return pl.pallas_call(
        _merge_kernel,
        grid=(t // tm,),
        in_specs=[
            pl.BlockSpec((tm, d), row),
            pl.BlockSpec((1, d), fixed),
            pl.BlockSpec((d, 3 * d), fixed),
            pl.BlockSpec((tm, 512), row),
            pl.BlockSpec((tm, 512), row),
            pl.BlockSpec((tm, 512), row),
            pl.BlockSpec((512, d), fixed),
            pl.BlockSpec((512, d), fixed),
            pl.BlockSpec((512, d), fixed),
            pl.BlockSpec((d, d), fixed),
        ],
        out_specs=pl.BlockSpec((tm, d), row),
        out_shape=jax.ShapeDtypeStruct((t, d), F32),
        compiler_params=_cparams("parallel"),
    )(x, g.reshape(1, d), wgl, oa, ob, oc, wpa, wpb, wpc, wout)


def _first_argmax(vals, lane):
    m = jnp.max(vals, axis=-1, keepdims=True)
    idx = jnp.min(jnp.where(vals == m, lane, LANES), axis=-1, keepdims=True)
    return m, idx


def _router_kernel(x_ref, g_ref, w_ref, b_ref, h_ref, comb_ref):
    x = x_ref[...]
    ms = jnp.mean(x * x, axis=-1, keepdims=True)
    h = x * lax.rsqrt(ms + EPS) * g_ref[...]
    h_ref[...] = h.astype(BF16)
    logits = jnp.dot(h, w_ref[...], preferred_element_type=F32, precision=lax.Precision.HIGHEST) + b_ref[...]
    lane = lax.broadcasted_iota(jnp.int32, logits.shape, 1)
    is_grp = (lane >= N_EXPERTS) & (lane < N_EXPERTS + N_GROUPS)
    gl = jnp.where(is_grp, logits, -jnp.inf)
    gmax, gidx = _first_argmax(gl, lane)
    pg = 1.0 / jnp.sum(jnp.exp(gl - gmax), axis=-1, keepdims=True)
    gsel = gidx - N_EXPERTS
    in_grp = (lane >> 3) == gsel
    el = jnp.where(in_grp, logits, -jnp.inf)
    m1, i1 = _first_argmax(el, lane)
    z = jnp.sum(jnp.exp(el - m1), axis=-1, keepdims=True)
    el2 = jnp.where(lane == i1, -jnp.inf, el)
    m2, i2 = _first_argmax(el2, lane)
    pe1 = 1.0 / z
    pe2 = jnp.exp(m2 - m1) / z
    den = pe1 + pe2
    comb_ref[...] = (jnp.where(lane == i1, pg * pe1 / den, 0.0)
                     + jnp.where(lane == i2, pg * pe2 / den, 0.0))


def _router(x, g, w, b, *, tm):
    t, d = x.shape
    return pl.pallas_call(
        _router_kernel,
        grid=(t // tm,),
        in_specs=[
            pl.BlockSpec((tm, d), lambda i: (i, 0)),
            pl.BlockSpec((1, d), lambda i: (0, 0)),
            pl.BlockSpec((d, LANES), lambda i: (0, 0)),
            pl.BlockSpec((1, LANES), lambda i: (0, 0)),
        ],
        out_specs=[
            pl.BlockSpec((tm, d), lambda i: (i, 0)),
            pl.BlockSpec((tm, LANES), lambda i: (i, 0)),
        ],
        out_shape=[
            jax.ShapeDtypeStruct((t, d), BF16),
            jax.ShapeDtypeStruct((t, LANES), F32),
        ],
        compiler_params=_cparams("parallel"),
    )(x, g.reshape(1, d), w, b)


def _experts_kernel(x_ref, h_ref, comb_ref, wg_ref, wu_ref, wd_ref, o_ref, acc_ref):
    e = pl.program_id(1)

    @pl.when(e == 0)
    def _():
        acc_ref[...] = jnp.zeros_like(acc_ref)

    h = h_ref[...]
    a = jnp.dot(h, wg_ref[...], preferred_element_type=F32)
    u = jnp.dot(h, wu_ref[...], preferred_element_type=F32)
    comb = comb_ref[...]
    lane = lax.broadcasted_iota(jnp.int32, comb.shape, 1)
    c = jnp.sum(jnp.where(lane == e, comb, 0.0), axis=-1, keepdims=True)
    hh = (a * jax.nn.sigmoid(a)) * u * c
    acc_ref[...] += jnp.dot(hh.astype(BF16), wd_ref[...], preferred_element_type=F32)

    @pl.when(e == pl.num_programs(1) - 1)
    def _():
        o_ref[...] = x_ref[...] + acc_ref[...]


def _experts(x, h, comb, wg, wu, wd, *, tm):
    t, d = x.shape
    return pl.pallas_call(
        _experts_kernel,
        grid=(t // tm, N_EXPERTS),
        in_specs=[
            pl.BlockSpec((tm, d), lambda i, e: (i, 0)),
            pl.BlockSpec((tm, d), lambda i, e: (i, 0)),
            pl.BlockSpec((tm, LANES), lambda i, e: (i, 0)),
            pl.BlockSpec((None, d, D_EXPERT), lambda i, e: (e, 0, 0)),
            pl.BlockSpec((None, d, D_EXPERT), lambda i, e: (e, 0, 0)),
            pl.BlockSpec((None, D_EXPERT, d), lambda i, e: (e, 0, 0)),
        ],
        out_specs=pl.BlockSpec((tm, d), lambda i, e: (i, 0)),
        out_shape=jax.ShapeDtypeStruct((t, d), F32),
        scratch_shapes=[pltpu.VMEM((tm, d), F32)],
        compiler_params=_cparams("parallel", "arbitrary"),
    )(x, h, comb, wg, wu, wd)


def _final_norm_kernel(x_ref, g_ref, o_ref):
    x = x_ref[...]
    ms = jnp.mean(x * x, axis=-1, keepdims=True)
    o_ref[...] = x * lax.rsqrt(ms + EPS) * g_ref[...]


def _final_norm(x, g, *, tm):
    t, d = x.shape
    return pl.pallas_call(
        _final_norm_kernel,
        grid=(t // tm,),
        in_specs=[pl.BlockSpec((tm, d), lambda i: (i, 0)), pl.BlockSpec((1, d), lambda i: (0, 0))],
        out_specs=pl.BlockSpec((tm, d), lambda i: (i, 0)),
        out_shape=jax.ShapeDtypeStruct((t, d), F32),
        compiler_params=_cparams("parallel"),
    )(x, g.reshape(1, d))


def _pad_cols(w, n):
    return jnp.pad(w, ((0, 0), (0, n - w.shape[1])))


def kernel(x, attn_norm_g, w_in, b_q_norm_g, b_w_uq, b_kv_norm_g, b_w_ukv, c_rel_bias, w_proj_a, w_proj_b, w_proj_c, w_out, ffn_norm_g, w_group, b_group, w_router, b_router, w_gate, w_up, w_down, final_norm_g):
    bsz, seq, d = x.shape
    assert (seq, d) == (SEQ, D_MODEL)
    t = bsz * seq
    depth = w_in.shape[0]
    tm = 512

    tab_a = _rope_table(A_ROT, A_HEAD_DIM)
    tab_i = _rope_table(IDX_ROT, IDX_DIM)
    tab_i_half = _rope_table(IDX_ROT, IDX_DIM, active_lanes=IDX_DIM)
    tab_b = _rope_table(B_ROPE, B_ROPE)
    tab_b_half = _rope_table(B_ROPE, B_ROPE, active_lanes=B_ROPE)

    xf = x.reshape(t, d)
    for l in range(depth):
        w = w_in[l]
        w_a = w[:, 0:768].astype(BF16)
        w_i = _pad_cols(w[:, 768:1352], 640).astype(BF16)
        w_b = _pad_cols(w[:, 1352:1800], 512).astype(BF16)
        w_c = jnp.concatenate([w[:, 2312:3336], w[:, 1800:2312]], axis=1).astype(BF16)
        w_g = w[:, 3336:6408].astype(BF16)
        g_attn = attn_norm_g[l]

        za = _norm_proj(xf, g_attn, w_a, BF16, tm=tm, tabs=(tab_a,), halves=(A_ROT // 2,),
                        tile_tab=(0, 0, 0, 0, 0, -1))
        zi = _norm_proj(xf, g_attn, w_i, F32, tm=tm, tabs=(tab_i, tab_i_half), halves=(IDX_ROT // 2,) * 2,
                        tile_tab=(0, 0, 0, 0, 1))
        zb = _norm_proj(xf, g_attn, w_b, F32, tm=tm, tabs=(tab_b_half,), halves=(B_ROPE // 2,),
                        tile_tab=(-1, -1, -1, 0))
        zc = _norm_proj(xf, g_attn, w_c, BF16, tm=tm)

        w_uq = b_w_uq[l].reshape(B_Q_RANK, B_HEADS, B_NOPE + B_ROPE)
        w_uq = jnp.concatenate([w_uq[:, :, :B_NOPE].reshape(B_Q_RANK, -1),
                                w_uq[:, :, B_NOPE:].reshape(B_Q_RANK, -1)], axis=1).astype(BF16)
        w_ukv = b_w_ukv[l].reshape(B_KV_RANK, B_HEADS, B_NOPE + B_V)
        w_ukv = jnp.concatenate([w_ukv[:, :, :B_NOPE].reshape(B_KV_RANK, -1),
                                 w_ukv[:, :, B_NOPE:].reshape(B_KV_RANK, -1)], axis=1).astype(BF16)
        qb = _norm_proj(zb, b_q_norm_g[l], w_uq, BF16, tm=tm, x_cols=0, tabs=(tab_b,),
                        halves=(B_ROPE // 2,), tile_tab=(-1, -1, -1, -1, 0, 0))
        kvb = _norm_proj(zb, b_kv_norm_g[l], w_ukv, BF16, tm=tm, x_cols=2)

        o_a = _dsa_attention(za.reshape(bsz, seq, -1), zi.reshape(bsz, seq, -1), bsz)
        o_b = _mla_attention(qb.reshape(bsz, seq, -1), kvb.reshape(bsz, seq, -1), zb.reshape(bsz, seq, -1), bsz)
        o_c = _band_attention(zc.reshape(bsz, seq, -1), _band_bias(c_rel_bias[l]), bsz)

        xf = _merge(xf, g_attn, w_g, o_a.reshape(t, -1), o_b.reshape(t, -1), o_c.reshape(t, -1),
                    w_proj_a[l].astype(BF16), w_proj_b[l].astype(BF16), w_proj_c[l].astype(BF16),
                    w_out[l].astype(BF16), tm=tm)

        w_r = _pad_cols(jnp.concatenate([w_router[l], w_group[l]], axis=1), LANES)
        b_r = _pad_cols(jnp.concatenate([b_router[l], b_group[l]])[None, :], LANES)
        hn, comb = _router(xf, ffn_norm_g[l], w_r, b_r, tm=tm)
        xf = _experts(xf, hn, comb, w_gate[l].astype(BF16), w_up[l].astype(BF16), w_down[l].astype(BF16), tm=1024)

    return _final_norm(xf, final_norm_g, tm=tm).reshape(bsz, seq, d)
```

```python
import functools

import numpy as np
import jax
import jax.numpy as jnp
from jax import lax
from jax.experimental import pallas as pl
from jax.experimental.pallas import tpu as pltpu

F32 = jnp.float32
BF16 = jnp.bfloat16

LANES = 128
D_MODEL = 1024
SEQ = 2048
CHUNK = 64
Q_BLOCK = 128
ROPE_THETA = 500000.0
EPS = 1e-6

A_HEADS = 4
A_HEAD_DIM = 128
A_ROT = 32
IDX_HEADS = 8
IDX_DIM = 64
IDX_ROT = 16
TOPK = 256
B_HEADS = 4
B_NOPE = 128
B_ROPE = 64
B_V = 128
B_Q_RANK = 256
B_KV_RANK = 128
C_HEADS = 4
C_HEAD_DIM = 128
C_LEFT_CHUNKS = 8
REL_CLIP = 128
N_GROUPS = 4
EXPERTS_PER_GROUP = 8
N_EXPERTS = 32
D_EXPERT = 256

C_KEY_BLOCKS = C_LEFT_CHUNKS * CHUNK // Q_BLOCK + 1
N_QB = SEQ // Q_BLOCK
KV_VARIANTS = 4
KV_STEP = SEQ // KV_VARIANTS

VMEM_LIMIT = 56 * 1024 * 1024

MOE_TILE = 2048
EXPERT_CHUNK = 128
XS_ROWS = 2 * MOE_TILE + 3 * EXPERT_CHUNK
assert XS_ROWS >= 2 * MOE_TILE + N_EXPERTS * 7 + EXPERT_CHUNK - 1
COMBINE_ROWS = 512
HIGH_HALF = -65536

INT_MIN = -2 ** 31
NEG_INF_KEY = int(np.array(0x807FFFFF, np.uint32).view(np.int32))

NT_DIMS = (((1,), (1,)), ((), ()))


def _nt_dot(a, b):
    return lax.dot_general(a, b, NT_DIMS, preferred_element_type=F32)


def _cparams(*sem):
    return pltpu.CompilerParams(dimension_semantics=sem, vmem_limit_bytes=VMEM_LIMIT)


def _rope_table(rot, period, active_lanes=LANES):
    half = rot // 2
    lane = np.arange(LANES)
    p = lane % period
    first = (p < half) & (lane < active_lanes)
    second = (p >= half) & (p < rot) & (lane < active_lanes)
    idx = np.where(first, p, np.where(second, p - half, 0))
    pos = jnp.arange(SEQ, dtype=F32)
    inv = ROPE_THETA ** (-jnp.arange(0, rot, 2, dtype=F32) / rot)
    ang = pos[:, None] * inv[idx][None, :]
    cos, sin = jnp.cos(ang), jnp.sin(ang)
    c = jnp.where(first | second, cos, 1.0)
    s_prev = jnp.where(second, sin, 0.0)
    s_next = jnp.where(first, -sin, 0.0)
    return jnp.stack([c, s_prev, s_next]).astype(F32)


def _norm_proj_kernel(x_ref, g_ref, w_ref, *refs, tile_tab, halves):
    n_tabs = len(halves)
    tab_refs, o_ref = refs[:n_tabs], refs[n_tabs]
    x = x_ref[...].astype(F32)
    ms = jnp.mean(x * x, axis=-1, keepdims=True)
    h = (x * lax.rsqrt(ms + EPS) * g_ref[...]).astype(BF16)
    z = jnp.dot(h, w_ref[...], preferred_element_type=F32)
    for c, t in enumerate(tile_tab):
        zt = z[:, c * LANES:(c + 1) * LANES]
        if t >= 0:
            tab, half = tab_refs[t], halves[t]
            zt = (zt * tab[0] + pltpu.roll(zt, half, 1) * tab[1]
                  + pltpu.roll(zt, LANES - half, 1) * tab[2])
        o_ref[:, c * LANES:(c + 1) * LANES] = zt.astype(o_ref.dtype)


def _norm_proj(x, g, w, out_dtype, *, tm, x_cols=None, tabs=(), halves=(), tile_tab=None):
    t = x.shape[0]
    k, n = w.shape
    col_blk = 0 if x_cols is None else x_cols
    if tile_tab is None:
        tile_tab = (-1,) * (n // LANES)
    seq_tiles = SEQ // tm
    in_specs = [
        pl.BlockSpec((tm, k), lambda i: (i, col_blk)),
        pl.BlockSpec((1, k), lambda i: (0, 0)),
        pl.BlockSpec((k, n), lambda i: (0, 0)),
    ] + [pl.BlockSpec((3, tm, LANES), lambda i: (0, i % seq_tiles, 0)) for _ in tabs]
    return pl.pallas_call(
        functools.partial(_norm_proj_kernel, tile_tab=tuple(tile_tab), halves=tuple(halves)),
        grid=(t // tm,),
        in_specs=in_specs,
        out_specs=pl.BlockSpec((tm, n), lambda i: (i, 0)),
        out_shape=jax.ShapeDtypeStruct((t, n), out_dtype),
        compiler_params=_cparams("parallel"),
    )(x, g.reshape(1, k), w, *tabs)


def _chunk_causal_mask(j, n_keys):
    qpos = j * Q_BLOCK + lax.broadcasted_iota(jnp.int32, (Q_BLOCK, n_keys), 0)
    kpos = lax.broadcasted_iota(jnp.int32, (Q_BLOCK, n_keys), 1)
    return (kpos >> 6) <= (qpos >> 6)


def _dsa_body(q_ref, kv_ref, iq_ref, ikw_ref, o_ref, bias_ref, n_keys):
    j = pl.program_id(1)
    row0 = pl.multiple_of(j * Q_BLOCK, Q_BLOCK)
    iq = iq_ref[...].astype(BF16)
    ik = ikw_ref[0:n_keys, 0:IDX_DIM].astype(BF16)
    iw = ikw_ref[pl.ds(row0, Q_BLOCK), IDX_DIM:IDX_DIM + IDX_HEADS] * (IDX_HEADS ** -0.5)
    score = jnp.zeros((Q_BLOCK, n_keys), F32)
    for h in range(IDX_HEADS):
        r = _nt_dot(iq[:, h * IDX_DIM:(h + 1) * IDX_DIM], ik)
        score = score + jnp.maximum(r * (IDX_DIM ** -0.5), 0.0) * iw[:, h:h + 1]
    allowed = _chunk_causal_mask(j, n_keys)
    score = jnp.where(score == 0.0, 0.0, score)
    score = jnp.where(allowed, score, -jnp.inf)
    bits = lax.bitcast_convert_type(score, jnp.int32)
    key = bits ^ ((bits >> 31) & 0x7FFFFFFF)

    def count_ge(cand):
        return jnp.sum(jnp.where(key >= cand, 1.0, 0.0), axis=1, keepdims=True)

    base = jnp.where(count_ge(jnp.zeros((Q_BLOCK, 1), jnp.int32)) >= TOPK, 0, INT_MIN).astype(jnp.int32)

    def search(i, base):
        cand = base | jnp.left_shift(jnp.int32(1), 30 - i)
        return jnp.where(count_ge(cand) >= TOPK, cand, base)

    thr = lax.fori_loop(0, 31, search, base)
    gt = key > thr
    eq = key == thr
    cnt_gt = jnp.sum(jnp.where(gt, 1.0, 0.0), axis=1, keepdims=True)
    cnt_eq = jnp.sum(jnp.where(eq, 1.0, 0.0), axis=1, keepdims=True)
    tie_rows = (cnt_gt + cnt_eq > TOPK) & (thr > NEG_INF_KEY)
    has_tie = jnp.max(jnp.where(tie_rows, 1.0, 0.0)) > 0.0

    @pl.when(jnp.logical_not(has_tie))
    def _():
        bias_ref[:, 0:n_keys] = jnp.where((gt | eq) & allowed, 0.0, -jnp.inf)

    @pl.when(has_tie)
    def _():
        need = TOPK - cnt_gt
        a = lax.broadcasted_iota(jnp.int32, (LANES, LANES), 0)
        b = lax.broadcasted_iota(jnp.int32, (LANES, LANES), 1)
        upper = jnp.where(a < b, 1.0, 0.0).astype(BF16)
        eqf = jnp.where(eq, 1.0, 0.0)
        carry = jnp.zeros((Q_BLOCK, 1), F32)
        for c in range(n_keys // LANES):
            sl = slice(c * LANES, (c + 1) * LANES)
            e = eqf[:, sl]
            before = jnp.dot(e.astype(BF16), upper, preferred_element_type=F32) + carry
            keep = gt[:, sl] | (eq[:, sl] & (before < need))
            bias_ref[:, sl] = jnp.where(keep & allowed[:, sl], 0.0, -jnp.inf)
            carry = carry + jnp.sum(e, axis=1, keepdims=True)

    q = q_ref[...]
    qs = jnp.concatenate([q[:, h * A_HEAD_DIM:(h + 1) * A_HEAD_DIM] for h in range(A_HEADS)], axis=0)
    k = kv_ref[0:n_keys, 0:A_HEAD_DIM]
    v = kv_ref[0:n_keys, A_HEAD_DIM:2 * A_HEAD_DIM]
    bias = bias_ref[:, 0:n_keys]
    for h in range(A_HEADS):
        logits = _nt_dot(qs[h * Q_BLOCK:(h + 1) * Q_BLOCK], k) * (A_HEAD_DIM ** -0.5) + bias
        m = jnp.max(logits, axis=-1, keepdims=True)
        p = jnp.exp(logits - m)
        l = jnp.sum(p, axis=-1, keepdims=True)
        o = jnp.dot(p.astype(BF16), v, preferred_element_type=F32) / l
        o_ref[:, h * A_HEAD_DIM:(h + 1) * A_HEAD_DIM] = o.astype(o_ref.dtype)


def _dsa_kernel(q_ref, kv_ref, iq_ref, ikw_ref, o_ref, bias_ref):
    j = pl.program_id(1)
    for v in range(KV_VARIANTS):
        @pl.when(j // (N_QB // KV_VARIANTS) == v)
        def _(v=v):
            _dsa_body(q_ref, kv_ref, iq_ref, ikw_ref, o_ref, bias_ref, KV_STEP * (v + 1))


def _dsa_attention(za, zi, bsz):
    return pl.pallas_call(
        _dsa_kernel,
        grid=(bsz, N_QB),
        in_specs=[
            pl.BlockSpec((None, Q_BLOCK, 512), lambda b, j: (b, j, 0)),
            pl.BlockSpec((None, SEQ, 256), lambda b, j: (b, 0, 2)),
            pl.BlockSpec((None, Q_BLOCK, 512), lambda b, j: (b, j, 0)),
            pl.BlockSpec((None, SEQ, LANES), lambda b, j: (b, 0, 4)),
        ],
        out_specs=pl.BlockSpec((None, Q_BLOCK, 512), lambda b, j: (b, j, 0)),
        out_shape=jax.ShapeDtypeStruct((bsz, SEQ, 512), BF16),
        scratch_shapes=[pltpu.VMEM((Q_BLOCK, SEQ), F32)],
        compiler_params=_cparams("parallel", "arbitrary"),
    )(za, za, zi, zi)


def _mla_body(q_ref, kv_ref, zb_ref, o_ref, n_keys):
    j = pl.program_id(1)
    mask = _chunk_causal_mask(j, n_keys)
    kr = zb_ref[0:n_keys, 0:B_ROPE].astype(BF16)
    scale = (B_NOPE + B_ROPE) ** -0.5
    for h in range(B_HEADS):
        qn = q_ref[:, h * B_NOPE:(h + 1) * B_NOPE]
        qr = q_ref[:, B_HEADS * B_NOPE + h * B_ROPE:B_HEADS * B_NOPE + (h + 1) * B_ROPE]
        kn = kv_ref[0:n_keys, h * B_NOPE:(h + 1) * B_NOPE]
        vv = kv_ref[0:n_keys, B_HEADS * B_NOPE + h * B_V:B_HEADS * B_NOPE + (h + 1) * B_V]
        s = (_nt_dot(qn, kn) + _nt_dot(qr, kr)) * scale
        s = jnp.where(mask, s, -jnp.inf)
        m = jnp.max(s, axis=-1, keepdims=True)
        p = jnp.exp(s - m)
        l = jnp.sum(p, axis=-1, keepdims=True)
        o = jnp.dot(p.astype(BF16), vv, preferred_element_type=F32) / l
        o_ref[:, h * B_V:(h + 1) * B_V] = o.astype(o_ref.dtype)


def _mla_kernel(q_ref, kv_ref, zb_ref, o_ref):
    j = pl.program_id(1)
    for v in range(KV_VARIANTS):
        @pl.when(j // (N_QB // KV_VARIANTS) == v)
        def _(v=v):
            _mla_body(q_ref, kv_ref, zb_ref, o_ref, KV_STEP * (v + 1))


def _mla_attention(qb, kvb, zb, bsz):
    return pl.pallas_call(
        _mla_kernel,
        grid=(bsz, N_QB),
        in_specs=[
            pl.BlockSpec((None, Q_BLOCK, 768), lambda b, j: (b, j, 0)),
            pl.BlockSpec((None, SEQ, 1024), lambda b, j: (b, 0, 0)),
            pl.BlockSpec((None, SEQ, LANES), lambda b, j: (b, 0, 3)),
        ],
        out_specs=pl.BlockSpec((None, Q_BLOCK, 512), lambda b, j: (b, j, 0)),
        out_shape=jax.ShapeDtypeStruct((bsz, SEQ, 512), BF16),
        compiler_params=_cparams("parallel", "arbitrary"),
    )(qb, kvb, zb)


def _band_bias(rel_table):
    qi = np.arange(Q_BLOCK)[:, None]
    kj = np.arange(Q_BLOCK)[None, :]
    out = []
    for d in range(C_KEY_BLOCKS):
        dist = d * Q_BLOCK + qi - kj
        cdiff = 2 * d + qi // CHUNK - kj // CHUNK
        valid = (cdiff >= 0) & (cdiff <= C_LEFT_CHUNKS)
        bias = rel_table[:, np.clip(dist, -REL_CLIP, REL_CLIP) + REL_CLIP].astype(F32)
        out.append(jnp.where(valid[None], bias, -jnp.inf))
    return jnp.stack(out, axis=1)


def _band_kernel(q_ref, kv_ref, bias_ref, o_ref):
    j = pl.program_id(1)
    scale = C_HEAD_DIM ** -0.5
    for h in range(C_HEADS):
        q = q_ref[:, h * C_HEAD_DIM:(h + 1) * C_HEAD_DIM]
        ss, vs = [], []
        for d in range(C_KEY_BLOCKS):
            kb = j - d
            row0 = pl.multiple_of(jnp.maximum(kb, 0) * Q_BLOCK, Q_BLOCK)
            k = kv_ref[pl.ds(row0, Q_BLOCK), h * C_HEAD_DIM:(h + 1) * C_HEAD_DIM]
            vs.append(kv_ref[pl.ds(row0, Q_BLOCK), (C_HEADS + h) * C_HEAD_DIM:(C_HEADS + h + 1) * C_HEAD_DIM])
            s = _nt_dot(q, k) * scale + bias_ref[h, d]
            ss.append(jnp.where(kb >= 0, s, -jnp.inf))
        m = ss[0].max(axis=-1, keepdims=True)
        for s in ss[1:]:
            m = jnp.maximum(m, s.max(axis=-1, keepdims=True))
        l = jnp.zeros((Q_BLOCK, 1), F32)
        o = jnp.zeros((Q_BLOCK, C_HEAD_DIM), F32)
        for s, v in zip(ss, vs):
            p = jnp.exp(s - m)
            l = l + jnp.sum(p, axis=-1, keepdims=True)
            o = o + jnp.dot(p.astype(BF16), v, preferred_element_type=F32)
        o_ref[:, h * C_HEAD_DIM:(h + 1) * C_HEAD_DIM] = (o / l).astype(o_ref.dtype)


def _band_attention(zc, bias, bsz):
    return pl.pallas_call(
        _band_kernel,
        grid=(bsz, N_QB),
        in_specs=[
            pl.BlockSpec((None, Q_BLOCK, 512), lambda b, j: (b, j, 2)),
            pl.BlockSpec((None, SEQ, 1024), lambda b, j: (b, 0, 0)),
            pl.BlockSpec((C_HEADS, C_KEY_BLOCKS, Q_BLOCK, Q_BLOCK), lambda b, j: (0, 0, 0, 0)),
        ],
        out_specs=pl.BlockSpec((None, Q_BLOCK, 512), lambda b, j: (b, j, 0)),
        out_shape=jax.ShapeDtypeStruct((bsz, SEQ, 512), BF16),
        compiler_params=_cparams("parallel", "arbitrary"),
    )(zc, zc, bias)


def _merge_kernel(x_ref, g_ref, wgl_ref, oa_ref, ob_ref, oc_ref, wpa_ref, wpb_ref, wpc_ref, wout_ref, o_ref):
    x = x_ref[...]
    ms = jnp.mean(x * x, axis=-1, keepdims=True)
    h = (x * lax.rsqrt(ms + EPS) * g_ref[...]).astype(BF16)
    mix = jnp.zeros(x.shape, F32)
    for i, (o_in, wp) in enumerate(((oa_ref, wpa_ref), (ob_ref, wpb_ref), (oc_ref, wpc_ref))):
        gl = jnp.dot(h, wgl_ref[:, i * D_MODEL:(i + 1) * D_MODEL], preferred_element_type=F32)
        gate = jax.nn.sigmoid(gl)
        mix = mix + gate * jnp.dot(o_in[...], wp[...], preferred_element_type=F32)
    o_ref[...] = x + jnp.dot(mix.astype(BF16), wout_ref[...], preferred_element_type=F32)


def _merge(x, g, wgl, oa, ob, oc, wpa, wpb, wpc, wout, *, tm):
    t, d = x.shape
    row = lambda i: (i, 0)
    fixed = lambda i: (0, 0)
    return pl.pallas_call(
        _merge_kernel,
        grid=(t // tm,),
        in_specs=[
            pl.BlockSpec((tm, d), row),
            pl.BlockSpec((1, d), fixed),
            pl.BlockSpec((d, 3 * d), fixed),
            pl.BlockSpec((tm, 512), row),
            pl.BlockSpec((tm, 512), row),
            pl.BlockSpec((tm, 512), row),
            pl.BlockSpec((512, d), fixed),
            pl.BlockSpec((512, d), fixed),
            pl.BlockSpec((512, d), fixed),
            pl.BlockSpec((d, d), fixed),
        ],
        out_specs=pl.BlockSpec((tm, d), row),
        out_shape=jax.ShapeDtypeStruct((t, d), F32),
        compiler_params=_cparams("parallel"),
    )(x, g.reshape(1, d), wgl, oa, ob, oc, wpa, wpb, wpc, wout)


def _first_argmax(vals, lane):
    m = jnp.max(vals, axis=-1, keepdims=True)
    idx = jnp.min(jnp.where(vals == m, lane, LANES), axis=-1, keepdims=True)
    return m, idx


def _pack_bf16_pairs(h):
    n = h.shape[1] // 2
    bits = lax.bitcast_convert_type(h.astype(jnp.bfloat16).astype(F32), jnp.int32)
    return lax.shift_right_logical(bits[:, :n], 16) | bits[:, n:]


def _unpack_bf16_pairs(w):
    lo = lax.bitcast_convert_type(w << 16, F32).astype(BF16)
    hi = lax.bitcast_convert_type(w & HIGH_HALF, F32).astype(BF16)
    return lo, hi


def _router_kernel(x_ref, g_ref, w_ref, b_ref, hp_ref, meta_ref, seg_ref):
    x = x_ref[...]
    ms = jnp.mean(x * x, axis=-1, keepdims=True)
    h = x * lax.rsqrt(ms + EPS) * g_ref[...]
    hp_ref[...] = _pack_bf16_pairs(h)
    logits = jnp.dot(h, w_ref[...], preferred_element_type=F32, precision=lax.Precision.HIGHEST) + b_ref[...]
    lane = lax.broadcasted_iota(jnp.int32, logits.shape, 1)
    is_grp = (lane >= N_EXPERTS) & (lane < N_EXPERTS + N_GROUPS)
    gl = jnp.where(is_grp, logits, -jnp.inf)
    gmax, gidx = _first_argmax(gl, lane)
    pg = 1.0 / jnp.sum(jnp.exp(gl - gmax), axis=-1, keepdims=True)
    gsel = gidx - N_EXPERTS
    in_grp = (lane >> 3) == gsel
    el = jnp.where(in_grp, logits, -jnp.inf)
    m1, i1 = _first_argmax(el, lane)
    z = jnp.sum(jnp.exp(el - m1), axis=-1, keepdims=True)
    el2 = jnp.where(lane == i1, -jnp.inf, el)
    m2, i2 = _first_argmax(el2, lane)
    pe1 = 1.0 / z
    pe2 = jnp.exp(m2 - m1) / z
    den = pe1 + pe2
    w1 = pg * pe1 / den
    w2 = pg * pe2 / den

    sel1 = lane == i1
    sel2 = lane == i2
    onehot = jnp.where(sel1 | sel2, 1.0, 0.0)
    a = lax.broadcasted_iota(jnp.int32, (LANES, LANES), 0)
    b = lax.broadcasted_iota(jnp.int32, (LANES, LANES), 1)
    lower = jnp.where(b < a, 1.0, 0.0).astype(BF16)
    carry = jnp.zeros((1, LANES), F32)
    ranks = []
    for c in range(MOE_TILE // LANES):
        blk = onehot[c * LANES:(c + 1) * LANES]
        ranks.append(jnp.dot(lower, blk.astype(BF16), preferred_element_type=F32) + carry)
        carry = carry + jnp.sum(blk, axis=0, keepdims=True)
    rank = jnp.concatenate(ranks, axis=0)
    cnt = jnp.broadcast_to(carry, (8, LANES))
    seg = jnp.floor((cnt + 7.0) * 0.125) * 8.0
    lane8 = lax.broadcasted_iota(jnp.int32, (8, LANES), 1)
    scan = seg
    for k in (1, 2, 4, 8, 16, 32, 64):
        scan = scan + jnp.where(lane8 >= k, pltpu.roll(scan, k, 1), 0.0)
    off = scan - seg
    where_row = rank + off[0:1]
    pos1 = jnp.sum(jnp.where(sel1, where_row, 0.0), axis=-1, keepdims=True)
    pos2 = jnp.sum(jnp.where(sel2, where_row, 0.0), axis=-1, keepdims=True)
    meta_ref[...] = (jnp.where(lane == 0, pos1, 0.0) + jnp.where(lane == 1, pos2, 0.0)
                     + jnp.where(lane == 2, w1, 0.0) + jnp.where(lane == 3, w2, 0.0))
    row8 = lax.broadcasted_iota(jnp.int32, (8, LANES), 0)
    seg_ref[...] = jnp.where(row8 == 0, cnt, jnp.where(row8 == 1, off, 0.0)).astype(jnp.int32)


def _router(x, g, w, b):
    t, d = x.shape
    nt = t // MOE_TILE
    return pl.pallas_call(
        _router_kernel,
        grid=(nt,),
        in_specs=[
            pl.BlockSpec((MOE_TILE, d), lambda i: (i, 0)),
            pl.BlockSpec((1, d), lambda i: (0, 0)),
            pl.BlockSpec((d, LANES), lambda i: (0, 0)),
            pl.BlockSpec((1, LANES), lambda i: (0, 0)),
        ],
        out_specs=[
            pl.BlockSpec((MOE_TILE, d // 2), lambda i: (i, 0)),
            pl.BlockSpec((MOE_TILE, LANES), lambda i: (i, 0)),
            pl.BlockSpec((None, 8, LANES), lambda i: (i, 0, 0)),
        ],
        out_shape=[
            jax.ShapeDtypeStruct((t, d // 2), jnp.int32),
            jax.ShapeDtypeStruct((t, LANES), F32),
            jax.ShapeDtypeStruct((nt, 8, LANES), jnp.int32),
        ],
        compiler_params=_cparams("parallel"),
    )(x, g.reshape(1, d), w, b)


def _scatter_kernel(pos_ref, hp_ref, xs_ref):
    xs_ref[...] = jnp.zeros_like(xs_ref)

    def body(t, carry):
        row = hp_ref[pl.ds(t, 1), :]
        xs_ref[pl.ds(pos_ref[0, t], 1), :] = row
        xs_ref[pl.ds(pos_ref[0, MOE_TILE + t], 1), :] = row
        return carry

    lax.fori_loop(0, MOE_TILE, body, 0, unroll=8)


def _scatter_rows(pos, hp):
    nt = pos.shape[0]
    return pl.pallas_call(
        _scatter_kernel,
        grid=(nt,),
        in_specs=[
            pl.BlockSpec((None, 1, 2 * MOE_TILE), lambda i: (i, 0, 0), memory_space=pltpu.SMEM),
            pl.BlockSpec((MOE_TILE, hp.shape[1]), lambda i: (i, 0)),
        ],
        out_specs=pl.BlockSpec((None, XS_ROWS, hp.shape[1]), lambda i: (i, 0, 0)),
        out_shape=jax.ShapeDtypeStruct((nt, XS_ROWS, hp.shape[1]), jnp.int32),
        compiler_params=_cparams("parallel"),
    )(pos, hp)


def _experts_kernel(cnt_ref, off_ref, xs_ref, wg_ref, wu_ref, wd_ref, ys_ref):
    i = pl.program_id(0)
    e = pl.program_id(1)
    half = D_MODEL // 2

    @pl.when(e == 0)
    def _():
        ys_ref[2 * MOE_TILE:XS_ROWS, :] = jnp.zeros((XS_ROWS - 2 * MOE_TILE, D_MODEL), F32)

    n = cnt_ref[i, e]
    off = off_ref[i, e]

    def body(c, carry):
        start = pl.multiple_of(off + c * EXPERT_CHUNK, 8)
        lo, hi = _unpack_bf16_pairs(xs_ref[pl.ds(start, EXPERT_CHUNK), :])
        a = (jnp.dot(lo, wg_ref[0:half], preferred_element_type=F32)
             + jnp.dot(hi, wg_ref[half:D_MODEL], preferred_element_type=F32))
        u = (jnp.dot(lo, wu_ref[0:half], preferred_element_type=F32)
             + jnp.dot(hi, wu_ref[half:D_MODEL], preferred_element_type=F32))
        hh = (a * jax.nn.sigmoid(a)) * u
        ys_ref[pl.ds(start, EXPERT_CHUNK), :] = jnp.dot(hh.astype(BF16), wd_ref[...], preferred_element_type=F32)
        return carry

    lax.fori_loop(0, (n + EXPERT_CHUNK - 1) // EXPERT_CHUNK, body, 0)


def _experts(cnt, off, xs, wg, wu, wd):
    nt = xs.shape[0]
    d = D_MODEL
    grid_spec = pltpu.PrefetchScalarGridSpec(
        num_scalar_prefetch=2,
        grid=(nt, N_EXPERTS),
        in_specs=[
            pl.BlockSpec((None, XS_ROWS, d // 2), lambda i, e, c, o: (i, 0, 0), pipeline_mode=pl.Buffered(1)),
            pl.BlockSpec((None, d, D_EXPERT), lambda i, e, c, o: (e, 0, 0)),
            pl.BlockSpec((None, d, D_EXPERT), lambda i, e, c, o: (e, 0, 0)),
            pl.BlockSpec((None, D_EXPERT, d), lambda i, e, c, o: (e, 0, 0)),
        ],
        out_specs=pl.BlockSpec((None, XS_ROWS, d), lambda i, e, c, o: (i, 0, 0)),
    )
    return pl.pallas_call(
        _experts_kernel,
        grid_spec=grid_spec,
        out_shape=jax.ShapeDtypeStruct((nt, XS_ROWS, d), F32),
        compiler_params=_cparams("parallel", "arbitrary"),
    )(cnt, off, xs, wg, wu, wd)


def _combine_kernel(pos_ref, wt_ref, x_ref, ys_ref, o_ref):
    s = pl.program_id(1)

    def body(tl, carry):
        t = s * COMBINE_ROWS + tl
        y = (ys_ref[pl.ds(pos_ref[0, t], 1), :] * wt_ref[0, t]
             + ys_ref[pl.ds(pos_ref[0, MOE_TILE + t], 1), :] * wt_ref[0, MOE_TILE + t])
        o_ref[pl.ds(tl, 1), :] = x_ref[pl.ds(tl, 1), :] + y
        return carry

    lax.fori_loop(0, COMBINE_ROWS, body, 0, unroll=8)


def _combine(pos, wt, x, ys):
    t, d = x.shape
    nt = pos.shape[0]
    sub = MOE_TILE // COMBINE_ROWS
    return pl.pallas_call(
        _combine_kernel,
        grid=(nt, sub),
        in_specs=[
            pl.BlockSpec((None, 1, 2 * MOE_TILE), lambda i, s: (i, 0, 0), memory_space=pltpu.SMEM),
            pl.BlockSpec((None, 1, 2 * MOE_TILE), lambda i, s: (i, 0, 0), memory_space=pltpu.SMEM),
            pl.BlockSpec((COMBINE_ROWS, d), lambda i, s: (i * sub + s, 0)),
            pl.BlockSpec((None, XS_ROWS, d), lambda i, s: (i, 0, 0)),
        ],
        out_specs=pl.BlockSpec((COMBINE_ROWS, d), lambda i, s: (i * sub + s, 0)),
        out_shape=jax.ShapeDtypeStruct((t, d), F32),
        compiler_params=_cparams("parallel", "arbitrary"),
    )(pos, wt, x, ys)


def _moe(x, g, w_r, b_r, wg, wu, wd):
    t = x.shape[0]
    nt = t // MOE_TILE
    hp, meta, seg = _router(x, g, w_r, b_r)
    pair_major = lambda m: m.reshape(nt, MOE_TILE, 2).transpose(0, 2, 1).reshape(nt, 1, 2 * MOE_TILE)
    pos = pair_major(meta[:, 0:2].astype(jnp.int32))
    wt = pair_major(meta[:, 2:4])
    xs = _scatter_rows(pos, hp)
    ys = _experts(seg[:, 0, :N_EXPERTS], seg[:, 1, :N_EXPERTS], xs, wg, wu, wd)
    return _combine(pos, wt, x, ys)


def _final_norm_kernel(x_ref, g_ref, o_ref):
    x = x_ref[...]
    ms = jnp.mean(x * x, axis=-1, keepdims=True)
    o_ref[...] = x * lax.rsqrt(ms + EPS) * g_ref[...]


def _final_norm(x, g, *, tm):
    t, d = x.shape
    return pl.pallas_call(
        _final_norm_kernel,
        grid=(t // tm,),
        in_specs=[pl.BlockSpec((tm, d), lambda i: (i, 0)), pl.BlockSpec((1, d), lambda i: (0, 0))],
        out_specs=pl.BlockSpec((tm, d), lambda i: (i, 0)),
        out_shape=jax.ShapeDtypeStruct((t, d), F32),
        compiler_params=_cparams("parallel"),
    )(x, g.reshape(1, d))


def _pad_cols(w, n):
    return jnp.pad(w, ((0, 0), (0, n - w.shape[1])))


def kernel(x, attn_norm_g, w_in, b_q_norm_g, b_w_uq, b_kv_norm_g, b_w_ukv, c_rel_bias, w_proj_a, w_proj_b, w_proj_c, w_out, ffn_norm_g, w_group, b_group, w_router, b_router, w_gate, w_up, w_down, final_norm_g):
    bsz, seq, d = x.shape
    assert (seq, d) == (SEQ, D_MODEL)
    t = bsz * seq
    depth = w_in.shape[0]
    tm = 512

    tab_a = _rope_table(A_ROT, A_HEAD_DIM)
    tab_i = _rope_table(IDX_ROT, IDX_DIM)
    tab_i_half = _rope_table(IDX_ROT, IDX_DIM, active_lanes=IDX_DIM)
    tab_b = _rope_table(B_ROPE, B_ROPE)
    tab_b_half = _rope_table(B_ROPE, B_ROPE, active_lanes=B_ROPE)

    xf = x.reshape(t, d)
    for l in range(depth):
        w = w_in[l]
        w_a = w[:, 0:768].astype(BF16)
        w_i = _pad_cols(w[:, 768:1352], 640).astype(BF16)
        w_b = _pad_cols(w[:, 1352:1800], 512).astype(BF16)
        w_c = jnp.concatenate([w[:, 2312:3336], w[:, 1800:2312]], axis=1).astype(BF16)
        w_g = w[:, 3336:6408].astype(BF16)
        g_attn = attn_norm_g[l]

        za = _norm_proj(xf, g_attn, w_a, BF16, tm=tm, tabs=(tab_a,), halves=(A_ROT // 2,),
                        tile_tab=(0, 0, 0, 0, 0, -1))
        zi = _norm_proj(xf, g_attn, w_i, F32, tm=tm, tabs=(tab_i, tab_i_half), halves=(IDX_ROT // 2,) * 2,
                        tile_tab=(0, 0, 0, 0, 1))
        zb = _norm_proj(xf, g_attn, w_b, F32, tm=tm, tabs=(tab_b_half,), halves=(B_ROPE // 2,),
                        tile_tab=(-1, -1, -1, 0))
        zc = _norm_proj(xf, g_attn, w_c, BF16, tm=tm)

        w_uq = b_w_uq[l].reshape(B_Q_RANK, B_HEADS, B_NOPE + B_ROPE)
        w_uq = jnp.concatenate([w_uq[:, :, :B_NOPE].reshape(B_Q_RANK, -1),
                                w_uq[:, :, B_NOPE:].reshape(B_Q_RANK, -1)], axis=1).astype(BF16)
        w_ukv = b_w_ukv[l].reshape(B_KV_RANK, B_HEADS, B_NOPE + B_V)
        w_ukv = jnp.concatenate([w_ukv[:, :, :B_NOPE].reshape(B_KV_RANK, -1),
                                 w_ukv[:, :, B_NOPE:].reshape(B_KV_RANK, -1)], axis=1).astype(BF16)
        qb = _norm_proj(zb, b_q_norm_g[l], w_uq, BF16, tm=tm, x_cols=0, tabs=(tab_b,),
                        halves=(B_ROPE // 2,), tile_tab=(-1, -1, -1, -1, 0, 0))
        kvb = _norm_proj(zb, b_kv_norm_g[l], w_ukv, BF16, tm=tm, x_cols=2)

        o_a = _dsa_attention(za.reshape(bsz, seq, -1), zi.reshape(bsz, seq, -1), bsz)
        o_b = _mla_attention(qb.reshape(bsz, seq, -1), kvb.reshape(bsz, seq, -1), zb.reshape(bsz, seq, -1), bsz)
        o_c = _band_attention(zc.reshape(bsz, seq, -1), _band_bias(c_rel_bias[l]), bsz)

        xf = _merge(xf, g_attn, w_g, o_a.reshape(t, -1), o_b.reshape(t, -1), o_c.reshape(t, -1),
                    w_proj_a[l].astype(BF16), w_proj_b[l].astype(BF16), w_proj_c[l].astype(BF16),
                    w_out[l].astype(BF16), tm=tm)

        w_r = _pad_cols(jnp.concatenate([w_router[l], w_group[l]], axis=1), LANES)
        b_r = _pad_cols(jnp.concatenate([b_router[l], b_group[l]])[None, :], LANES)
        xf = _moe(xf, ffn_norm_g[l], w_r, b_r, w_gate[l].astype(BF16), w_up[l].astype(BF16), w_down[l].astype(BF16))

    return _final_norm(xf, final_norm_g, tm=tm).reshape(bsz, seq, d)
```

```python
import functools

import numpy as np
import jax
import jax.numpy as jnp
from jax import lax
from jax.experimental import pallas as pl
from jax.experimental.pallas import tpu as pltpu

F32 = jnp.float32
BF16 = jnp.bfloat16

LANES = 128
D_MODEL = 1024
SEQ = 2048
CHUNK = 64
Q_BLOCK = 128
ROPE_THETA = 500000.0
EPS = 1e-6

A_HEADS = 4
A_HEAD_DIM = 128
A_ROT = 32
IDX_HEADS = 8
IDX_DIM = 64
IDX_ROT = 16
TOPK = 256
B_HEADS = 4
B_NOPE = 128
B_ROPE = 64
B_V = 128
B_Q_RANK = 256
B_KV_RANK = 128
C_HEADS = 4
C_HEAD_DIM = 128
C_LEFT_CHUNKS = 8
REL_CLIP = 128
N_GROUPS = 4
EXPERTS_PER_GROUP = 8
N_EXPERTS = 32
D_EXPERT = 256

C_KEY_BLOCKS = C_LEFT_CHUNKS * CHUNK // Q_BLOCK + 1
N_QB = SEQ // Q_BLOCK

VMEM_LIMIT = 56 * 1024 * 1024

MOE_TILE = 2048
EXPERT_CHUNK = 128
XS_ROWS = 2 * MOE_TILE + 3 * EXPERT_CHUNK
assert XS_ROWS >= 2 * MOE_TILE + N_EXPERTS * 7 + EXPERT_CHUNK - 1
COMBINE_ROWS = 512
HIGH_HALF = -65536

INT_MIN = -2 ** 31
NEG_INF_KEY = int(np.array(0x807FFFFF, np.uint32).view(np.int32))

NT_DIMS = (((1,), (1,)), ((), ()))


def _nt_dot(a, b):
    return lax.dot_general(a, b, NT_DIMS, preferred_element_type=F32)


TN_DIMS = (((0,), (0,)), ((), ()))


def _tn_dot(a, b):
    return lax.dot_general(a, b, TN_DIMS, preferred_element_type=F32)


def _cparams(*sem):
    return pltpu.CompilerParams(dimension_semantics=sem, vmem_limit_bytes=VMEM_LIMIT)


def _rope_table(rot, period, active_lanes=LANES):
    half = rot // 2
    lane = np.arange(LANES)
    p = lane % period
    first = (p < half) & (lane < active_lanes)
    second = (p >= half) & (p < rot) & (lane < active_lanes)
    idx = np.where(first, p, np.where(second, p - half, 0))
    pos = jnp.arange(SEQ, dtype=F32)
    inv = ROPE_THETA ** (-jnp.arange(0, rot, 2, dtype=F32) / rot)
    ang = pos[:, None] * inv[idx][None, :]
    cos, sin = jnp.cos(ang), jnp.sin(ang)
    c = jnp.where(first | second, cos, 1.0)
    s_prev = jnp.where(second, sin, 0.0)
    s_next = jnp.where(first, -sin, 0.0)
    return jnp.stack([c, s_prev, s_next]).astype(F32)


def _norm_proj_kernel(x_ref, g_ref, w_ref, *refs, tile_tab, halves):
    n_tabs = len(halves)
    tab_refs, o_ref = refs[:n_tabs], refs[n_tabs]
    x = x_ref[...].astype(F32)
    ms = jnp.mean(x * x, axis=-1, keepdims=True)
    h = (x * lax.rsqrt(ms + EPS) * g_ref[...]).astype(BF16)
    z = jnp.dot(h, w_ref[...], preferred_element_type=F32)
    for c, t in enumerate(tile_tab):
        zt = z[:, c * LANES:(c + 1) * LANES]
        if t >= 0:
            tab, half = tab_refs[t], halves[t]
            zt = (zt * tab[0] + pltpu.roll(zt, half, 1) * tab[1]
                  + pltpu.roll(zt, LANES - half, 1) * tab[2])
        o_ref[:, c * LANES:(c + 1) * LANES] = zt.astype(o_ref.dtype)


def _norm_proj(x, g, w, out_dtype, *, tm, x_cols=None, tabs=(), halves=(), tile_tab=None):
    t = x.shape[0]
    k, n = w.shape
    col_blk = 0 if x_cols is None else x_cols
    if tile_tab is None:
        tile_tab = (-1,) * (n // LANES)
    seq_tiles = SEQ // tm
    in_specs = [
        pl.BlockSpec((tm, k), lambda i: (i, col_blk)),
        pl.BlockSpec((1, k), lambda i: (0, 0)),
        pl.BlockSpec((k, n), lambda i: (0, 0)),
    ] + [pl.BlockSpec((3, tm, LANES), lambda i: (0, i % seq_tiles, 0)) for _ in tabs]
    return pl.pallas_call(
        functools.partial(_norm_proj_kernel, tile_tab=tuple(tile_tab), halves=tuple(halves)),
        grid=(t // tm,),
        in_specs=in_specs,
        out_specs=pl.BlockSpec((tm, n), lambda i: (i, 0)),
        out_shape=jax.ShapeDtypeStruct((t, n), out_dtype),
        compiler_params=_cparams("parallel"),
    )(x, g.reshape(1, k), w, *tabs)


def _flash_step(s, v, carry, acc_ref):
    m, l = carry
    m_new = jnp.maximum(m, jnp.max(s, axis=0, keepdims=True))
    m_safe = jnp.where(m_new == -jnp.inf, 0.0, m_new)
    alpha = jnp.exp(m - m_safe)
    p = jnp.exp(s - m_safe)
    l = l * alpha + jnp.sum(p, axis=0, keepdims=True)
    acc_ref[...] = acc_ref[...] * alpha + _tn_dot(v, p.astype(BF16))
    return m_new, l


def _flash_init(n):
    return jnp.full((1, n), -jnp.inf, F32), jnp.zeros((1, n), F32)


SEARCH_ROWS = 256


def _dsa_kernel(q_ref, kv_ref, iq_ref, ikw_ref, o_ref, key_ref, bias_ref, acc_ref):
    j = pl.program_id(1)
    n_blk = j + 1
    n_grp = (j + 2) // 2
    row0 = pl.multiple_of(j * Q_BLOCK, Q_BLOCK)
    sub = lax.broadcasted_iota(jnp.int32, (Q_BLOCK, Q_BLOCK), 0)
    lane = lax.broadcasted_iota(jnp.int32, (Q_BLOCK, Q_BLOCK), 1)
    q_chunk = (row0 + lane) >> 6

    iq = iq_ref[...].astype(BF16)
    iq_stack = jnp.concatenate([iq[:, h * IDX_DIM:(h + 1) * IDX_DIM] for h in range(IDX_HEADS)], axis=0)
    iw_t = ikw_ref[pl.ds(row0, Q_BLOCK), :].T * (IDX_HEADS ** -0.5)

    def score_block(kb, carry):
        k0 = pl.multiple_of(kb * Q_BLOCK, Q_BLOCK)
        ik = ikw_ref[pl.ds(k0, Q_BLOCK), 0:IDX_DIM].astype(BF16)
        score = jnp.zeros((Q_BLOCK, Q_BLOCK), F32)
        for hp in range(IDX_HEADS // 2):
            r = _nt_dot(ik, iq_stack[hp * 2 * Q_BLOCK:(hp + 1) * 2 * Q_BLOCK])
            for u in range(2):
                h = 2 * hp + u
                rel = jnp.maximum(r[:, u * Q_BLOCK:(u + 1) * Q_BLOCK] * (IDX_DIM ** -0.5), 0.0)
                score = score + rel * iw_t[IDX_DIM + h:IDX_DIM + h + 1, :]
        allowed = ((k0 + sub) >> 6) <= q_chunk
        score = jnp.where(score == 0.0, 0.0, score)
        score = jnp.where(allowed, score, -jnp.inf)
        bits = lax.bitcast_convert_type(score, jnp.int32)
        key_ref[pl.ds(k0, Q_BLOCK), :] = bits ^ ((bits >> 31) & 0x7FFFFFFF)
        return carry

    lax.fori_loop(0, 2 * n_grp, score_block, 0)

    def count(pred):
        def group(g, accs):
            g0 = pl.multiple_of(g * SEARCH_ROWS, SEARCH_ROWS)
            accs = list(accs)
            for r in range(SEARCH_ROWS // 8):
                kk = key_ref[pl.ds(g0 + 8 * r, 8), :]
                accs[r % 4] = accs[r % 4] + jnp.where(pred(kk), 1.0, 0.0)
            return tuple(accs)
        accs = lax.fori_loop(0, n_grp, group, (jnp.zeros((8, Q_BLOCK), F32),) * 4)
        return jnp.sum(accs[0] + accs[1] + accs[2] + accs[3], axis=0, keepdims=True)

    def count_ge(cand):
        cand8 = jnp.broadcast_to(cand, (8, Q_BLOCK))
        return count(lambda kk: kk >= cand8)

    base = jnp.where(count_ge(jnp.zeros((1, Q_BLOCK), jnp.int32)) >= TOPK, 0, INT_MIN).astype(jnp.int32)

    def search(i, base):
        cand = base | jnp.left_shift(jnp.int32(1), 30 - i)
        return jnp.where(count_ge(cand) >= TOPK, cand, base)

    thr = lax.fori_loop(0, 31, search, base)
    thr8 = jnp.broadcast_to(thr, (8, Q_BLOCK))
    cnt_gt = count(lambda kk: kk > thr8)
    cnt_ge = count_ge(thr)
    tie_cols = (cnt_ge > TOPK) & (thr > NEG_INF_KEY)
    has_tie = jnp.max(jnp.where(tie_cols, 1.0, 0.0)) > 0.0

    def bias_block(kb, carry):
        k0 = pl.multiple_of(kb * Q_BLOCK, Q_BLOCK)
        kk = key_ref[pl.ds(k0, Q_BLOCK), :]
        bias_ref[pl.ds(k0, Q_BLOCK), :] = jnp.where((kk >= thr) & (kk > NEG_INF_KEY), 0.0, -jnp.inf)
        return carry

    lax.fori_loop(0, n_blk, bias_block, 0)

    @pl.when(has_tie)
    def _():
        need = TOPK - cnt_gt
        lower = jnp.where(lane < sub, 1.0, 0.0).astype(BF16)

        def tie_block(kb, seen):
            k0 = pl.multiple_of(kb * Q_BLOCK, Q_BLOCK)
            kk = key_ref[pl.ds(k0, Q_BLOCK), :]
            eq = jnp.where(kk == thr, 1.0, 0.0)
            before = jnp.dot(lower, eq.astype(BF16), preferred_element_type=F32) + seen
            keep = (kk > thr) | ((kk == thr) & (before < need))
            bias_ref[pl.ds(k0, Q_BLOCK), :] = jnp.where(keep & (kk > NEG_INF_KEY), 0.0, -jnp.inf)
            return seen + jnp.sum(eq, axis=0, keepdims=True)

        lax.fori_loop(0, n_blk, tie_block, jnp.zeros((1, Q_BLOCK), F32))

    q = q_ref[...]
    q_stack = jnp.concatenate([q[:, h * A_HEAD_DIM:(h + 1) * A_HEAD_DIM] for h in range(A_HEADS)], axis=0)
    acc_ref[...] = jnp.zeros_like(acc_ref)

    def att_block(kb, carry):
        k0 = pl.multiple_of(kb * Q_BLOCK, Q_BLOCK)
        k = kv_ref[pl.ds(k0, Q_BLOCK), 0:A_HEAD_DIM]
        v = kv_ref[pl.ds(k0, Q_BLOCK), A_HEAD_DIM:2 * A_HEAD_DIM]
        bias = bias_ref[pl.ds(k0, Q_BLOCK), :]
        s = _nt_dot(k, q_stack) * (A_HEAD_DIM ** -0.5) + jnp.concatenate([bias] * A_HEADS, axis=1)
        return _flash_step(s, v, carry, acc_ref)

    _, l = lax.fori_loop(0, n_blk, att_block, _flash_init(A_HEADS * Q_BLOCK))
    o_t = acc_ref[...] / l
    for h in range(A_HEADS):
        o_ref[:, h * A_HEAD_DIM:(h + 1) * A_HEAD_DIM] = o_t[:, h * Q_BLOCK:(h + 1) * Q_BLOCK].T.astype(o_ref.dtype)


def _dsa_attention(za, zi, bsz):
    return pl.pallas_call(
        _dsa_kernel,
        grid=(bsz, N_QB),
        in_specs=[
            pl.BlockSpec((None, Q_BLOCK, 512), lambda b, j: (b, j, 0)),
            pl.BlockSpec((None, SEQ, 256), lambda b, j: (b, 0, 2)),
            pl.BlockSpec((None, Q_BLOCK, 512), lambda b, j: (b, j, 0)),
            pl.BlockSpec((None, SEQ, LANES), lambda b, j: (b, 0, 4)),
        ],
        out_specs=pl.BlockSpec((None, Q_BLOCK, 512), lambda b, j: (b, j, 0)),
        out_shape=jax.ShapeDtypeStruct((bsz, SEQ, 512), BF16),
        scratch_shapes=[pltpu.VMEM((SEQ, Q_BLOCK), jnp.int32),
                        pltpu.VMEM((SEQ, Q_BLOCK), F32),
                        pltpu.VMEM((A_HEAD_DIM, A_HEADS * Q_BLOCK), F32)],
        compiler_params=_cparams("parallel", "arbitrary"),
    )(za, za, zi, zi)


MLA_BLOCK = 256


def _mla_kernel(q_ref, kv_ref, zb_ref, o_ref, acc_ref):
    j = pl.program_id(1)
    row0 = pl.multiple_of(j * MLA_BLOCK, MLA_BLOCK)
    scale = (B_NOPE + B_ROPE) ** -0.5
    sub = lax.broadcasted_iota(jnp.int32, (MLA_BLOCK, MLA_BLOCK), 0)
    lane = lax.broadcasted_iota(jnp.int32, (MLA_BLOCK, MLA_BLOCK), 1)
    diag_ok = (sub >> 6) <= (lane >> 6)
    for h in range(B_HEADS):
        qcat = jnp.concatenate(
            [q_ref[:, h * B_NOPE:(h + 1) * B_NOPE],
             q_ref[:, B_HEADS * B_NOPE + h * B_ROPE:B_HEADS * B_NOPE + (h + 1) * B_ROPE]], axis=1)
        acc_ref[...] = jnp.zeros_like(acc_ref)

        def logits(k0):
            kcat = jnp.concatenate([kv_ref[pl.ds(k0, MLA_BLOCK), h * B_NOPE:(h + 1) * B_NOPE],
                                    zb_ref[pl.ds(k0, MLA_BLOCK), 0:B_ROPE].astype(BF16)], axis=1)
            v = kv_ref[pl.ds(k0, MLA_BLOCK), B_HEADS * B_NOPE + h * B_V:B_HEADS * B_NOPE + (h + 1) * B_V]
            return _nt_dot(kcat, qcat) * scale, v

        def full_block(kb, carry):
            s, v = logits(pl.multiple_of(kb * MLA_BLOCK, MLA_BLOCK))
            return _flash_step(s, v, carry, acc_ref)

        carry = lax.fori_loop(0, j, full_block, _flash_init(MLA_BLOCK))
        s, v = logits(row0)
        _, l = _flash_step(jnp.where(diag_ok, s, -jnp.inf), v, carry, acc_ref)
        o_ref[:, h * B_V:(h + 1) * B_V] = (acc_ref[...] / l).T.astype(o_ref.dtype)


def _mla_attention(qb, kvb, zb, bsz):
    return pl.pallas_call(
        _mla_kernel,
        grid=(bsz, SEQ // MLA_BLOCK),
        in_specs=[
            pl.BlockSpec((None, MLA_BLOCK, 768), lambda b, j: (b, j, 0)),
            pl.BlockSpec((None, SEQ, 1024), lambda b, j: (b, 0, 0)),
            pl.BlockSpec((None, SEQ, LANES), lambda b, j: (b, 0, 3)),
        ],
        out_specs=pl.BlockSpec((None, MLA_BLOCK, 512), lambda b, j: (b, j, 0)),
        out_shape=jax.ShapeDtypeStruct((bsz, SEQ, 512), BF16),
        scratch_shapes=[pltpu.VMEM((B_V, MLA_BLOCK), F32)],
        compiler_params=_cparams("parallel", "arbitrary"),
    )(qb, kvb, zb)


def _band_bias(rel_table):
    n = 2 * REL_CLIP + 1
    period = 2 * n - 1
    heads = rel_table.shape[0]
    ext = jnp.concatenate([rel_table, jnp.broadcast_to(rel_table[:, n - 1:n], (heads, n - 1))], axis=1)
    kj = np.arange(Q_BLOCK)[:, None]
    qi = np.arange(Q_BLOCK)[None, :]
    out = []
    for d in range(C_KEY_BLOCKS):
        base = d * Q_BLOCK + REL_CLIP
        if base - (Q_BLOCK - 1) >= n - 1:
            bias = jnp.broadcast_to(rel_table[:, n - 1][:, None, None], (heads, Q_BLOCK, Q_BLOCK))
        else:
            shifted = jnp.roll(ext, -base, axis=1)
            bias = jnp.tile(shifted, (1, Q_BLOCK))[:, :Q_BLOCK * (period - 1)]
            bias = bias.reshape(heads, Q_BLOCK, period - 1)[:, :, :Q_BLOCK]
        cdiff = 2 * d + qi // CHUNK - kj // CHUNK
        valid = (cdiff >= 0) & (cdiff <= C_LEFT_CHUNKS)
        out.append(jnp.where(valid[None], bias.astype(F32), -jnp.inf))
    return jnp.stack(out, axis=1)


def _band_kernel(q_ref, kv_ref, bias_ref, o_ref, acc_ref):
    j = pl.program_id(1)
    scale = C_HEAD_DIM ** -0.5
    n_blk = jnp.minimum(j + 1, C_KEY_BLOCKS)
    for h in range(C_HEADS):
        q = q_ref[:, h * C_HEAD_DIM:(h + 1) * C_HEAD_DIM]
        acc_ref[...] = jnp.zeros_like(acc_ref)

        def block(d, carry):
            k0 = pl.multiple_of((j - d) * Q_BLOCK, Q_BLOCK)
            k = kv_ref[pl.ds(k0, Q_BLOCK), h * C_HEAD_DIM:(h + 1) * C_HEAD_DIM]
            v = kv_ref[pl.ds(k0, Q_BLOCK), (C_HEADS + h) * C_HEAD_DIM:(C_HEADS + h + 1) * C_HEAD_DIM]
            s = _nt_dot(k, q) * scale + bias_ref[h, d]
            return _flash_step(s, v, carry, acc_ref)

        _, l = lax.fori_loop(0, n_blk, block, _flash_init(Q_BLOCK))
        o_ref[:, h * C_HEAD_DIM:(h + 1) * C_HEAD_DIM] = (acc_ref[...] / l).T.astype(o_ref.dtype)


def _band_attention(zc, bias, bsz):
    return pl.pallas_call(
        _band_kernel,
        grid=(bsz, N_QB),
        in_specs=[
            pl.BlockSpec((None, Q_BLOCK, 512), lambda b, j: (b, j, 2)),
            pl.BlockSpec((None, SEQ, 1024), lambda b, j: (b, 0, 0)),
            pl.BlockSpec((C_HEADS, C_KEY_BLOCKS, Q_BLOCK, Q_BLOCK), lambda b, j: (0, 0, 0, 0)),
        ],
        out_specs=pl.BlockSpec((None, Q_BLOCK, 512), lambda b, j: (b, j, 0)),
        out_shape=jax.ShapeDtypeStruct((bsz, SEQ, 512), BF16),
        scratch_shapes=[pltpu.VMEM((C_HEAD_DIM, Q_BLOCK), F32)],
        compiler_params=_cparams("parallel", "arbitrary"),
    )(zc, zc, bias)


def _merge_kernel(x_ref, g_ref, wgl_ref, oa_ref, ob_ref, oc_ref, wpa_ref, wpb_ref, wpc_ref, wout_ref, o_ref):
    x = x_ref[...]
    ms = jnp.mean(x * x, axis=-1, keepdims=True)
    h = (x * lax.rsqrt(ms + EPS) * g_ref[...]).astype(BF16)
    mix = jnp.zeros(x.shape, F32)
    for i, (o_in, wp) in enumerate(((oa_ref, wpa_ref), (ob_ref, wpb_ref), (oc_ref, wpc_ref))):
        gl = jnp.dot(h, wgl_ref[:, i * D_MODEL:(i + 1) * D_MODEL], preferred_element_type=F32)
        gate = jax.nn.sigmoid(gl)
        mix = mix + gate * jnp.dot(o_in[...], wp[...], preferred_element_type=F32)
    o_ref[...] = x + jnp.dot(mix.astype(BF16), wout_ref[...], preferred_element_type=F32)


def _merge(x, g, wgl, oa, ob, oc, wpa, wpb, wpc, wout, *, tm):
    t, d = x.shape
    row = lambda i: (i, 0)
    fixed = lambda i: (0, 0)
    return pl.pallas_call(
        _merge_kernel,
        grid=(t // tm,),
        in_specs=[
            pl.BlockSpec((tm, d), row),
            pl.BlockSpec((1, d), fixed),
            pl.BlockSpec((d, 3 * d), fixed),
            pl.BlockSpec((tm, 512), row),
            pl.BlockSpec((tm, 512), row),
            pl.BlockSpec((tm, 512), row),
            pl.BlockSpec((512, d), fixed),
            pl.BlockSpec((512, d), fixed),
            pl.BlockSpec((512, d), fixed),
            pl.BlockSpec((d, d), fixed),
        ],
        out_specs=pl.BlockSpec((tm, d), row),
        out_shape=jax.ShapeDtypeStruct((t, d), F32),
        compiler_params=_cparams("parallel"),
    )(x, g.reshape(1, d), wgl, oa, ob, oc, wpa, wpb, wpc, wout)


def _first_argmax(vals, lane):
    m = jnp.max(vals, axis=-1, keepdims=True)
    idx = jnp.min(jnp.where(vals == m, lane, LANES), axis=-1, keepdims=True)
    return m, idx


def _pack_bf16_pairs(h):
    n = h.shape[1] // 2
    bits = lax.bitcast_convert_type(h.astype(jnp.bfloat16).astype(F32), jnp.int32)
    return lax.shift_right_logical(bits[:, :n], 16) | bits[:, n:]


def _unpack_bf16_pairs(w):
    lo = lax.bitcast_convert_type(w << 16, F32).astype(BF16)
    hi = lax.bitcast_convert_type(w & HIGH_HALF, F32).astype(BF16)
    return lo, hi


def _router_kernel(x_ref, g_ref, w_ref, b_ref, hp_ref, meta_ref, seg_ref):
    x = x_ref[...]
    ms = jnp.mean(x * x, axis=-1, keepdims=True)
    h = x * lax.rsqrt(ms + EPS) * g_ref[...]
    hp_ref[...] = _pack_bf16_pairs(h)
    logits = jnp.dot(h, w_ref[...], preferred_element_type=F32, precision=lax.Precision.HIGHEST) + b_ref[...]
    lane = lax.broadcasted_iota(jnp.int32, logits.shape, 1)
    is_grp = (lane >= N_EXPERTS) & (lane < N_EXPERTS + N_GROUPS)
    gl = jnp.where(is_grp, logits, -jnp.inf)
    gmax, gidx = _first_argmax(gl, lane)
    pg = 1.0 / jnp.sum(jnp.exp(gl - gmax), axis=-1, keepdims=True)
    gsel = gidx - N_EXPERTS
    in_grp = (lane >> 3) == gsel
    el = jnp.where(in_grp, logits, -jnp.inf)
    m1, i1 = _first_argmax(el, lane)
    z = jnp.sum(jnp.exp(el - m1), axis=-1, keepdims=True)
    el2 = jnp.where(lane == i1, -jnp.inf, el)
    m2, i2 = _first_argmax(el2, lane)
    pe1 = 1.0 / z
    pe2 = jnp.exp(m2 - m1) / z
    den = pe1 + pe2
    w1 = pg * pe1 / den
    w2 = pg * pe2 / den

    sel1 = lane == i1
    sel2 = lane == i2
    onehot = jnp.where(sel1 | sel2, 1.0, 0.0)
    a = lax.broadcasted_iota(jnp.int32, (LANES, LANES), 0)
    b = lax.broadcasted_iota(jnp.int32, (LANES, LANES), 1)
    lower = jnp.where(b < a, 1.0, 0.0).astype(BF16)
    carry = jnp.zeros((1, LANES), F32)
    ranks = []
    for c in range(MOE_TILE // LANES):
        blk = onehot[c * LANES:(c + 1) * LANES]
        ranks.append(jnp.dot(lower, blk.astype(BF16), preferred_element_type=F32) + carry)
        carry = carry + jnp.sum(blk, axis=0, keepdims=True)
    rank = jnp.concatenate(ranks, axis=0)
    cnt = jnp.broadcast_to(carry, (8, LANES))
    seg = jnp.floor((cnt + 7.0) * 0.125) * 8.0
    lane8 = lax.broadcasted_iota(jnp.int32, (8, LANES), 1)
    scan = seg
    for k in (1, 2, 4, 8, 16, 32, 64):
        scan = scan + jnp.where(lane8 >= k, pltpu.roll(scan, k, 1), 0.0)
    off = scan - seg
    where_row = rank + off[0:1]
    pos1 = jnp.sum(jnp.where(sel1, where_row, 0.0), axis=-1, keepdims=True)
    pos2 = jnp.sum(jnp.where(sel2, where_row, 0.0), axis=-1, keepdims=True)
    meta_ref[...] = (jnp.where(lane == 0, pos1, 0.0) + jnp.where(lane == 1, pos2, 0.0)
                     + jnp.where(lane == 2, w1, 0.0) + jnp.where(lane == 3, w2, 0.0))
    row8 = lax.broadcasted_iota(jnp.int32, (8, LANES), 0)
    seg_ref[...] = jnp.where(row8 == 0, cnt, jnp.where(row8 == 1, off, 0.0)).astype(jnp.int32)


def _router(x, g, w, b):
    t, d = x.shape
    nt = t // MOE_TILE
    return pl.pallas_call(
        _router_kernel,
        grid=(nt,),
        in_specs=[
            pl.BlockSpec((MOE_TILE, d), lambda i: (i, 0)),
            pl.BlockSpec((1, d), lambda i: (0, 0)),
            pl.BlockSpec((d, LANES), lambda i: (0, 0)),
            pl.BlockSpec((1, LANES), lambda i: (0, 0)),
        ],
        out_specs=[
            pl.BlockSpec((MOE_TILE, d // 2), lambda i: (i, 0)),
            pl.BlockSpec((MOE_TILE, LANES), lambda i: (i, 0)),
            pl.BlockSpec((None, 8, LANES), lambda i: (i, 0, 0)),
        ],
        out_shape=[
            jax.ShapeDtypeStruct((t, d // 2), jnp.int32),
            jax.ShapeDtypeStruct((t, LANES), F32),
            jax.ShapeDtypeStruct((nt, 8, LANES), jnp.int32),
        ],
        compiler_params=_cparams("parallel"),
    )(x, g.reshape(1, d), w, b)


def _scatter_kernel(pos_ref, hp_ref, xs_ref):
    xs_ref[...] = jnp.zeros_like(xs_ref)

    def body(t, carry):
        row = hp_ref[pl.ds(t, 1), :]
        xs_ref[pl.ds(pos_ref[0, t], 1), :] = row
        xs_ref[pl.ds(pos_ref[0, MOE_TILE + t], 1), :] = row
        return carry

    lax.fori_loop(0, MOE_TILE, body, 0, unroll=8)


def _scatter_rows(pos, hp):
    nt = pos.shape[0]
    return pl.pallas_call(
        _scatter_kernel,
        grid=(nt,),
        in_specs=[
            pl.BlockSpec((None, 1, 2 * MOE_TILE), lambda i: (i, 0, 0), memory_space=pltpu.SMEM),
            pl.BlockSpec((MOE_TILE, hp.shape[1]), lambda i: (i, 0)),
        ],
        out_specs=pl.BlockSpec((None, XS_ROWS, hp.shape[1]), lambda i: (i, 0, 0)),
        out_shape=jax.ShapeDtypeStruct((nt, XS_ROWS, hp.shape[1]), jnp.int32),
        compiler_params=_cparams("parallel"),
    )(pos, hp)


def _experts_kernel(cnt_ref, off_ref, xs_ref, wg_ref, wu_ref, wd_ref, ys_ref):
    i = pl.program_id(0)
    e = pl.program_id(1)
    half = D_MODEL // 2

    @pl.when(e == 0)
    def _():
        ys_ref[2 * MOE_TILE:XS_ROWS, :] = jnp.zeros((XS_ROWS - 2 * MOE_TILE, D_MODEL), F32)

    n = cnt_ref[i, e]
    off = off_ref[i, e]

    def body(c, carry):
        start = pl.multiple_of(off + c * EXPERT_CHUNK, 8)
        lo, hi = _unpack_bf16_pairs(xs_ref[pl.ds(start, EXPERT_CHUNK), :])
        a = (jnp.dot(lo, wg_ref[0:half], preferred_element_type=F32)
             + jnp.dot(hi, wg_ref[half:D_MODEL], preferred_element_type=F32))
        u = (jnp.dot(lo, wu_ref[0:half], preferred_element_type=F32)
             + jnp.dot(hi, wu_ref[half:D_MODEL], preferred_element_type=F32))
        hh = (a * jax.nn.sigmoid(a)) * u
        ys_ref[pl.ds(start, EXPERT_CHUNK), :] = jnp.dot(hh.astype(BF16), wd_ref[...], preferred_element_type=F32)
        return carry

    lax.fori_loop(0, (n + EXPERT_CHUNK - 1) // EXPERT_CHUNK, body, 0)


def _experts(cnt, off, xs, wg, wu, wd):
    nt = xs.shape[0]
    d = D_MODEL
    grid_spec = pltpu.PrefetchScalarGridSpec(
        num_scalar_prefetch=2,
        grid=(nt, N_EXPERTS),
        in_specs=[
            pl.BlockSpec((None, XS_ROWS, d // 2), lambda i, e, c, o: (i, 0, 0), pipeline_mode=pl.Buffered(1)),
            pl.BlockSpec((None, d, D_EXPERT), lambda i, e, c, o: (e, 0, 0)),
            pl.BlockSpec((None, d, D_EXPERT), lambda i, e, c, o: (e, 0, 0)),
            pl.BlockSpec((None, D_EXPERT, d), lambda i, e, c, o: (e, 0, 0)),
        ],
        out_specs=pl.BlockSpec((None, XS_ROWS, d), lambda i, e, c, o: (i, 0, 0)),
    )
    return pl.pallas_call(
        _experts_kernel,
        grid_spec=grid_spec,
        out_shape=jax.ShapeDtypeStruct((nt, XS_ROWS, d), F32),
        compiler_params=_cparams("parallel", "arbitrary"),
    )(cnt, off, xs, wg, wu, wd)


def _combine_kernel(pos_ref, wt_ref, x_ref, ys_ref, o_ref):
    s = pl.program_id(1)

    def body(tl, carry):
        t = s * COMBINE_ROWS + tl
        y = (ys_ref[pl.ds(pos_ref[0, t], 1), :] * wt_ref[0, t]
             + ys_ref[pl.ds(pos_ref[0, MOE_TILE + t], 1), :] * wt_ref[0, MOE_TILE + t])
        o_ref[pl.ds(tl, 1), :] = x_ref[pl.ds(tl, 1), :] + y
        return carry

    lax.fori_loop(0, COMBINE_ROWS, body, 0, unroll=8)


def _combine(pos, wt, x, ys):
    t, d = x.shape
    nt = pos.shape[0]
    sub = MOE_TILE // COMBINE_ROWS
    return pl.pallas_call(
        _combine_kernel,
        grid=(nt, sub),
        in_specs=[
            pl.BlockSpec((None, 1, 2 * MOE_TILE), lambda i, s: (i, 0, 0), memory_space=pltpu.SMEM),
            pl.BlockSpec((None, 1, 2 * MOE_TILE), lambda i, s: (i, 0, 0), memory_space=pltpu.SMEM),
            pl.BlockSpec((COMBINE_ROWS, d), lambda i, s: (i * sub + s, 0)),
            pl.BlockSpec((None, XS_ROWS, d), lambda i, s: (i, 0, 0)),
        ],
        out_specs=pl.BlockSpec((COMBINE_ROWS, d), lambda i, s: (i * sub + s, 0)),
        out_shape=jax.ShapeDtypeStruct((t, d), F32),
        compiler_params=_cparams("parallel", "arbitrary"),
    )(pos, wt, x, ys)


def _moe(x, g, w_r, b_r, wg, wu, wd):
    t = x.shape[0]
    nt = t // MOE_TILE
    hp, meta, seg = _router(x, g, w_r, b_r)
    pair_major = lambda m: m.reshape(nt, MOE_TILE, 2).transpose(0, 2, 1).reshape(nt, 1, 2 * MOE_TILE)
    pos = pair_major(meta[:, 0:2].astype(jnp.int32))
    wt = pair_major(meta[:, 2:4])
    xs = _scatter_rows(pos, hp)
    ys = _experts(seg[:, 0, :N_EXPERTS], seg[:, 1, :N_EXPERTS], xs, wg, wu, wd)
    return _combine(pos, wt, x, ys)


def _final_norm_kernel(x_ref, g_ref, o_ref):
    x = x_ref[...]
    ms = jnp.mean(x * x, axis=-1, keepdims=True)
    o_ref[...] = x * lax.rsqrt(ms + EPS) * g_ref[...]


def _final_norm(x, g, *, tm):
    t, d = x.shape
    return pl.pallas_call(
        _final_norm_kernel,
        grid=(t // tm,),
        in_specs=[pl.BlockSpec((tm, d), lambda i: (i, 0)), pl.BlockSpec((1, d), lambda i: (0, 0))],
        out_specs=pl.BlockSpec((tm, d), lambda i: (i, 0)),
        out_shape=jax.ShapeDtypeStruct((t, d), F32),
        compiler_params=_cparams("parallel"),
    )(x, g.reshape(1, d))


def _pad_cols(w, n):
    return jnp.pad(w, ((0, 0), (0, n - w.shape[1])))


def kernel(x, attn_norm_g, w_in, b_q_norm_g, b_w_uq, b_kv_norm_g, b_w_ukv, c_rel_bias, w_proj_a, w_proj_b, w_proj_c, w_out, ffn_norm_g, w_group, b_group, w_router, b_router, w_gate, w_up, w_down, final_norm_g):
    bsz, seq, d = x.shape
    assert (seq, d) == (SEQ, D_MODEL)
    t = bsz * seq
    depth = w_in.shape[0]
    tm = 512

    tab_a = _rope_table(A_ROT, A_HEAD_DIM)
    tab_i = _rope_table(IDX_ROT, IDX_DIM)
    tab_i_half = _rope_table(IDX_ROT, IDX_DIM, active_lanes=IDX_DIM)
    tab_b = _rope_table(B_ROPE, B_ROPE)
    tab_b_half = _rope_table(B_ROPE, B_ROPE, active_lanes=B_ROPE)

    xf = x.reshape(t, d)
    for l in range(depth):
        w = w_in[l]
        w_a = w[:, 0:768].astype(BF16)
        w_i = _pad_cols(w[:, 768:1352], 640).astype(BF16)
        w_b = _pad_cols(w[:, 1352:1800], 512).astype(BF16)
        w_c = jnp.concatenate([w[:, 2312:3336], w[:, 1800:2312]], axis=1).astype(BF16)
        w_g = w[:, 3336:6408].astype(BF16)
        g_attn = attn_norm_g[l]

        za = _norm_proj(xf, g_attn, w_a, BF16, tm=tm, tabs=(tab_a,), halves=(A_ROT // 2,),
                        tile_tab=(0, 0, 0, 0, 0, -1))
        zi = _norm_proj(xf, g_attn, w_i, F32, tm=tm, tabs=(tab_i, tab_i_half), halves=(IDX_ROT // 2,) * 2,
                        tile_tab=(0, 0, 0, 0, 1))
        zb = _norm_proj(xf, g_attn, w_b, F32, tm=tm, tabs=(tab_b_half,), halves=(B_ROPE // 2,),
                        tile_tab=(-1, -1, -1, 0))
        zc = _norm_proj(xf, g_attn, w_c, BF16, tm=tm)

        w_uq = b_w_uq[l].reshape(B_Q_RANK, B_HEADS, B_NOPE + B_ROPE)
        w_uq = jnp.concatenate([w_uq[:, :, :B_NOPE].reshape(B_Q_RANK, -1),
                                w_uq[:, :, B_NOPE:].reshape(B_Q_RANK, -1)], axis=1).astype(BF16)
        w_ukv = b_w_ukv[l].reshape(B_KV_RANK, B_HEADS, B_NOPE + B_V)
        w_ukv = jnp.concatenate([w_ukv[:, :, :B_NOPE].reshape(B_KV_RANK, -1),
                                 w_ukv[:, :, B_NOPE:].reshape(B_KV_RANK, -1)], axis=1).astype(BF16)
        qb = _norm_proj(zb, b_q_norm_g[l], w_uq, BF16, tm=tm, x_cols=0, tabs=(tab_b,),
                        halves=(B_ROPE // 2,), tile_tab=(-1, -1, -1, -1, 0, 0))
        kvb = _norm_proj(zb, b_kv_norm_g[l], w_ukv, BF16, tm=tm, x_cols=2)

        o_a = _dsa_attention(za.reshape(bsz, seq, -1), zi.reshape(bsz, seq, -1), bsz)
        o_b = _mla_attention(qb.reshape(bsz, seq, -1), kvb.reshape(bsz, seq, -1), zb.reshape(bsz, seq, -1), bsz)
        o_c = _band_attention(zc.reshape(bsz, seq, -1), _band_bias(c_rel_bias[l]), bsz)

        xf = _merge(xf, g_attn, w_g, o_a.reshape(t, -1), o_b.reshape(t, -1), o_c.reshape(t, -1),
                    w_proj_a[l].astype(BF16), w_proj_b[l].astype(BF16), w_proj_c[l].astype(BF16),
                    w_out[l].astype(BF16), tm=tm)

        w_r = _pad_cols(jnp.concatenate([w_router[l], w_group[l]], axis=1), LANES)
        b_r = _pad_cols(jnp.concatenate([b_router[l], b_group[l]])[None, :], LANES)
        xf = _moe(xf, ffn_norm_g[l], w_r, b_r, w_gate[l].astype(BF16), w_up[l].astype(BF16), w_down[l].astype(BF16))

    return _final_norm(xf, final_norm_g, tm=tm).reshape(bsz, seq, d)
```

```python
import functools

import numpy as np
import jax
import jax.numpy as jnp
from jax import lax
from jax.experimental import pallas as pl
from jax.experimental.pallas import tpu as pltpu

F32 = jnp.float32
BF16 = jnp.bfloat16

LANES = 128
D_MODEL = 1024
SEQ = 2048
CHUNK = 64
Q_BLOCK = 128
ROPE_THETA = 500000.0
EPS = 1e-6

A_HEADS = 4
A_HEAD_DIM = 128
A_ROT = 32
IDX_HEADS = 8
IDX_DIM = 64
IDX_ROT = 16
TOPK = 256
B_HEADS = 4
B_NOPE = 128
B_ROPE = 64
B_V = 128
B_Q_RANK = 256
B_KV_RANK = 128
C_HEADS = 4
C_HEAD_DIM = 128
C_LEFT_CHUNKS = 8
REL_CLIP = 128
N_GROUPS = 4
EXPERTS_PER_GROUP = 8
N_EXPERTS = 32
D_EXPERT = 256

C_KEY_BLOCKS = C_LEFT_CHUNKS * CHUNK // Q_BLOCK + 1
N_QB = SEQ // Q_BLOCK
KV_VARIANTS = 4
KV_STEP = SEQ // KV_VARIANTS

VMEM_LIMIT = 56 * 1024 * 1024

MOE_TILE = 2048
EXPERT_CHUNK = 256
XS_ROWS = 2 * MOE_TILE + 2 * EXPERT_CHUNK
assert XS_ROWS >= 2 * MOE_TILE + N_EXPERTS * 7 + EXPERT_CHUNK - 1
COMBINE_ROWS = 512
HIGH_HALF = -65536

INT_MIN = -2 ** 31
NEG_INF_KEY = int(np.array(0x807FFFFF, np.uint32).view(np.int32))

NT_DIMS = (((1,), (1,)), ((), ()))


def _nt_dot(a, b):
    return lax.dot_general(a, b, NT_DIMS, preferred_element_type=F32)


def _cparams(*sem):
    return pltpu.CompilerParams(dimension_semantics=sem, vmem_limit_bytes=VMEM_LIMIT)


def _rope_table(rot, period, active_lanes=LANES):
    half = rot // 2
    lane = np.arange(LANES)
    p = lane % period
    first = (p < half) & (lane < active_lanes)
    second = (p >= half) & (p < rot) & (lane < active_lanes)
    idx = np.where(first, p, np.where(second, p - half, 0))
    pos = jnp.arange(SEQ, dtype=F32)
    inv = ROPE_THETA ** (-jnp.arange(0, rot, 2, dtype=F32) / rot)
    ang = pos[:, None] * inv[idx][None, :]
    cos, sin = jnp.cos(ang), jnp.sin(ang)
    c = jnp.where(first | second, cos, 1.0)
    s_prev = jnp.where(second, sin, 0.0)
    s_next = jnp.where(first, -sin, 0.0)
    return jnp.stack([c, s_prev, s_next]).astype(F32)


def _rms(x, g):
    ms = jnp.mean(x * x, axis=-1, keepdims=True)
    return x * lax.rsqrt(ms + EPS) * g


def _rope_tiles(z, tile_tab, tabs, halves):
    out = []
    for c, t in enumerate(tile_tab):
        zt = z[:, c * LANES:(c + 1) * LANES]
        if t >= 0:
            tab, half = tabs[t], halves[t]
            zt = (zt * tab[0] + pltpu.roll(zt, half, 1) * tab[1]
                  + pltpu.roll(zt, LANES - half, 1) * tab[2])
        out.append(zt)
    return out


def _store_tiles(o_ref, tiles):
    for c, zt in enumerate(tiles):
        o_ref[:, c * LANES:(c + 1) * LANES] = zt.astype(o_ref.dtype)


ROPE_HALVES = (A_ROT // 2, IDX_ROT // 2, IDX_ROT // 2, B_ROPE // 2, B_ROPE // 2)


def _in_proj_kernel(x_ref, g_ref, wa_ref, wi_ref, wb_ref, wc_ref, gq_ref, wuq_ref, gkv_ref, wukv_ref,
                    ta_ref, ti_ref, tih_ref, tb_ref, tbh_ref,
                    za_ref, zi_ref, qb_ref, kvb_ref, kr_ref, zc_ref):
    tabs = (ta_ref, ti_ref, tih_ref, tb_ref, tbh_ref)
    rope = functools.partial(_rope_tiles, tabs=tabs, halves=ROPE_HALVES)
    h = _rms(x_ref[...], g_ref[...]).astype(BF16)
    dot = functools.partial(jnp.dot, preferred_element_type=F32)
    _store_tiles(za_ref, rope(dot(h, wa_ref[...]), (0, 0, 0, 0, 0, -1)))
    _store_tiles(zi_ref, rope(dot(h, wi_ref[...]), (1, 1, 1, 1, 2)))
    _store_tiles(zc_ref, rope(dot(h, wc_ref[...]), (-1,) * (3 * C_HEADS)))
    zb = dot(h, wb_ref[...])
    _store_tiles(kr_ref, rope(zb[:, B_Q_RANK + B_KV_RANK:], (4,)))
    cq = _rms(zb[:, :B_Q_RANK], gq_ref[...]).astype(BF16)
    _store_tiles(qb_ref, rope(dot(cq, wuq_ref[...]), (-1, -1, -1, -1, 3, 3)))
    ckv = _rms(zb[:, B_Q_RANK:B_Q_RANK + B_KV_RANK], gkv_ref[...]).astype(BF16)
    _store_tiles(kvb_ref, rope(dot(ckv, wukv_ref[...]), (-1,) * (2 * B_HEADS)))


def _in_proj(x, g, w_a, w_i, w_b, w_c, g_q, w_uq, g_kv, w_ukv, tabs, *, tm):
    t, d = x.shape
    seq_tiles = SEQ // tm
    fixed = lambda i: (0, 0)
    row = lambda i: (i, 0)
    weights = (w_a, w_i, w_b, w_c)
    outs = ((768, BF16), (640, F32), (768, BF16), (1024, BF16), (LANES, BF16), (1536, BF16))
    return pl.pallas_call(
        _in_proj_kernel,
        grid=(t // tm,),
        in_specs=[pl.BlockSpec((tm, d), row), pl.BlockSpec((1, d), fixed)]
        + [pl.BlockSpec(w.shape, fixed) for w in weights]
        + [pl.BlockSpec((1, B_Q_RANK), fixed), pl.BlockSpec(w_uq.shape, fixed),
           pl.BlockSpec((1, B_KV_RANK), fixed), pl.BlockSpec(w_ukv.shape, fixed)]
        + [pl.BlockSpec((3, tm, LANES), lambda i: (0, i % seq_tiles, 0)) for _ in tabs],
        out_specs=[pl.BlockSpec((tm, n), row) for n, _ in outs],
        out_shape=[jax.ShapeDtypeStruct((t, n), dt) for n, dt in outs],
        compiler_params=_cparams("parallel"),
    )(x, g.reshape(1, d), *weights, g_q.reshape(1, -1), w_uq, g_kv.reshape(1, -1), w_ukv, *tabs)


def _chunk_causal_mask(j, n_keys):
    qpos = j * Q_BLOCK + lax.broadcasted_iota(jnp.int32, (Q_BLOCK, n_keys), 0)
    kpos = lax.broadcasted_iota(jnp.int32, (Q_BLOCK, n_keys), 1)
    return (kpos >> 6) <= (qpos >> 6)


def _dsa_body(q_ref, kv_ref, iq_ref, ikw_ref, o_ref, bias_ref, n_keys):
    j = pl.program_id(1)
    row0 = pl.multiple_of(j * Q_BLOCK, Q_BLOCK)
    iq = iq_ref[...].astype(BF16)
    ik = ikw_ref[0:n_keys, 0:IDX_DIM].astype(BF16)
    iw = ikw_ref[pl.ds(row0, Q_BLOCK), IDX_DIM:IDX_DIM + IDX_HEADS] * (IDX_HEADS ** -0.5)
    score = jnp.zeros((Q_BLOCK, n_keys), F32)
    for h in range(IDX_HEADS):
        r = _nt_dot(iq[:, h * IDX_DIM:(h + 1) * IDX_DIM], ik)
        score = score + jnp.maximum(r * (IDX_DIM ** -0.5), 0.0) * iw[:, h:h + 1]
    allowed = _chunk_causal_mask(j, n_keys)
    score = jnp.where(score == 0.0, 0.0, score)
    score = jnp.where(allowed, score, -jnp.inf)
    bits = lax.bitcast_convert_type(score, jnp.int32)
    key = bits ^ ((bits >> 31) & 0x7FFFFFFF)

    def count_ge(cand):
        return jnp.sum(jnp.where(key >= cand, 1.0, 0.0), axis=1, keepdims=True)

    base = jnp.where(count_ge(jnp.zeros((Q_BLOCK, 1), jnp.int32)) >= TOPK, 0, INT_MIN).astype(jnp.int32)

    def search(i, base):
        cand = base | jnp.left_shift(jnp.int32(1), 30 - i)
        return jnp.where(count_ge(cand) >= TOPK, cand, base)

    thr = lax.fori_loop(0, 31, search, base)
    gt = key > thr
    eq = key == thr
    cnt_gt = jnp.sum(jnp.where(gt, 1.0, 0.0), axis=1, keepdims=True)
    cnt_eq = jnp.sum(jnp.where(eq, 1.0, 0.0), axis=1, keepdims=True)
    tie_rows = (cnt_gt + cnt_eq > TOPK) & (thr > NEG_INF_KEY)
    has_tie = jnp.max(jnp.where(tie_rows, 1.0, 0.0)) > 0.0

    @pl.when(jnp.logical_not(has_tie))
    def _():
        bias_ref[:, 0:n_keys] = jnp.where((gt | eq) & allowed, 0.0, -jnp.inf)

    @pl.when(has_tie)
    def _():
        need = TOPK - cnt_gt
        a = lax.broadcasted_iota(jnp.int32, (LANES, LANES), 0)
        b = lax.broadcasted_iota(jnp.int32, (LANES, LANES), 1)
        upper = jnp.where(a < b, 1.0, 0.0).astype(BF16)
        eqf = jnp.where(eq, 1.0, 0.0)
        carry = jnp.zeros((Q_BLOCK, 1), F32)
        for c in range(n_keys // LANES):
            sl = slice(c * LANES, (c + 1) * LANES)
            e = eqf[:, sl]
            before = jnp.dot(e.astype(BF16), upper, preferred_element_type=F32) + carry
            keep = gt[:, sl] | (eq[:, sl] & (before < need))
            bias_ref[:, sl] = jnp.where(keep & allowed[:, sl], 0.0, -jnp.inf)
            carry = carry + jnp.sum(e, axis=1, keepdims=True)

    q = q_ref[...]
    qs = jnp.concatenate([q[:, h * A_HEAD_DIM:(h + 1) * A_HEAD_DIM] for h in range(A_HEADS)], axis=0)
    k = kv_ref[0:n_keys, 0:A_HEAD_DIM]
    v = kv_ref[0:n_keys, A_HEAD_DIM:2 * A_HEAD_DIM]
    bias = bias_ref[:, 0:n_keys]
    for h in range(A_HEADS):
        logits = _nt_dot(qs[h * Q_BLOCK:(h + 1) * Q_BLOCK], k) * (A_HEAD_DIM ** -0.5) + bias
        m = jnp.max(logits, axis=-1, keepdims=True)
        p = jnp.exp(logits - m)
        l = jnp.sum(p, axis=-1, keepdims=True)
        o = jnp.dot(p.astype(BF16), v, preferred_element_type=F32) / l
        o_ref[:, h * A_HEAD_DIM:(h + 1) * A_HEAD_DIM] = o.astype(o_ref.dtype)


def _dsa_kernel(q_ref, kv_ref, iq_ref, ikw_ref, o_ref, bias_ref):
    j = pl.program_id(1)
    for v in range(KV_VARIANTS):
        @pl.when(j // (N_QB // KV_VARIANTS) == v)
        def _(v=v):
            _dsa_body(q_ref, kv_ref, iq_ref, ikw_ref, o_ref, bias_ref, KV_STEP * (v + 1))


def _dsa_attention(za, zi, bsz):
    return pl.pallas_call(
        _dsa_kernel,
        grid=(bsz, N_QB),
        in_specs=[
            pl.BlockSpec((None, Q_BLOCK, 512), lambda b, j: (b, j, 0)),
            pl.BlockSpec((None, SEQ, 256), lambda b, j: (b, 0, 2)),
            pl.BlockSpec((None, Q_BLOCK, 512), lambda b, j: (b, j, 0)),
            pl.BlockSpec((None, SEQ, LANES), lambda b, j: (b, 0, 4)),
        ],
        out_specs=pl.BlockSpec((None, Q_BLOCK, 512), lambda b, j: (b, j, 0)),
        out_shape=jax.ShapeDtypeStruct((bsz, SEQ, 512), BF16),
        scratch_shapes=[pltpu.VMEM((Q_BLOCK, SEQ), F32)],
        compiler_params=_cparams("parallel", "arbitrary"),
    )(za, za, zi, zi)


def _mla_body(q_ref, kv_ref, kr_ref, o_ref, n_keys):
    j = pl.program_id(1)
    mask = _chunk_causal_mask(j, n_keys)
    kr = kr_ref[0:n_keys, 0:B_ROPE]
    scale = (B_NOPE + B_ROPE) ** -0.5
    for h in range(B_HEADS):
        qn = q_ref[:, h * B_NOPE:(h + 1) * B_NOPE]
        qr = q_ref[:, B_HEADS * B_NOPE + h * B_ROPE:B_HEADS * B_NOPE + (h + 1) * B_ROPE]
        kn = kv_ref[0:n_keys, h * B_NOPE:(h + 1) * B_NOPE]
        vv = kv_ref[0:n_keys, B_HEADS * B_NOPE + h * B_V:B_HEADS * B_NOPE + (h + 1) * B_V]
        s = (_nt_dot(qn, kn) + _nt_dot(qr, kr)) * scale
        s = jnp.where(mask, s, -jnp.inf)
        m = jnp.max(s, axis=-1, keepdims=True)
        p = jnp.exp(s - m)
        l = jnp.sum(p, axis=-1, keepdims=True)
        o = jnp.dot(p.astype(BF16), vv, preferred_element_type=F32) / l
        o_ref[:, h * B_V:(h + 1) * B_V] = o.astype(o_ref.dtype)


def _mla_kernel(q_ref, kv_ref, kr_ref, o_ref):
    j = pl.program_id(1)
    for v in range(KV_VARIANTS):
        @pl.when(j // (N_QB // KV_VARIANTS) == v)
        def _(v=v):
            _mla_body(q_ref, kv_ref, kr_ref, o_ref, KV_STEP * (v + 1))


def _mla_attention(qb, kvb, kr, bsz):
    return pl.pallas_call(
        _mla_kernel,
        grid=(bsz, N_QB),
        in_specs=[
            pl.BlockSpec((None, Q_BLOCK, 768), lambda b, j: (b, j, 0)),
            pl.BlockSpec((None, SEQ, 1024), lambda b, j: (b, 0, 0)),
            pl.BlockSpec((None, SEQ, LANES), lambda b, j: (b, 0, 0)),
        ],
        out_specs=pl.BlockSpec((None, Q_BLOCK, 512), lambda b, j: (b, j, 0)),
        out_shape=jax.ShapeDtypeStruct((bsz, SEQ, 512), BF16),
        compiler_params=_cparams("parallel", "arbitrary"),
    )(qb, kvb, kr)


def _band_bias(rel_table):
    n = 2 * REL_CLIP + 1
    period = 2 * n - 1
    heads = rel_table.shape[0]
    ext = jnp.concatenate([rel_table, jnp.broadcast_to(rel_table[:, n - 1:n], (heads, n - 1))], axis=1)
    kj = np.arange(Q_BLOCK)[:, None]
    qi = np.arange(Q_BLOCK)[None, :]
    out = []
    for d in range(C_KEY_BLOCKS):
        base = d * Q_BLOCK + REL_CLIP
        if base - (Q_BLOCK - 1) >= n - 1:
            bias = jnp.broadcast_to(rel_table[:, n - 1][:, None, None], (heads, Q_BLOCK, Q_BLOCK))
        else:
            shifted = jnp.roll(ext, -base, axis=1)
            bias = jnp.tile(shifted, (1, Q_BLOCK))[:, :Q_BLOCK * (period - 1)]
            bias = bias.reshape(heads, Q_BLOCK, period - 1)[:, :, :Q_BLOCK]
        cdiff = 2 * d + qi // CHUNK - kj // CHUNK
        valid = (cdiff >= 0) & (cdiff <= C_LEFT_CHUNKS)
        out.append(jnp.where(valid[None], bias.astype(F32), -jnp.inf))
    return jnp.swapaxes(jnp.stack(out, axis=1), 2, 3)


def _band_kernel(q_ref, kv_ref, bias_ref, o_ref):
    j = pl.program_id(1)
    scale = C_HEAD_DIM ** -0.5
    for h in range(C_HEADS):
        q = q_ref[:, h * C_HEAD_DIM:(h + 1) * C_HEAD_DIM]
        ss, vs = [], []
        for d in range(C_KEY_BLOCKS):
            kb = j - d
            row0 = pl.multiple_of(jnp.maximum(kb, 0) * Q_BLOCK, Q_BLOCK)
            k = kv_ref[pl.ds(row0, Q_BLOCK), h * C_HEAD_DIM:(h + 1) * C_HEAD_DIM]
            vs.append(kv_ref[pl.ds(row0, Q_BLOCK), (C_HEADS + h) * C_HEAD_DIM:(C_HEADS + h + 1) * C_HEAD_DIM])
            s = _nt_dot(q, k) * scale + bias_ref[h, d]
            ss.append(jnp.where(kb >= 0, s, -jnp.inf))
        s_all = jnp.concatenate(ss, axis=1)
        v_all = jnp.concatenate(vs, axis=0)
        m = jnp.max(s_all, axis=-1, keepdims=True)
        p = jnp.exp(s_all - m)
        l = jnp.sum(p, axis=-1, keepdims=True)
        o = jnp.dot(p.astype(BF16), v_all, preferred_element_type=F32) / l
        o_ref[:, h * C_HEAD_DIM:(h + 1) * C_HEAD_DIM] = o.astype(o_ref.dtype)


def _band_attention(zc, bias, bsz):
    return pl.pallas_call(
        _band_kernel,
        grid=(bsz, N_QB),
        in_specs=[
            pl.BlockSpec((None, Q_BLOCK, 512), lambda b, j: (b, j, 2)),
            pl.BlockSpec((None, SEQ, 1024), lambda b, j: (b, 0, 0)),
            pl.BlockSpec((C_HEADS, C_KEY_BLOCKS, Q_BLOCK, Q_BLOCK), lambda b, j: (0, 0, 0, 0)),
        ],
        out_specs=pl.BlockSpec((None, Q_BLOCK, 512), lambda b, j: (b, j, 0)),
        out_shape=jax.ShapeDtypeStruct((bsz, SEQ, 512), BF16),
        compiler_params=_cparams("parallel", "arbitrary"),
    )(zc, zc, bias)


def _merge_kernel(x_ref, g_ref, wgl_ref, oa_ref, ob_ref, oc_ref, wpa_ref, wpb_ref, wpc_ref, wout_ref, o_ref):
    x = x_ref[...]
    ms = jnp.mean(x * x, axis=-1, keepdims=True)
    h = (x * lax.rsqrt(ms + EPS) * g_ref[...]).astype(BF16)
    mix = jnp.zeros(x.shape, F32)
    for i, (o_in, wp) in enumerate(((oa_ref, wpa_ref), (ob_ref, wpb_ref), (oc_ref, wpc_ref))):
        gl = jnp.dot(h, wgl_ref[:, i * D_MODEL:(i + 1) * D_MODEL], preferred_element_type=F32)
        gate = jax.nn.sigmoid(gl)
        mix = mix + gate * jnp.dot(o_in[...], wp[...], preferred_element_type=F32)
    o_ref[...] = x + jnp.dot(mix.astype(BF16), wout_ref[...], preferred_element_type=F32)


def _merge(x, g, wgl, oa, ob, oc, wpa, wpb, wpc, wout, *, tm):
    t, d = x.shape
    row = lambda i: (i, 0)
    fixed = lambda i: (0, 0)
    return pl.pallas_call(
        _merge_kernel,
        grid=(t // tm,),
        in_specs=[
            pl.BlockSpec((tm, d), row),
            pl.BlockSpec((1, d), fixed),
            pl.BlockSpec((d, 3 * d), fixed),
            pl.BlockSpec((tm, 512), row),
            pl.BlockSpec((tm, 512), row),
            pl.BlockSpec((tm, 512), row),
            pl.BlockSpec((512, d), fixed),
            pl.BlockSpec((512, d), fixed),
            pl.BlockSpec((512, d), fixed),
            pl.BlockSpec((d, d), fixed),
        ],
        out_specs=pl.BlockSpec((tm, d), row),
        out_shape=jax.ShapeDtypeStruct((t, d), F32),
        compiler_params=_cparams("parallel"),
    )(x, g.reshape(1, d), wgl, oa, ob, oc, wpa, wpb, wpc, wout)


def _first_argmax(vals, lane):
    m = jnp.max(vals, axis=-1, keepdims=True)
    idx = jnp.min(jnp.where(vals == m, lane, LANES), axis=-1, keepdims=True)
    return m, idx


def _pack_bf16_pairs(h):
    n = h.shape[1] // 2
    bits = lax.bitcast_convert_type(h.astype(jnp.bfloat16).astype(F32), jnp.int32)
    return lax.shift_right_logical(bits[:, :n], 16) | bits[:, n:]


def _unpack_bf16_pairs(w):
    lo = lax.bitcast_convert_type(w << 16, F32).astype(BF16)
    hi = lax.bitcast_convert_type(w & HIGH_HALF, F32).astype(BF16)
    return lo, hi


def _router_kernel(x_ref, g_ref, w_ref, b_ref, hp_ref, meta_ref, seg_ref):
    x = x_ref[...]
    ms = jnp.mean(x * x, axis=-1, keepdims=True)
    h = x * lax.rsqrt(ms + EPS) * g_ref[...]
    hp_ref[...] = _pack_bf16_pairs(h)
    logits = jnp.dot(h, w_ref[...], preferred_element_type=F32, precision=lax.Precision.HIGHEST) + b_ref[...]
    lane = lax.broadcasted_iota(jnp.int32, logits.shape, 1)
    is_grp = (lane >= N_EXPERTS) & (lane < N_EXPERTS + N_GROUPS)
    gl = jnp.where(is_grp, logits, -jnp.inf)
    gmax, gidx = _first_argmax(gl, lane)
    pg = 1.0 / jnp.sum(jnp.exp(gl - gmax), axis=-1, keepdims=True)
    gsel = gidx - N_EXPERTS
    in_grp = (lane >> 3) == gsel
    el = jnp.where(in_grp, logits, -jnp.inf)
    m1, i1 = _first_argmax(el, lane)
    z = jnp.sum(jnp.exp(el - m1), axis=-1, keepdims=True)
    el2 = jnp.where(lane == i1, -jnp.inf, el)
    m2, i2 = _first_argmax(el2, lane)
    pe1 = 1.0 / z
    pe2 = jnp.exp(m2 - m1) / z
    den = pe1 + pe2
    w1 = pg * pe1 / den
    w2 = pg * pe2 / den

    sel1 = lane == i1
    sel2 = lane == i2
    onehot = jnp.where(sel1 | sel2, 1.0, 0.0)
    a = lax.broadcasted_iota(jnp.int32, (LANES, LANES), 0)
    b = lax.broadcasted_iota(jnp.int32, (LANES, LANES), 1)
    lower = jnp.where(b < a, 1.0, 0.0).astype(BF16)
    carry = jnp.zeros((1, LANES), F32)
    ranks = []
    for c in range(MOE_TILE // LANES):
        blk = onehot[c * LANES:(c + 1) * LANES]
        ranks.append(jnp.dot(lower, blk.astype(BF16), preferred_element_type=F32) + carry)
        carry = carry + jnp.sum(blk, axis=0, keepdims=True)
    rank = jnp.concatenate(ranks, axis=0)
    cnt = jnp.broadcast_to(carry, (8, LANES))
    seg = jnp.floor((cnt + 7.0) * 0.125) * 8.0
    lane8 = lax.broadcasted_iota(jnp.int32, (8, LANES), 1)
    scan = seg
    for k in (1, 2, 4, 8, 16, 32, 64):
        scan = scan + jnp.where(lane8 >= k, pltpu.roll(scan, k, 1), 0.0)
    off = scan - seg
    where_row = rank + off[0:1]
    pos1 = jnp.sum(jnp.where(sel1, where_row, 0.0), axis=-1, keepdims=True)
    pos2 = jnp.sum(jnp.where(sel2, where_row, 0.0), axis=-1, keepdims=True)
    meta_ref[...] = (jnp.where(lane == 0, pos1, 0.0) + jnp.where(lane == 1, pos2, 0.0)
                     + jnp.where(lane == 2, w1, 0.0) + jnp.where(lane == 3, w2, 0.0))
    row8 = lax.broadcasted_iota(jnp.int32, (8, LANES), 0)
    seg_ref[...] = jnp.where(row8 == 0, cnt, jnp.where(row8 == 1, off, 0.0)).astype(jnp.int32)


def _router(x, g, w, b):
    t, d = x.shape
    nt = t // MOE_TILE
    return pl.pallas_call(
        _router_kernel,
        grid=(nt,),
        in_specs=[
            pl.BlockSpec((MOE_TILE, d), lambda i: (i, 0)),
            pl.BlockSpec((1, d), lambda i: (0, 0)),
            pl.BlockSpec((d, LANES), lambda i: (0, 0)),
            pl.BlockSpec((1, LANES), lambda i: (0, 0)),
        ],
        out_specs=[
            pl.BlockSpec((MOE_TILE, d // 2), lambda i: (i, 0)),
            pl.BlockSpec((MOE_TILE, LANES), lambda i: (i, 0)),
            pl.BlockSpec((None, 8, LANES), lambda i: (i, 0, 0)),
        ],
        out_shape=[
            jax.ShapeDtypeStruct((t, d // 2), jnp.int32),
            jax.ShapeDtypeStruct((t, LANES), F32),
            jax.ShapeDtypeStruct((nt, 8, LANES), jnp.int32),
        ],
        compiler_params=_cparams("parallel"),
    )(x, g.reshape(1, d), w, b)


def _scatter_kernel(pos_ref, hp_ref, xs_ref):
    xs_ref[...] = jnp.zeros_like(xs_ref)

    def body(t, carry):
        row = hp_ref[pl.ds(t, 1), :]
        xs_ref[pl.ds(pos_ref[0, t], 1), :] = row
        xs_ref[pl.ds(pos_ref[0, MOE_TILE + t], 1), :] = row
        return carry

    lax.fori_loop(0, MOE_TILE, body, 0, unroll=8)


def _scatter_rows(pos, hp):
    nt = pos.shape[0]
    return pl.pallas_call(
        _scatter_kernel,
        grid=(nt,),
        in_specs=[
            pl.BlockSpec((None, 1, 2 * MOE_TILE), lambda i: (i, 0, 0), memory_space=pltpu.SMEM),
            pl.BlockSpec((MOE_TILE, hp.shape[1]), lambda i: (i, 0)),
        ],
        out_specs=pl.BlockSpec((None, XS_ROWS, hp.shape[1]), lambda i: (i, 0, 0)),
        out_shape=jax.ShapeDtypeStruct((nt, XS_ROWS, hp.shape[1]), jnp.int32),
        compiler_params=_cparams("parallel"),
    )(pos, hp)


def _experts_kernel(cnt_ref, off_ref, xs_ref, wg_ref, wu_ref, wd_ref, ys_ref):
    i = pl.program_id(0)
    e = pl.program_id(1)
    half = D_MODEL // 2

    @pl.when(e == 0)
    def _():
        ys_ref[2 * MOE_TILE:XS_ROWS, :] = jnp.zeros((XS_ROWS - 2 * MOE_TILE, D_MODEL), F32)

    n = cnt_ref[i, e]
    off = off_ref[i, e]

    def body(c, carry):
        start = pl.multiple_of(off + c * EXPERT_CHUNK, 8)
        lo, hi = _unpack_bf16_pairs(xs_ref[pl.ds(start, EXPERT_CHUNK), :])
        a = (jnp.dot(lo, wg_ref[0:half].astype(BF16), preferred_element_type=F32)
             + jnp.dot(hi, wg_ref[half:D_MODEL].astype(BF16), preferred_element_type=F32))
        u = (jnp.dot(lo, wu_ref[0:half].astype(BF16), preferred_element_type=F32)
             + jnp.dot(hi, wu_ref[half:D_MODEL].astype(BF16), preferred_element_type=F32))
        hh = (a * jax.nn.sigmoid(a)) * u
        ys_ref[pl.ds(start, EXPERT_CHUNK), :] = jnp.dot(hh.astype(BF16), wd_ref[...].astype(BF16),
                                                        preferred_element_type=F32)
        return carry

    lax.fori_loop(0, (n + EXPERT_CHUNK - 1) // EXPERT_CHUNK, body, 0)


def _experts(cnt, off, xs, wg, wu, wd):
    nt = xs.shape[0]
    d = D_MODEL
    grid_spec = pltpu.PrefetchScalarGridSpec(
        num_scalar_prefetch=2,
        grid=(nt, N_EXPERTS),
        in_specs=[
            pl.BlockSpec((None, XS_ROWS, d // 2), lambda i, e, c, o: (i, 0, 0), pipeline_mode=pl.Buffered(1)),
            pl.BlockSpec((None, d, D_EXPERT), lambda i, e, c, o: (e, 0, 0)),
            pl.BlockSpec((None, d, D_EXPERT), lambda i, e, c, o: (e, 0, 0)),
            pl.BlockSpec((None, D_EXPERT, d), lambda i, e, c, o: (e, 0, 0)),
        ],
        out_specs=pl.BlockSpec((None, XS_ROWS, d), lambda i, e, c, o: (i, 0, 0)),
    )
    return pl.pallas_call(
        _experts_kernel,
        grid_spec=grid_spec,
        out_shape=jax.ShapeDtypeStruct((nt, XS_ROWS, d), F32),
        compiler_params=_cparams("parallel", "arbitrary"),
    )(cnt, off, xs, wg, wu, wd)


def _combine_kernel(pos_ref, wt_ref, x_ref, ys_ref, g_ref, o_ref, *, final_norm):
    s = pl.program_id(1)

    def body(tl, carry):
        t = s * COMBINE_ROWS + tl
        y = (ys_ref[pl.ds(pos_ref[0, t], 1), :] * wt_ref[0, t]
             + ys_ref[pl.ds(pos_ref[0, MOE_TILE + t], 1), :] * wt_ref[0, MOE_TILE + t])
        o_ref[pl.ds(tl, 1), :] = x_ref[pl.ds(tl, 1), :] + y
        return carry

    lax.fori_loop(0, COMBINE_ROWS, body, 0, unroll=8)
    if final_norm:
        o_ref[...] = _rms(o_ref[...], g_ref[...])


def _combine(pos, wt, x, ys, final_g):
    t, d = x.shape
    nt = pos.shape[0]
    sub = MOE_TILE // COMBINE_ROWS
    g = jnp.ones((d,), F32) if final_g is None else final_g
    return pl.pallas_call(
        functools.partial(_combine_kernel, final_norm=final_g is not None),
        grid=(nt, sub),
        in_specs=[
            pl.BlockSpec((None, 1, 2 * MOE_TILE), lambda i, s: (i, 0, 0), memory_space=pltpu.SMEM),
            pl.BlockSpec((None, 1, 2 * MOE_TILE), lambda i, s: (i, 0, 0), memory_space=pltpu.SMEM),
            pl.BlockSpec((COMBINE_ROWS, d), lambda i, s: (i * sub + s, 0)),
            pl.BlockSpec((None, XS_ROWS, d), lambda i, s: (i, 0, 0)),
            pl.BlockSpec((1, d), lambda i, s: (0, 0)),
        ],
        out_specs=pl.BlockSpec((COMBINE_ROWS, d), lambda i, s: (i * sub + s, 0)),
        out_shape=jax.ShapeDtypeStruct((t, d), F32),
        compiler_params=_cparams("parallel", "arbitrary"),
    )(pos, wt, x, ys, g.reshape(1, d))


def _moe(x, g, w_r, b_r, wg, wu, wd, final_g=None):
    t = x.shape[0]
    nt = t // MOE_TILE
    hp, meta, seg = _router(x, g, w_r, b_r)
    pair_major = lambda m: m.reshape(nt, MOE_TILE, 2).transpose(0, 2, 1).reshape(nt, 1, 2 * MOE_TILE)
    pos = pair_major(meta[:, 0:2].astype(jnp.int32))
    wt = pair_major(meta[:, 2:4])
    xs = _scatter_rows(pos, hp)
    ys = _experts(seg[:, 0, :N_EXPERTS], seg[:, 1, :N_EXPERTS], xs, wg, wu, wd)
    return _combine(pos, wt, x, ys, final_g)


def _pad_cols(w, n):
    return jnp.pad(w, ((0, 0), (0, n - w.shape[1])))


def kernel(x, attn_norm_g, w_in, b_q_norm_g, b_w_uq, b_kv_norm_g, b_w_ukv, c_rel_bias, w_proj_a, w_proj_b, w_proj_c, w_out, ffn_norm_g, w_group, b_group, w_router, b_router, w_gate, w_up, w_down, final_norm_g):
    bsz, seq, d = x.shape
    assert (seq, d) == (SEQ, D_MODEL)
    t = bsz * seq
    depth = w_in.shape[0]
    tm = 512

    tab_a = _rope_table(A_ROT, A_HEAD_DIM)
    tab_i = _rope_table(IDX_ROT, IDX_DIM)
    tab_i_half = _rope_table(IDX_ROT, IDX_DIM, active_lanes=IDX_DIM)
    tab_b = _rope_table(B_ROPE, B_ROPE)
    tab_b_half = _rope_table(B_ROPE, B_ROPE, active_lanes=B_ROPE)

    tabs = (tab_a, tab_i, tab_i_half, tab_b, tab_b_half)
    xf = x.reshape(t, d)
    for l in range(depth):
        w = w_in[l]
        w_a = w[:, 0:768].astype(BF16)
        w_i = _pad_cols(w[:, 768:1352], 640).astype(BF16)
        w_b = _pad_cols(w[:, 1352:1800], 512).astype(BF16)
        w_c = jnp.concatenate([w[:, 2312:3336], w[:, 1800:2312]], axis=1).astype(BF16)
        w_g = w[:, 3336:6408].astype(BF16)
        g_attn = attn_norm_g[l]
        w_uq = b_w_uq[l].reshape(B_Q_RANK, B_HEADS, B_NOPE + B_ROPE)
        w_uq = jnp.concatenate([w_uq[:, :, :B_NOPE].reshape(B_Q_RANK, -1),
                                w_uq[:, :, B_NOPE:].reshape(B_Q_RANK, -1)], axis=1).astype(BF16)
        w_ukv = b_w_ukv[l].reshape(B_KV_RANK, B_HEADS, B_NOPE + B_V)
        w_ukv = jnp.concatenate([w_ukv[:, :, :B_NOPE].reshape(B_KV_RANK, -1),
                                 w_ukv[:, :, B_NOPE:].reshape(B_KV_RANK, -1)], axis=1).astype(BF16)

        za, zi, qb, kvb, kr, zc = _in_proj(xf, g_attn, w_a, w_i, w_b, w_c, b_q_norm_g[l], w_uq,
                                           b_kv_norm_g[l], w_ukv, tabs, tm=tm)
        per_seq = lambda z: z.reshape(bsz, seq, -1)
        o_a = _dsa_attention(per_seq(za), per_seq(zi), bsz)
        o_b = _mla_attention(per_seq(qb), per_seq(kvb), per_seq(kr), bsz)
        o_c = _band_attention(per_seq(zc), _band_bias(c_rel_bias[l]), bsz)

        xf = _merge(xf, g_attn, w_g, o_a.reshape(t, -1), o_b.reshape(t, -1), o_c.reshape(t, -1),
                    w_proj_a[l].astype(BF16), w_proj_b[l].astype(BF16), w_proj_c[l].astype(BF16),
                    w_out[l].astype(BF16), tm=tm)

        w_r = _pad_cols(jnp.concatenate([w_router[l], w_group[l]], axis=1), LANES)
        b_r = _pad_cols(jnp.concatenate([b_router[l], b_group[l]])[None, :], LANES)
        xf = _moe(xf, ffn_norm_g[l], w_r, b_r, w_gate[l], w_up[l], w_down[l],
                  final_g=final_norm_g if l == depth - 1 else None)

    return xf.reshape(bsz, seq, d)
```

```python
import functools

import numpy as np
import jax
import jax.numpy as jnp
from jax import lax
from jax.experimental import pallas as pl
from jax.experimental.pallas import tpu as pltpu

F32 = jnp.float32
BF16 = jnp.bfloat16

LANES = 128
D_MODEL = 1024
SEQ = 2048
CHUNK = 64
Q_BLOCK = 128
ROPE_THETA = 500000.0
EPS = 1e-6

A_HEADS = 4
A_HEAD_DIM = 128
A_ROT = 32
IDX_HEADS = 8
IDX_DIM = 64
IDX_ROT = 16
TOPK = 256
B_HEADS = 4
B_NOPE = 128
B_ROPE = 64
B_V = 128
B_Q_RANK = 256
B_KV_RANK = 128
C_HEADS = 4
C_HEAD_DIM = 128
C_LEFT_CHUNKS = 8
REL_CLIP = 128
N_GROUPS = 4
EXPERTS_PER_GROUP = 8
N_EXPERTS = 32
D_EXPERT = 256

C_KEY_BLOCKS = C_LEFT_CHUNKS * CHUNK // Q_BLOCK + 1
N_QB = SEQ // Q_BLOCK
KV_VARIANTS = 4
KV_STEP = SEQ // KV_VARIANTS

VMEM_LIMIT = 56 * 1024 * 1024

MOE_TILE = 2048
EXPERT_CHUNK = 256
XS_ROWS = 2 * MOE_TILE + 2 * EXPERT_CHUNK
assert XS_ROWS >= 2 * MOE_TILE + N_EXPERTS * 7 + EXPERT_CHUNK - 1
COMBINE_ROWS = 512
HIGH_HALF = -65536

INT_MIN = -2 ** 31
NEG_INF_KEY = int(np.array(0x807FFFFF, np.uint32).view(np.int32))

NT_DIMS = (((1,), (1,)), ((), ()))


def _nt_dot(a, b):
    return lax.dot_general(a, b, NT_DIMS, preferred_element_type=F32)


def _cparams(*sem):
    return pltpu.CompilerParams(dimension_semantics=sem, vmem_limit_bytes=VMEM_LIMIT)


def _rope_table(rot, period, active_lanes=LANES):
    half = rot // 2
    lane = np.arange(LANES)
    p = lane % period
    first = (p < half) & (lane < active_lanes)
    second = (p >= half) & (p < rot) & (lane < active_lanes)
    idx = np.where(first, p, np.where(second, p - half, 0))
    pos = jnp.arange(SEQ, dtype=F32)
    inv = ROPE_THETA ** (-jnp.arange(0, rot, 2, dtype=F32) / rot)
    ang = pos[:, None] * inv[idx][None, :]
    cos, sin = jnp.cos(ang), jnp.sin(ang)
    c = jnp.where(first | second, cos, 1.0)
    s_prev = jnp.where(second, sin, 0.0)
    s_next = jnp.where(first, -sin, 0.0)
    return jnp.stack([c, s_prev, s_next]).astype(F32)


def _rms(x, g):
    ms = jnp.mean(x * x, axis=-1, keepdims=True)
    return x * lax.rsqrt(ms + EPS) * g


def _rope_tiles(z, tile_tab, tabs, halves):
    out = []
    for c, t in enumerate(tile_tab):
        zt = z[:, c * LANES:(c + 1) * LANES]
        if t >= 0:
            tab, half = tabs[t], halves[t]
            zt = (zt * tab[0] + pltpu.roll(zt, half, 1) * tab[1]
                  + pltpu.roll(zt, LANES - half, 1) * tab[2])
        out.append(zt)
    return out


def _store_tiles(o_ref, tiles):
    for c, zt in enumerate(tiles):
        o_ref[:, c * LANES:(c + 1) * LANES] = zt.astype(o_ref.dtype)


ROPE_HALVES = (A_ROT // 2, IDX_ROT // 2, IDX_ROT // 2, B_ROPE // 2, B_ROPE // 2)


def _in_proj_kernel(x_ref, g_ref, wa_ref, wi_ref, wb_ref, wc_ref, gq_ref, wuq_ref, gkv_ref, wukv_ref,
                    ta_ref, ti_ref, tih_ref, tb_ref, tbh_ref,
                    za_ref, zi_ref, qb_ref, kvb_ref, kr_ref, zc_ref):
    tabs = (ta_ref, ti_ref, tih_ref, tb_ref, tbh_ref)
    rope = functools.partial(_rope_tiles, tabs=tabs, halves=ROPE_HALVES)
    h = _rms(x_ref[...], g_ref[...]).astype(BF16)
    dot = functools.partial(jnp.dot, preferred_element_type=F32)
    _store_tiles(za_ref, rope(dot(h, wa_ref[...]), (0, 0, 0, 0, 0, -1)))
    _store_tiles(zi_ref, rope(dot(h, wi_ref[...]), (1, 1, 1, 1, 2)))
    _store_tiles(zc_ref, rope(dot(h, wc_ref[...]), (-1,) * (3 * C_HEADS)))
    zb = dot(h, wb_ref[...])
    _store_tiles(kr_ref, rope(zb[:, B_Q_RANK + B_KV_RANK:], (4,)))
    cq = _rms(zb[:, :B_Q_RANK], gq_ref[...]).astype(BF16)
    _store_tiles(qb_ref, rope(dot(cq, wuq_ref[...]), (-1, -1, -1, -1, 3, 3)))
    ckv = _rms(zb[:, B_Q_RANK:B_Q_RANK + B_KV_RANK], gkv_ref[...]).astype(BF16)
    _store_tiles(kvb_ref, rope(dot(ckv, wukv_ref[...]), (-1,) * (2 * B_HEADS)))


def _in_proj(x, g, w_a, w_i, w_b, w_c, g_q, w_uq, g_kv, w_ukv, tabs, *, tm):
    t, d = x.shape
    seq_tiles = SEQ // tm
    fixed = lambda i: (0, 0)
    row = lambda i: (i, 0)
    weights = (w_a, w_i, w_b, w_c)
    outs = ((768, BF16), (640, F32), (768, BF16), (1024, BF16), (LANES, BF16), (1536, BF16))
    return pl.pallas_call(
        _in_proj_kernel,
        grid=(t // tm,),
        in_specs=[pl.BlockSpec((tm, d), row), pl.BlockSpec((1, d), fixed)]
        + [pl.BlockSpec(w.shape, fixed) for w in weights]
        + [pl.BlockSpec((1, B_Q_RANK), fixed), pl.BlockSpec(w_uq.shape, fixed),
           pl.BlockSpec((1, B_KV_RANK), fixed), pl.BlockSpec(w_ukv.shape, fixed)]
        + [pl.BlockSpec((3, tm, LANES), lambda i: (0, i % seq_tiles, 0)) for _ in tabs],
        out_specs=[pl.BlockSpec((tm, n), row) for n, _ in outs],
        out_shape=[jax.ShapeDtypeStruct((t, n), dt) for n, dt in outs],
        compiler_params=_cparams("parallel"),
    )(x, g.reshape(1, d), *weights, g_q.reshape(1, -1), w_uq, g_kv.reshape(1, -1), w_ukv, *tabs)


def _chunk_causal_mask(j, n_keys):
    qpos = j * Q_BLOCK + lax.broadcasted_iota(jnp.int32, (Q_BLOCK, n_keys), 0)
    kpos = lax.broadcasted_iota(jnp.int32, (Q_BLOCK, n_keys), 1)
    return (kpos >> 6) <= (qpos >> 6)


SEARCH_ROWS = 256


def _dsa_select(iq_ref, ikw_ref, key_ref, bias_ref):
    j = pl.program_id(1)
    n_blk = j + 1
    n_grp = (j + 2) // 2
    row0 = pl.multiple_of(j * Q_BLOCK, Q_BLOCK)
    sub = lax.broadcasted_iota(jnp.int32, (SEARCH_ROWS, Q_BLOCK), 0)
    lane = lax.broadcasted_iota(jnp.int32, (SEARCH_ROWS, Q_BLOCK), 1)
    q_chunk = (row0 + lane) >> 6

    iq = iq_ref[...].astype(BF16)
    iq_stack = jnp.concatenate([iq[:, h * IDX_DIM:(h + 1) * IDX_DIM] for h in range(IDX_HEADS)], axis=0)
    iw_t = ikw_ref[pl.ds(row0, Q_BLOCK), :].T * (IDX_HEADS ** -0.5)

    def score_group(g, carry):
        k0 = pl.multiple_of(g * SEARCH_ROWS, SEARCH_ROWS)
        ik = ikw_ref[pl.ds(k0, SEARCH_ROWS), 0:IDX_DIM].astype(BF16)
        score = jnp.zeros((SEARCH_ROWS, Q_BLOCK), F32)
        for hp in range(IDX_HEADS // 2):
            r = _nt_dot(ik, iq_stack[hp * 2 * Q_BLOCK:(hp + 1) * 2 * Q_BLOCK])
            for u in range(2):
                h = 2 * hp + u
                rel = jnp.maximum(r[:, u * Q_BLOCK:(u + 1) * Q_BLOCK] * (IDX_DIM ** -0.5), 0.0)
                score = score + rel * iw_t[IDX_DIM + h:IDX_DIM + h + 1, :]
        allowed = ((k0 + sub) >> 6) <= q_chunk
        score = jnp.where(score == 0.0, 0.0, score)
        score = jnp.where(allowed, score, -jnp.inf)
        bits = lax.bitcast_convert_type(score, jnp.int32)
        key_ref[pl.ds(k0, SEARCH_ROWS), :] = bits ^ ((bits >> 31) & 0x7FFFFFFF)
        return carry

    lax.fori_loop(0, n_grp, score_group, 0)

    def count(pred):
        def group(g, accs):
            g0 = pl.multiple_of(g * SEARCH_ROWS, SEARCH_ROWS)
            accs = list(accs)
            for r in range(SEARCH_ROWS // 8):
                kk = key_ref[pl.ds(g0 + 8 * r, 8), :]
                accs[r % 4] = accs[r % 4] + jnp.where(pred(kk), 1.0, 0.0)
            return tuple(accs)
        accs = lax.fori_loop(0, n_grp, group, (jnp.zeros((8, Q_BLOCK), F32),) * 4)
        return jnp.sum(accs[0] + accs[1] + accs[2] + accs[3], axis=0, keepdims=True)

    def count_ge(cand):
        cand8 = jnp.broadcast_to(cand, (8, Q_BLOCK))
        return count(lambda kk: kk >= cand8)

    base = jnp.where(count_ge(jnp.zeros((1, Q_BLOCK), jnp.int32)) >= TOPK, 0, INT_MIN).astype(jnp.int32)

    def search(i, base):
        cand = base | jnp.left_shift(jnp.int32(1), 30 - i)
        return jnp.where(count_ge(cand) >= TOPK, cand, base)

    thr = lax.fori_loop(0, 31, search, base)
    thr8 = jnp.broadcast_to(thr, (8, Q_BLOCK))
    cnt_gt = count(lambda kk: kk > thr8)
    cnt_ge = count_ge(thr)
    tie_cols = (cnt_ge > TOPK) & (thr > NEG_INF_KEY)
    has_tie = jnp.max(jnp.where(tie_cols, 1.0, 0.0)) > 0.0

    def bias_block(kb, carry):
        k0 = pl.multiple_of(kb * Q_BLOCK, Q_BLOCK)
        kk = key_ref[pl.ds(k0, Q_BLOCK), :]
        bias_ref[pl.ds(k0, Q_BLOCK), :] = jnp.where((kk >= thr) & (kk > NEG_INF_KEY), 0.0, -jnp.inf).T
        return carry

    lax.fori_loop(0, n_blk, bias_block, 0)

    @pl.when(has_tie)
    def _():
        need = TOPK - cnt_gt
        row = lax.broadcasted_iota(jnp.int32, (Q_BLOCK, Q_BLOCK), 0)
        col = lax.broadcasted_iota(jnp.int32, (Q_BLOCK, Q_BLOCK), 1)
        lower = jnp.where(col < row, 1.0, 0.0).astype(BF16)

        def tie_block(kb, seen):
            k0 = pl.multiple_of(kb * Q_BLOCK, Q_BLOCK)
            kk = key_ref[pl.ds(k0, Q_BLOCK), :]
            eq = jnp.where(kk == thr, 1.0, 0.0)
            before = jnp.dot(lower, eq.astype(BF16), preferred_element_type=F32) + seen
            keep = (kk > thr) | ((kk == thr) & (before < need))
            bias_ref[pl.ds(k0, Q_BLOCK), :] = jnp.where(keep & (kk > NEG_INF_KEY), 0.0, -jnp.inf).T
            return seen + jnp.sum(eq, axis=0, keepdims=True)

        lax.fori_loop(0, n_blk, tie_block, jnp.zeros((1, Q_BLOCK), F32))


def _dsa_attend(q_ref, kv_ref, bias_ref, o_ref, n_keys):
    q = q_ref[...]
    k = kv_ref[0:n_keys, 0:A_HEAD_DIM]
    v = kv_ref[0:n_keys, A_HEAD_DIM:2 * A_HEAD_DIM]
    bias = jnp.concatenate([bias_ref[c * Q_BLOCK:(c + 1) * Q_BLOCK, :] for c in range(n_keys // Q_BLOCK)],
                           axis=1)
    for h in range(A_HEADS):
        logits = _nt_dot(q[:, h * A_HEAD_DIM:(h + 1) * A_HEAD_DIM], k) * (A_HEAD_DIM ** -0.5) + bias
        m = jnp.max(logits, axis=-1, keepdims=True)
        p = jnp.exp(logits - m)
        l = jnp.sum(p, axis=-1, keepdims=True)
        o = jnp.dot(p.astype(BF16), v, preferred_element_type=F32) / l
        o_ref[:, h * A_HEAD_DIM:(h + 1) * A_HEAD_DIM] = o.astype(o_ref.dtype)


def _dsa_kernel(q_ref, kv_ref, iq_ref, ikw_ref, o_ref, key_ref, bias_ref):
    j = pl.program_id(1)
    _dsa_select(iq_ref, ikw_ref, key_ref, bias_ref)
    blocks_per_variant = N_QB // KV_VARIANTS
    for v in range(KV_VARIANTS):
        @pl.when(j // blocks_per_variant == v)
        def _(v=v):
            def hide(kb, carry):
                k0 = pl.multiple_of(kb * Q_BLOCK, Q_BLOCK)
                bias_ref[pl.ds(k0, Q_BLOCK), :] = jnp.full((Q_BLOCK, Q_BLOCK), -jnp.inf, F32)
                return carry
            lax.fori_loop(j + 1, blocks_per_variant * (v + 1), hide, 0)
            _dsa_attend(q_ref, kv_ref, bias_ref, o_ref, KV_STEP * (v + 1))


def _dsa_attention(za, zi, bsz):
    return pl.pallas_call(
        _dsa_kernel,
        grid=(bsz, N_QB),
        in_specs=[
            pl.BlockSpec((None, Q_BLOCK, 512), lambda b, j: (b, j, 0)),
            pl.BlockSpec((None, SEQ, 256), lambda b, j: (b, 0, 2)),
            pl.BlockSpec((None, Q_BLOCK, 512), lambda b, j: (b, j, 0)),
            pl.BlockSpec((None, SEQ, LANES), lambda b, j: (b, 0, 4)),
        ],
        out_specs=pl.BlockSpec((None, Q_BLOCK, 512), lambda b, j: (b, j, 0)),
        out_shape=jax.ShapeDtypeStruct((bsz, SEQ, 512), BF16),
        scratch_shapes=[pltpu.VMEM((SEQ, Q_BLOCK), jnp.int32),
                        pltpu.VMEM((SEQ, Q_BLOCK), F32)],
        compiler_params=_cparams("parallel", "arbitrary"),
    )(za, za, zi, zi)


def _mla_body(q_ref, kv_ref, kr_ref, o_ref, n_keys):
    j = pl.program_id(1)
    mask = _chunk_causal_mask(j, n_keys)
    kr = kr_ref[0:n_keys, 0:B_ROPE]
    scale = (B_NOPE + B_ROPE) ** -0.5
    for h in range(B_HEADS):
        qcat = jnp.concatenate(
            [q_ref[:, h * B_NOPE:(h + 1) * B_NOPE],
             q_ref[:, B_HEADS * B_NOPE + h * B_ROPE:B_HEADS * B_NOPE + (h + 1) * B_ROPE]], axis=1)
        kcat = jnp.concatenate([kv_ref[0:n_keys, h * B_NOPE:(h + 1) * B_NOPE], kr], axis=1)
        vv = kv_ref[0:n_keys, B_HEADS * B_NOPE + h * B_V:B_HEADS * B_NOPE + (h + 1) * B_V]
        s = _nt_dot(qcat, kcat) * scale
        s = jnp.where(mask, s, -jnp.inf)
        m = jnp.max(s, axis=-1, keepdims=True)
        p = jnp.exp(s - m)
        l = jnp.sum(p, axis=-1, keepdims=True)
        o = jnp.dot(p.astype(BF16), vv, preferred_element_type=F32) / l
        o_ref[:, h * B_V:(h + 1) * B_V] = o.astype(o_ref.dtype)


def _mla_kernel(q_ref, kv_ref, kr_ref, o_ref):
    j = pl.program_id(1)
    for v in range(KV_VARIANTS):
        @pl.when(j // (N_QB // KV_VARIANTS) == v)
        def _(v=v):
            _mla_body(q_ref, kv_ref, kr_ref, o_ref, KV_STEP * (v + 1))


def _mla_attention(qb, kvb, kr, bsz):
    return pl.pallas_call(
        _mla_kernel,
        grid=(bsz, N_QB),
        in_specs=[
            pl.BlockSpec((None, Q_BLOCK, 768), lambda b, j: (b, j, 0)),
            pl.BlockSpec((None, SEQ, 1024), lambda b, j: (b, 0, 0)),
            pl.BlockSpec((None, SEQ, LANES), lambda b, j: (b, 0, 0)),
        ],
        out_specs=pl.BlockSpec((None, Q_BLOCK, 512), lambda b, j: (b, j, 0)),
        out_shape=jax.ShapeDtypeStruct((bsz, SEQ, 512), BF16),
        compiler_params=_cparams("parallel", "arbitrary"),
    )(qb, kvb, kr)


def _band_bias(rel_table):
    n = 2 * REL_CLIP + 1
    period = 2 * n - 1
    heads = rel_table.shape[0]
    ext = jnp.concatenate([rel_table, jnp.broadcast_to(rel_table[:, n - 1:n], (heads, n - 1))], axis=1)
    kj = np.arange(Q_BLOCK)[:, None]
    qi = np.arange(Q_BLOCK)[None, :]
    out = []
    for d in range(C_KEY_BLOCKS):
        base = d * Q_BLOCK + REL_CLIP
        if base - (Q_BLOCK - 1) >= n - 1:
            bias = jnp.broadcast_to(rel_table[:, n - 1][:, None, None], (heads, Q_BLOCK, Q_BLOCK))
        else:
            shifted = jnp.roll(ext, -base, axis=1)
            bias = jnp.tile(shifted, (1, Q_BLOCK))[:, :Q_BLOCK * (period - 1)]
            bias = bias.reshape(heads, Q_BLOCK, period - 1)[:, :, :Q_BLOCK]
        cdiff = 2 * d + qi // CHUNK - kj // CHUNK
        valid = (cdiff >= 0) & (cdiff <= C_LEFT_CHUNKS)
        out.append(jnp.where(valid[None], bias.astype(F32), -jnp.inf))
    return jnp.swapaxes(jnp.stack(out, axis=1), 2, 3)


def _band_kernel(q_ref, kv_ref, bias_ref, o_ref):
    j = pl.program_id(1)
    scale = C_HEAD_DIM ** -0.5
    for h in range(C_HEADS):
        q = q_ref[:, h * C_HEAD_DIM:(h + 1) * C_HEAD_DIM]
        ss, vs = [], []
        for d in range(C_KEY_BLOCKS):
            kb = j - d
            row0 = pl.multiple_of(jnp.maximum(kb, 0) * Q_BLOCK, Q_BLOCK)
            k = kv_ref[pl.ds(row0, Q_BLOCK), h * C_HEAD_DIM:(h + 1) * C_HEAD_DIM]
            vs.append(kv_ref[pl.ds(row0, Q_BLOCK), (C_HEADS + h) * C_HEAD_DIM:(C_HEADS + h + 1) * C_HEAD_DIM])
            s = _nt_dot(q, k) * scale + bias_ref[h, d]
            ss.append(jnp.where(kb >= 0, s, -jnp.inf))
        s_all = jnp.concatenate(ss, axis=1)
        v_all = jnp.concatenate(vs, axis=0)
        m = jnp.max(s_all, axis=-1, keepdims=True)
        p = jnp.exp(s_all - m)
        l = jnp.sum(p, axis=-1, keepdims=True)
        o = jnp.dot(p.astype(BF16), v_all, preferred_element_type=F32) / l
        o_ref[:, h * C_HEAD_DIM:(h + 1) * C_HEAD_DIM] = o.astype(o_ref.dtype)


def _band_attention(zc, bias, bsz):
    return pl.pallas_call(
        _band_kernel,
        grid=(bsz, N_QB),
        in_specs=[
            pl.BlockSpec((None, Q_BLOCK, 512), lambda b, j: (b, j, 2)),
            pl.BlockSpec((None, SEQ, 1024), lambda b, j: (b, 0, 0)),
            pl.BlockSpec((C_HEADS, C_KEY_BLOCKS, Q_BLOCK, Q_BLOCK), lambda b, j: (0, 0, 0, 0)),
        ],
        out_specs=pl.BlockSpec((None, Q_BLOCK, 512), lambda b, j: (b, j, 0)),
        out_shape=jax.ShapeDtypeStruct((bsz, SEQ, 512), BF16),
        compiler_params=_cparams("parallel", "arbitrary"),
    )(zc, zc, bias)


def _merge_kernel(x_ref, g_ref, wgl_ref, oa_ref, ob_ref, oc_ref, wpa_ref, wpb_ref, wpc_ref, wout_ref, o_ref):
    x = x_ref[...]
    ms = jnp.mean(x * x, axis=-1, keepdims=True)
    h = (x * lax.rsqrt(ms + EPS) * g_ref[...]).astype(BF16)
    mix = jnp.zeros(x.shape, F32)
    for i, (o_in, wp) in enumerate(((oa_ref, wpa_ref), (ob_ref, wpb_ref), (oc_ref, wpc_ref))):
        gl = jnp.dot(h, wgl_ref[:, i * D_MODEL:(i + 1) * D_MODEL], preferred_element_type=F32)
        gate = jax.nn.sigmoid(gl)
        mix = mix + gate * jnp.dot(o_in[...], wp[...], preferred_element_type=F32)
    o_ref[...] = x + jnp.dot(mix.astype(BF16), wout_ref[...], preferred_element_type=F32)


def _merge(x, g, wgl, oa, ob, oc, wpa, wpb, wpc, wout, *, tm):
    t, d = x.shape
    row = lambda i: (i, 0)
    fixed = lambda i: (0, 0)
    return pl.pallas_call(
        _merge_kernel,
        grid=(t // tm,),
        in_specs=[
            pl.BlockSpec((tm, d), row),
            pl.BlockSpec((1, d), fixed),
            pl.BlockSpec((d, 3 * d), fixed),
            pl.BlockSpec((tm, 512), row),
            pl.BlockSpec((tm, 512), row),
            pl.BlockSpec((tm, 512), row),
            pl.BlockSpec((512, d), fixed),
            pl.BlockSpec((512, d), fixed),
            pl.BlockSpec((512, d), fixed),
            pl.BlockSpec((d, d), fixed),
        ],
        out_specs=pl.BlockSpec((tm, d), row),
        out_shape=jax.ShapeDtypeStruct((t, d), F32),
        compiler_params=_cparams("parallel"),
    )(x, g.reshape(1, d), wgl, oa, ob, oc, wpa, wpb, wpc, wout)


def _first_argmax(vals, lane):
    m = jnp.max(vals, axis=-1, keepdims=True)
    idx = jnp.min(jnp.where(vals == m, lane, LANES), axis=-1, keepdims=True)
    return m, idx


def _pack_bf16_pairs(h):
    n = h.shape[1] // 2
    bits = lax.bitcast_convert_type(h.astype(jnp.bfloat16).astype(F32), jnp.int32)
    return lax.shift_right_logical(bits[:, :n], 16) | bits[:, n:]


def _unpack_bf16_pairs(w):
    lo = lax.bitcast_convert_type(w << 16, F32).astype(BF16)
    hi = lax.bitcast_convert_type(w & HIGH_HALF, F32).astype(BF16)
    return lo, hi


def _router_kernel(x_ref, g_ref, w_ref, b_ref, hp_ref, meta_ref, seg_ref):
    x = x_ref[...]
    ms = jnp.mean(x * x, axis=-1, keepdims=True)
    h = x * lax.rsqrt(ms + EPS) * g_ref[...]
    hp_ref[...] = _pack_bf16_pairs(h)
    logits = jnp.dot(h, w_ref[...], preferred_element_type=F32, precision=lax.Precision.HIGHEST) + b_ref[...]
    lane = lax.broadcasted_iota(jnp.int32, logits.shape, 1)
    is_grp = (lane >= N_EXPERTS) & (lane < N_EXPERTS + N_GROUPS)
    gl = jnp.where(is_grp, logits, -jnp.inf)
    gmax, gidx = _first_argmax(gl, lane)
    pg = 1.0 / jnp.sum(jnp.exp(gl - gmax), axis=-1, keepdims=True)
    gsel = gidx - N_EXPERTS
    in_grp = (lane >> 3) == gsel
    el = jnp.where(in_grp, logits, -jnp.inf)
    m1, i1 = _first_argmax(el, lane)
    z = jnp.sum(jnp.exp(el - m1), axis=-1, keepdims=True)
    el2 = jnp.where(lane == i1, -jnp.inf, el)
    m2, i2 = _first_argmax(el2, lane)
    pe1 = 1.0 / z
    pe2 = jnp.exp(m2 - m1) / z
    den = pe1 + pe2
    w1 = pg * pe1 / den
    w2 = pg * pe2 / den

    sel1 = lane == i1
    sel2 = lane == i2
    onehot = jnp.where(sel1 | sel2, 1.0, 0.0)
    a = lax.broadcasted_iota(jnp.int32, (LANES, LANES), 0)
    b = lax.broadcasted_iota(jnp.int32, (LANES, LANES), 1)
    lower = jnp.where(b < a, 1.0, 0.0).astype(BF16)
    carry = jnp.zeros((1, LANES), F32)
    ranks = []
    for c in range(MOE_TILE // LANES):
        blk = onehot[c * LANES:(c + 1) * LANES]
        ranks.append(jnp.dot(lower, blk.astype(BF16), preferred_element_type=F32) + carry)
        carry = carry + jnp.sum(blk, axis=0, keepdims=True)
    rank = jnp.concatenate(ranks, axis=0)
    cnt = jnp.broadcast_to(carry, (8, LANES))
    seg = jnp.floor((cnt + 7.0) * 0.125) * 8.0
    lane8 = lax.broadcasted_iota(jnp.int32, (8, LANES), 1)
    scan = seg
    for k in (1, 2, 4, 8, 16, 32, 64):
        scan = scan + jnp.where(lane8 >= k, pltpu.roll(scan, k, 1), 0.0)
    off = scan - seg
    where_row = rank + off[0:1]
    pos1 = jnp.sum(jnp.where(sel1, where_row, 0.0), axis=-1, keepdims=True)
    pos2 = jnp.sum(jnp.where(sel2, where_row, 0.0), axis=-1, keepdims=True)
    meta_ref[...] = (jnp.where(lane == 0, pos1, 0.0) + jnp.where(lane == 1, pos2, 0.0)
                     + jnp.where(lane == 2, w1, 0.0) + jnp.where(lane == 3, w2, 0.0))
    row8 = lax.broadcasted_iota(jnp.int32, (8, LANES), 0)
    seg_ref[...] = jnp.where(row8 == 0, cnt, jnp.where(row8 == 1, off, 0.0)).astype(jnp.int32)


def _router(x, g, w, b):
    t, d = x.shape
    nt = t // MOE_TILE
    return pl.pallas_call(
        _router_kernel,
        grid=(nt,),
        in_specs=[
            pl.BlockSpec((MOE_TILE, d), lambda i: (i, 0)),
            pl.BlockSpec((1, d), lambda i: (0, 0)),
            pl.BlockSpec((d, LANES), lambda i: (0, 0)),
            pl.BlockSpec((1, LANES), lambda i: (0, 0)),
        ],
        out_specs=[
            pl.BlockSpec((MOE_TILE, d // 2), lambda i: (i, 0)),
            pl.BlockSpec((MOE_TILE, LANES), lambda i: (i, 0)),
            pl.BlockSpec((None, 8, LANES), lambda i: (i, 0, 0)),
        ],
        out_shape=[
            jax.ShapeDtypeStruct((t, d // 2), jnp.int32),
            jax.ShapeDtypeStruct((t, LANES), F32),
            jax.ShapeDtypeStruct((nt, 8, LANES), jnp.int32),
        ],
        compiler_params=_cparams("parallel"),
    )(x, g.reshape(1, d), w, b)


def _scatter_kernel(pos_ref, hp_ref, xs_ref):
    xs_ref[...] = jnp.zeros_like(xs_ref)

    def body(t, carry):
        row = hp_ref[pl.ds(t, 1), :]
        xs_ref[pl.ds(pos_ref[0, t], 1), :] = row
        xs_ref[pl.ds(pos_ref[0, MOE_TILE + t], 1), :] = row
        return carry

    lax.fori_loop(0, MOE_TILE, body, 0, unroll=8)


def _scatter_rows(pos, hp):
    nt = pos.shape[0]
    return pl.pallas_call(
        _scatter_kernel,
        grid=(nt,),
        in_specs=[
            pl.BlockSpec((None, 1, 2 * MOE_TILE), lambda i: (i, 0, 0), memory_space=pltpu.SMEM),
            pl.BlockSpec((MOE_TILE, hp.shape[1]), lambda i: (i, 0)),
        ],
        out_specs=pl.BlockSpec((None, XS_ROWS, hp.shape[1]), lambda i: (i, 0, 0)),
        out_shape=jax.ShapeDtypeStruct((nt, XS_ROWS, hp.shape[1]), jnp.int32),
        compiler_params=_cparams("parallel"),
    )(pos, hp)


def _experts_kernel(cnt_ref, off_ref, xs_ref, wg_ref, wu_ref, wd_ref, ys_ref):
    i = pl.program_id(0)
    e = pl.program_id(1)
    half = D_MODEL // 2

    @pl.when(e == 0)
    def _():
        ys_ref[2 * MOE_TILE:XS_ROWS, :] = jnp.zeros((XS_ROWS - 2 * MOE_TILE, D_MODEL), F32)

    n = cnt_ref[i, e]
    off = off_ref[i, e]

    def body(c, carry):
        start = pl.multiple_of(off + c * EXPERT_CHUNK, 8)
        lo, hi = _unpack_bf16_pairs(xs_ref[pl.ds(start, EXPERT_CHUNK), :])
        a = (jnp.dot(lo, wg_ref[0:half].astype(BF16), preferred_element_type=F32)
             + jnp.dot(hi, wg_ref[half:D_MODEL].astype(BF16), preferred_element_type=F32))
        u = (jnp.dot(lo, wu_ref[0:half].astype(BF16), preferred_element_type=F32)
             + jnp.dot(hi, wu_ref[half:D_MODEL].astype(BF16), preferred_element_type=F32))
        hh = (a * jax.nn.sigmoid(a)) * u
        ys_ref[pl.ds(start, EXPERT_CHUNK), :] = jnp.dot(hh.astype(BF16), wd_ref[...].astype(BF16),
                                                        preferred_element_type=F32)
        return carry

    lax.fori_loop(0, (n + EXPERT_CHUNK - 1) // EXPERT_CHUNK, body, 0)


def _experts(cnt, off, xs, wg, wu, wd, layer):
    nt = xs.shape[0]
    d = D_MODEL
    grid_spec = pltpu.PrefetchScalarGridSpec(
        num_scalar_prefetch=2,
        grid=(nt, N_EXPERTS),
        in_specs=[
            pl.BlockSpec((None, XS_ROWS, d // 2), lambda i, e, c, o: (i, 0, 0), pipeline_mode=pl.Buffered(1)),
            pl.BlockSpec((None, None, d, D_EXPERT), lambda i, e, c, o: (layer, e, 0, 0)),
            pl.BlockSpec((None, None, d, D_EXPERT), lambda i, e, c, o: (layer, e, 0, 0)),
            pl.BlockSpec((None, None, D_EXPERT, d), lambda i, e, c, o: (layer, e, 0, 0)),
        ],
        out_specs=pl.BlockSpec((None, XS_ROWS, d), lambda i, e, c, o: (i, 0, 0)),
    )
    return pl.pallas_call(
        _experts_kernel,
        grid_spec=grid_spec,
        out_shape=jax.ShapeDtypeStruct((nt, XS_ROWS, d), F32),
        compiler_params=_cparams("parallel", "arbitrary"),
    )(cnt, off, xs, wg, wu, wd)


def _combine_kernel(pos_ref, wt_ref, x_ref, ys_ref, g_ref, o_ref, *, final_norm):
    s = pl.program_id(1)

    def body(tl, carry):
        t = s * COMBINE_ROWS + tl
        y = (ys_ref[pl.ds(pos_ref[0, t], 1), :] * wt_ref[0, t]
             + ys_ref[pl.ds(pos_ref[0, MOE_TILE + t], 1), :] * wt_ref[0, MOE_TILE + t])
        o_ref[pl.ds(tl, 1), :] = x_ref[pl.ds(tl, 1), :] + y
        return carry

    lax.fori_loop(0, COMBINE_ROWS, body, 0, unroll=8)
    if final_norm:
        o_ref[...] = _rms(o_ref[...], g_ref[...])


def _combine(pos, wt, x, ys, final_g):
    t, d = x.shape
    nt = pos.shape[0]
    sub = MOE_TILE // COMBINE_ROWS
    g = jnp.ones((d,), F32) if final_g is None else final_g
    return pl.pallas_call(
        functools.partial(_combine_kernel, final_norm=final_g is not None),
        grid=(nt, sub),
        in_specs=[
            pl.BlockSpec((None, 1, 2 * MOE_TILE), lambda i, s: (i, 0, 0), memory_space=pltpu.SMEM),
            pl.BlockSpec((None, 1, 2 * MOE_TILE), lambda i, s: (i, 0, 0), memory_space=pltpu.SMEM),
            pl.BlockSpec((COMBINE_ROWS, d), lambda i, s: (i * sub + s, 0)),
            pl.BlockSpec((None, XS_ROWS, d), lambda i, s: (i, 0, 0)),
            pl.BlockSpec((1, d), lambda i, s: (0, 0)),
        ],
        out_specs=pl.BlockSpec((COMBINE_ROWS, d), lambda i, s: (i * sub + s, 0)),
        out_shape=jax.ShapeDtypeStruct((t, d), F32),
        compiler_params=_cparams("parallel", "arbitrary"),
    )(pos, wt, x, ys, g.reshape(1, d))


def _moe(x, g, w_r, b_r, wg, wu, wd, layer, final_g=None):
    t = x.shape[0]
    nt = t // MOE_TILE
    hp, meta, seg = _router(x, g, w_r, b_r)
    pair_major = lambda m: m.reshape(nt, MOE_TILE, 2).transpose(0, 2, 1).reshape(nt, 1, 2 * MOE_TILE)
    pos = pair_major(meta[:, 0:2].astype(jnp.int32))
    wt = pair_major(meta[:, 2:4])
    xs = _scatter_rows(pos, hp)
    ys = _experts(seg[:, 0, :N_EXPERTS], seg[:, 1, :N_EXPERTS], xs, wg, wu, wd, layer)
    return _combine(pos, wt, x, ys, final_g)


def _pad_cols(w, n):
    return jnp.pad(w, ((0, 0), (0, n - w.shape[1])))


def kernel(x, attn_norm_g, w_in, b_q_norm_g, b_w_uq, b_kv_norm_g, b_w_ukv, c_rel_bias, w_proj_a, w_proj_b, w_proj_c, w_out, ffn_norm_g, w_group, b_group, w_router, b_router, w_gate, w_up, w_down, final_norm_g):
    bsz, seq, d = x.shape
    assert (seq, d) == (SEQ, D_MODEL)
    t = bsz * seq
    depth = w_in.shape[0]
    tm = 512

    tab_a = _rope_table(A_ROT, A_HEAD_DIM)
    tab_i = _rope_table(IDX_ROT, IDX_DIM)
    tab_i_half = _rope_table(IDX_ROT, IDX_DIM, active_lanes=IDX_DIM)
    tab_b = _rope_table(B_ROPE, B_ROPE)
    tab_b_half = _rope_table(B_ROPE, B_ROPE, active_lanes=B_ROPE)

    tabs = (tab_a, tab_i, tab_i_half, tab_b, tab_b_half)
    xf = x.reshape(t, d)
    for l in range(depth):
        w = w_in[l]
        w_a = w[:, 0:768].astype(BF16)
        w_i = _pad_cols(w[:, 768:1352], 640).astype(BF16)
        w_b = _pad_cols(w[:, 1352:1800], 512).astype(BF16)
        w_c = jnp.concatenate([w[:, 2312:3336], w[:, 1800:2312]], axis=1).astype(BF16)
        w_g = w[:, 3336:6408].astype(BF16)
        g_attn = attn_norm_g[l]
        w_uq = b_w_uq[l].reshape(B_Q_RANK, B_HEADS, B_NOPE + B_ROPE)
        w_uq = jnp.concatenate([w_uq[:, :, :B_NOPE].reshape(B_Q_RANK, -1),
                                w_uq[:, :, B_NOPE:].reshape(B_Q_RANK, -1)], axis=1).astype(BF16)
        w_ukv = b_w_ukv[l].reshape(B_KV_RANK, B_HEADS, B_NOPE + B_V)
        w_ukv = jnp.concatenate([w_ukv[:, :, :B_NOPE].reshape(B_KV_RANK, -1),
                                 w_ukv[:, :, B_NOPE:].reshape(B_KV_RANK, -1)], axis=1).astype(BF16)

        za, zi, qb, kvb, kr, zc = _in_proj(xf, g_attn, w_a, w_i, w_b, w_c, b_q_norm_g[l], w_uq,
                                           b_kv_norm_g[l], w_ukv, tabs, tm=tm)
        per_seq = lambda z: z.reshape(bsz, seq, -1)
        o_a = _dsa_attention(per_seq(za), per_seq(zi), bsz)
        o_b = _mla_attention(per_seq(qb), per_seq(kvb), per_seq(kr), bsz)
        o_c = _band_attention(per_seq(zc), _band_bias(c_rel_bias[l]), bsz)

        xf = _merge(xf, g_attn, w_g, o_a.reshape(t, -1), o_b.reshape(t, -1), o_c.reshape(t, -1),
                    w_proj_a[l].astype(BF16), w_proj_b[l].astype(BF16), w_proj_c[l].astype(BF16),
                    w_out[l].astype(BF16), tm=tm)

        w_r = _pad_cols(jnp.concatenate([w_router[l], w_group[l]], axis=1), LANES)
        b_r = _pad_cols(jnp.concatenate([b_router[l], b_group[l]])[None, :], LANES)
        xf = _moe(xf, ffn_norm_g[l], w_r, b_r, w_gate, w_up, w_down, l,
                  final_g=final_norm_g if l == depth - 1 else None)

    return xf.reshape(bsz, seq, d)
```

```python
import functools

import numpy as np
import jax
import jax.numpy as jnp
from jax import lax
from jax.experimental import pallas as pl
from jax.experimental.pallas import tpu as pltpu

F32 = jnp.float32
BF16 = jnp.bfloat16

LANES = 128
D_MODEL = 1024
SEQ = 2048
CHUNK = 64
Q_BLOCK = 128
ROPE_THETA = 500000.0
EPS = 1e-6

A_HEADS = 4
A_HEAD_DIM = 128
A_ROT = 32
IDX_HEADS = 8
IDX_DIM = 64
IDX_ROT = 16
TOPK = 256
B_HEADS = 4
B_NOPE = 128
B_ROPE = 64
B_V = 128
B_Q_RANK = 256
B_KV_RANK = 128
C_HEADS = 4
C_HEAD_DIM = 128
C_LEFT_CHUNKS = 8
REL_CLIP = 128
N_GROUPS = 4
EXPERTS_PER_GROUP = 8
N_EXPERTS = 32
D_EXPERT = 256

C_KEY_BLOCKS = C_LEFT_CHUNKS * CHUNK // Q_BLOCK + 1
N_QB = SEQ // Q_BLOCK
KV_VARIANTS = 4
KV_STEP = SEQ // KV_VARIANTS

VMEM_LIMIT = 56 * 1024 * 1024

MOE_TILE = 2048
EXPERT_CHUNK = 256
XS_ROWS = 2 * MOE_TILE + 2 * EXPERT_CHUNK
assert XS_ROWS >= 2 * MOE_TILE + N_EXPERTS * 7 + EXPERT_CHUNK - 1
COMBINE_ROWS = 512
HIGH_HALF = -65536

INT_MIN = -2 ** 31
NEG_INF_KEY = int(np.array(0x807FFFFF, np.uint32).view(np.int32))

NT_DIMS = (((1,), (1,)), ((), ()))


def _nt_dot(a, b):
    return lax.dot_general(a, b, NT_DIMS, preferred_element_type=F32)


def _cparams(*sem):
    return pltpu.CompilerParams(dimension_semantics=sem, vmem_limit_bytes=VMEM_LIMIT)


def _rope_table(rot, period, active_lanes=LANES):
    half = rot // 2
    lane = np.arange(LANES)
    p = lane % period
    first = (p < half) & (lane < active_lanes)
    second = (p >= half) & (p < rot) & (lane < active_lanes)
    idx = np.where(first, p, np.where(second, p - half, 0))
    pos = jnp.arange(SEQ, dtype=F32)
    inv = ROPE_THETA ** (-jnp.arange(0, rot, 2, dtype=F32) / rot)
    ang = pos[:, None] * inv[idx][None, :]
    cos, sin = jnp.cos(ang), jnp.sin(ang)
    c = jnp.where(first | second, cos, 1.0)
    s_prev = jnp.where(second, sin, 0.0)
    s_next = jnp.where(first, -sin, 0.0)
    return jnp.stack([c, s_prev, s_next]).astype(F32)


def _rms(x, g):
    ms = jnp.mean(x * x, axis=-1, keepdims=True)
    return x * lax.rsqrt(ms + EPS) * g


def _rope_tiles(z, tile_tab, tabs, halves):
    out = []
    for c, t in enumerate(tile_tab):
        zt = z[:, c * LANES:(c + 1) * LANES]
        if t >= 0:
            tab, half = tabs[t], halves[t]
            zt = (zt * tab[0] + pltpu.roll(zt, half, 1) * tab[1]
                  + pltpu.roll(zt, LANES - half, 1) * tab[2])
        out.append(zt)
    return out


def _store_tiles(o_ref, tiles):
    for c, zt in enumerate(tiles):
        o_ref[:, c * LANES:(c + 1) * LANES] = zt.astype(o_ref.dtype)


ROPE_HALVES = (A_ROT // 2, IDX_ROT // 2, IDX_ROT // 2, B_ROPE // 2, B_ROPE // 2)


def _in_proj_kernel(x_ref, g_ref, wa_ref, wi_ref, wb_ref, wc_ref, gq_ref, wuq_ref, gkv_ref, wukv_ref,
                    ta_ref, ti_ref, tih_ref, tb_ref, tbh_ref,
                    za_ref, zi_ref, qb_ref, kvb_ref, kr_ref, zc_ref):
    tabs = (ta_ref, ti_ref, tih_ref, tb_ref, tbh_ref)
    rope = functools.partial(_rope_tiles, tabs=tabs, halves=ROPE_HALVES)
    h = _rms(x_ref[...], g_ref[...]).astype(BF16)
    dot = functools.partial(jnp.dot, preferred_element_type=F32)
    _store_tiles(za_ref, rope(dot(h, wa_ref[...]), (0, 0, 0, 0, 0, -1)))
    _store_tiles(zi_ref, rope(dot(h, wi_ref[...]), (1, 1, 1, 1, 2)))
    _store_tiles(zc_ref, rope(dot(h, wc_ref[...]), (-1,) * (3 * C_HEADS)))
    zb = dot(h, wb_ref[...])
    _store_tiles(kr_ref, rope(zb[:, B_Q_RANK + B_KV_RANK:], (4,)))
    cq = _rms(zb[:, :B_Q_RANK], gq_ref[...]).astype(BF16)
    _store_tiles(qb_ref, rope(dot(cq, wuq_ref[...]), (-1, -1, -1, -1, 3, 3)))
    ckv = _rms(zb[:, B_Q_RANK:B_Q_RANK + B_KV_RANK], gkv_ref[...]).astype(BF16)
    _store_tiles(kvb_ref, rope(dot(ckv, wukv_ref[...]), (-1,) * (2 * B_HEADS)))


def _in_proj(x, g, w_a, w_i, w_b, w_c, g_q, w_uq, g_kv, w_ukv, tabs, *, tm):
    t, d = x.shape
    seq_tiles = SEQ // tm
    fixed = lambda i: (0, 0)
    row = lambda i: (i, 0)
    weights = (w_a, w_i, w_b, w_c)
    outs = ((768, BF16), (640, F32), (768, BF16), (1024, BF16), (LANES, BF16), (1536, BF16))
    return pl.pallas_call(
        _in_proj_kernel,
        grid=(t // tm,),
        in_specs=[pl.BlockSpec((tm, d), row), pl.BlockSpec((1, d), fixed)]
        + [pl.BlockSpec(w.shape, fixed) for w in weights]
        + [pl.BlockSpec((1, B_Q_RANK), fixed), pl.BlockSpec(w_uq.shape, fixed),
           pl.BlockSpec((1, B_KV_RANK), fixed), pl.BlockSpec(w_ukv.shape, fixed)]
        + [pl.BlockSpec((3, tm, LANES), lambda i: (0, i % seq_tiles, 0)) for _ in tabs],
        out_specs=[pl.BlockSpec((tm, n), row) for n, _ in outs],
        out_shape=[jax.ShapeDtypeStruct((t, n), dt) for n, dt in outs],
        compiler_params=_cparams("parallel"),
    )(x, g.reshape(1, d), *weights, g_q.reshape(1, -1), w_uq, g_kv.reshape(1, -1), w_ukv, *tabs)


def _chunk_causal_mask(j, n_keys):
    qpos = j * Q_BLOCK + lax.broadcasted_iota(jnp.int32, (Q_BLOCK, n_keys), 0)
    kpos = lax.broadcasted_iota(jnp.int32, (Q_BLOCK, n_keys), 1)
    return (kpos >> 6) <= (qpos >> 6)


SEARCH_ROWS = 256
HALF_RANGE = 1 << 15
PACKS_PER_GROUP = SEARCH_ROWS // 16


def _dsa_select(iq_ref, ikw_ref, key_ref, hi_ref, lo_ref, bias_ref):
    j = pl.program_id(1)
    n_blk = j + 1
    n_grp = (j + 2) // 2
    row0 = pl.multiple_of(j * Q_BLOCK, Q_BLOCK)
    sub = lax.broadcasted_iota(jnp.int32, (SEARCH_ROWS, Q_BLOCK), 0)
    lane = lax.broadcasted_iota(jnp.int32, (SEARCH_ROWS, Q_BLOCK), 1)
    q_chunk = (row0 + lane) >> 6

    iq = iq_ref[...].astype(BF16)
    iq_stack = jnp.concatenate([iq[:, h * IDX_DIM:(h + 1) * IDX_DIM] for h in range(IDX_HEADS)], axis=0)
    iw_t = ikw_ref[pl.ds(row0, Q_BLOCK), :].T * (IDX_HEADS ** -0.5)

    def score_group(g, carry):
        k0 = pl.multiple_of(g * SEARCH_ROWS, SEARCH_ROWS)
        ik = ikw_ref[pl.ds(k0, SEARCH_ROWS), 0:IDX_DIM].astype(BF16)
        score = jnp.zeros((SEARCH_ROWS, Q_BLOCK), F32)
        for hp in range(IDX_HEADS // 2):
            r = _nt_dot(ik, iq_stack[hp * 2 * Q_BLOCK:(hp + 1) * 2 * Q_BLOCK])
            for u in range(2):
                h = 2 * hp + u
                rel = jnp.maximum(r[:, u * Q_BLOCK:(u + 1) * Q_BLOCK] * (IDX_DIM ** -0.5), 0.0)
                score = score + rel * iw_t[IDX_DIM + h:IDX_DIM + h + 1, :]
        allowed = ((k0 + sub) >> 6) <= q_chunk
        score = jnp.where(score == 0.0, 0.0, score)
        score = jnp.where(allowed, score, -jnp.inf)
        bits = lax.bitcast_convert_type(score, jnp.int32)
        key = bits ^ ((bits >> 31) & 0x7FFFFFFF)
        key_ref[pl.ds(k0, SEARCH_ROWS), :] = key
        hi = (key >> 16).astype(jnp.int16)
        lo = ((key & 0xFFFF) - HALF_RANGE).astype(jnp.int16)
        for r in range(PACKS_PER_GROUP):
            hi_ref[g * PACKS_PER_GROUP + r] = hi[16 * r:16 * (r + 1)]
            lo_ref[g * PACKS_PER_GROUP + r] = lo[16 * r:16 * (r + 1)]
        return carry

    lax.fori_loop(0, n_grp, score_group, 0)

    def count(pred):
        def group(g, accs):
            g0 = pl.multiple_of(g * SEARCH_ROWS, SEARCH_ROWS)
            accs = list(accs)
            for r in range(SEARCH_ROWS // 8):
                kk = key_ref[pl.ds(g0 + 8 * r, 8), :]
                accs[r % 4] = accs[r % 4] + jnp.where(pred(kk), 1.0, 0.0)
            return tuple(accs)
        accs = lax.fori_loop(0, n_grp, group, (jnp.zeros((8, Q_BLOCK), F32),) * 4)
        return jnp.sum(accs[0] + accs[1] + accs[2] + accs[3], axis=0, keepdims=True)

    one16 = jnp.ones((16, Q_BLOCK), jnp.int16)
    zero16 = jnp.zeros((16, Q_BLOCK), jnp.int16)

    def count16_ge(ref, cand):
        cand16 = jnp.broadcast_to(cand, (16, Q_BLOCK)).astype(jnp.int16)

        def group(g, accs):
            accs = list(accs)
            for r in range(PACKS_PER_GROUP):
                kk = ref[g * PACKS_PER_GROUP + r]
                accs[r % 4] = accs[r % 4] + jnp.where(kk >= cand16, one16, zero16)
            return tuple(accs)
        accs = lax.fori_loop(0, n_grp, group, (zero16,) * 4)
        total = (accs[0] + accs[1]) + (accs[2] + accs[3])
        return jnp.sum(total.astype(jnp.int32), axis=0, keepdims=True)

    def search16(ref):
        def rnd(i, base):
            cand = base + jnp.left_shift(jnp.int32(1), 15 - i)
            return jnp.where(count16_ge(ref, cand) >= TOPK, cand, base)
        return lax.fori_loop(0, 16, rnd, jnp.full((1, Q_BLOCK), -HALF_RANGE, jnp.int32))

    top = search16(hi_ref)
    top16 = jnp.broadcast_to(top, (16, Q_BLOCK)).astype(jnp.int16)

    def restrict(g, carry):
        for r in range(PACKS_PER_GROUP):
            t = g * PACKS_PER_GROUP + r
            hi = hi_ref[t]
            lo_ref[t] = jnp.where(hi > top16, jnp.int16(HALF_RANGE - 1),
                                  jnp.where(hi == top16, lo_ref[t], jnp.int16(-HALF_RANGE)))
        return carry

    lax.fori_loop(0, n_grp, restrict, 0)
    thr = (top << 16) | (search16(lo_ref) + HALF_RANGE)
    thr8 = jnp.broadcast_to(thr, (8, Q_BLOCK))
    cnt_gt = count(lambda kk: kk > thr8)
    cnt_ge = count(lambda kk: kk >= thr8)
    tie_cols = (cnt_ge > TOPK) & (thr > NEG_INF_KEY)
    has_tie = jnp.max(jnp.where(tie_cols, 1.0, 0.0)) > 0.0

    def bias_block(kb, carry):
        k0 = pl.multiple_of(kb * Q_BLOCK, Q_BLOCK)
        kk = key_ref[pl.ds(k0, Q_BLOCK), :]
        bias_ref[pl.ds(k0, Q_BLOCK), :] = jnp.where((kk >= thr) & (kk > NEG_INF_KEY), 0.0, -jnp.inf).T
        return carry

    lax.fori_loop(0, n_blk, bias_block, 0)

    @pl.when(has_tie)
    def _():
        need = TOPK - cnt_gt
        row = lax.broadcasted_iota(jnp.int32, (Q_BLOCK, Q_BLOCK), 0)
        col = lax.broadcasted_iota(jnp.int32, (Q_BLOCK, Q_BLOCK), 1)
        lower = jnp.where(col < row, 1.0, 0.0).astype(BF16)

        def tie_block(kb, seen):
            k0 = pl.multiple_of(kb * Q_BLOCK, Q_BLOCK)
            kk = key_ref[pl.ds(k0, Q_BLOCK), :]
            eq = jnp.where(kk == thr, 1.0, 0.0)
            before = jnp.dot(lower, eq.astype(BF16), preferred_element_type=F32) + seen
            keep = (kk > thr) | ((kk == thr) & (before < need))
            bias_ref[pl.ds(k0, Q_BLOCK), :] = jnp.where(keep & (kk > NEG_INF_KEY), 0.0, -jnp.inf).T
            return seen + jnp.sum(eq, axis=0, keepdims=True)

        lax.fori_loop(0, n_blk, tie_block, jnp.zeros((1, Q_BLOCK), F32))


def _dsa_attend(q_ref, kv_ref, bias_ref, o_ref, n_keys):
    q = q_ref[...]
    k = kv_ref[0:n_keys, 0:A_HEAD_DIM]
    v = kv_ref[0:n_keys, A_HEAD_DIM:2 * A_HEAD_DIM]
    bias = jnp.concatenate([bias_ref[c * Q_BLOCK:(c + 1) * Q_BLOCK, :] for c in range(n_keys // Q_BLOCK)],
                           axis=1)
    for h in range(A_HEADS):
        logits = _nt_dot(q[:, h * A_HEAD_DIM:(h + 1) * A_HEAD_DIM], k) * (A_HEAD_DIM ** -0.5) + bias
        m = jnp.max(logits, axis=-1, keepdims=True)
        p = jnp.exp(logits - m)
        l = jnp.sum(p, axis=-1, keepdims=True)
        o = jnp.dot(p.astype(BF16), v, preferred_element_type=F32) / l
        o_ref[:, h * A_HEAD_DIM:(h + 1) * A_HEAD_DIM] = o.astype(o_ref.dtype)


def _dsa_kernel(q_ref, kv_ref, iq_ref, ikw_ref, o_ref, key_ref, hi_ref, lo_ref, bias_ref):
    j = pl.program_id(1)
    _dsa_select(iq_ref, ikw_ref, key_ref, hi_ref, lo_ref, bias_ref)
    blocks_per_variant = N_QB // KV_VARIANTS
    for v in range(KV_VARIANTS):
        @pl.when(j // blocks_per_variant == v)
        def _(v=v):
            def hide(kb, carry):
                k0 = pl.multiple_of(kb * Q_BLOCK, Q_BLOCK)
                bias_ref[pl.ds(k0, Q_BLOCK), :] = jnp.full((Q_BLOCK, Q_BLOCK), -jnp.inf, F32)
                return carry
            lax.fori_loop(j + 1, blocks_per_variant * (v + 1), hide, 0)
            _dsa_attend(q_ref, kv_ref, bias_ref, o_ref, KV_STEP * (v + 1))


def _dsa_attention(za, zi, bsz):
    return pl.pallas_call(
        _dsa_kernel,
        grid=(bsz, N_QB),
        in_specs=[
            pl.BlockSpec((None, Q_BLOCK, 512), lambda b, j: (b, j, 0)),
            pl.BlockSpec((None, SEQ, 256), lambda b, j: (b, 0, 2)),
            pl.BlockSpec((None, Q_BLOCK, 512), lambda b, j: (b, j, 0)),
            pl.BlockSpec((None, SEQ, LANES), lambda b, j: (b, 0, 4)),
        ],
        out_specs=pl.BlockSpec((None, Q_BLOCK, 512), lambda b, j: (b, j, 0)),
        out_shape=jax.ShapeDtypeStruct((bsz, SEQ, 512), BF16),
        scratch_shapes=[pltpu.VMEM((SEQ, Q_BLOCK), jnp.int32),
                        pltpu.VMEM((SEQ // 16, 16, Q_BLOCK), jnp.int16),
                        pltpu.VMEM((SEQ // 16, 16, Q_BLOCK), jnp.int16),
                        pltpu.VMEM((SEQ, Q_BLOCK), F32)],
        compiler_params=_cparams("parallel", "arbitrary"),
    )(za, za, zi, zi)


def _mla_body(q_ref, kv_ref, kr_ref, o_ref, n_keys):
    j = pl.program_id(1)
    mask = _chunk_causal_mask(j, n_keys)
    kr = kr_ref[0:n_keys, 0:B_ROPE]
    scale = (B_NOPE + B_ROPE) ** -0.5
    for h in range(B_HEADS):
        qcat = jnp.concatenate(
            [q_ref[:, h * B_NOPE:(h + 1) * B_NOPE],
             q_ref[:, B_HEADS * B_NOPE + h * B_ROPE:B_HEADS * B_NOPE + (h + 1) * B_ROPE]], axis=1)
        kcat = jnp.concatenate([kv_ref[0:n_keys, h * B_NOPE:(h + 1) * B_NOPE], kr], axis=1)
        vv = kv_ref[0:n_keys, B_HEADS * B_NOPE + h * B_V:B_HEADS * B_NOPE + (h + 1) * B_V]
        s = _nt_dot(qcat, kcat) * scale
        s = jnp.where(mask, s, -jnp.inf)
        m = jnp.max(s, axis=-1, keepdims=True)
        p = jnp.exp(s - m)
        l = jnp.sum(p, axis=-1, keepdims=True)
        o = jnp.dot(p.astype(BF16), vv, preferred_element_type=F32) / l
        o_ref[:, h * B_V:(h + 1) * B_V] = o.astype(o_ref.dtype)


def _mla_kernel(q_ref, kv_ref, kr_ref, o_ref):
    j = pl.program_id(1)
    for v in range(KV_VARIANTS):
        @pl.when(j // (N_QB // KV_VARIANTS) == v)
        def _(v=v):
            _mla_body(q_ref, kv_ref, kr_ref, o_ref, KV_STEP * (v + 1))


def _mla_attention(qb, kvb, kr, bsz):
    return pl.pallas_call(
        _mla_kernel,
        grid=(bsz, N_QB),
        in_specs=[
            pl.BlockSpec((None, Q_BLOCK, 768), lambda b, j: (b, j, 0)),
            pl.BlockSpec((None, SEQ, 1024), lambda b, j: (b, 0, 0)),
            pl.BlockSpec((None, SEQ, LANES), lambda b, j: (b, 0, 0)),
        ],
        out_specs=pl.BlockSpec((None, Q_BLOCK, 512), lambda b, j: (b, j, 0)),
        out_shape=jax.ShapeDtypeStruct((bsz, SEQ, 512), BF16),
        compiler_params=_cparams("parallel", "arbitrary"),
    )(qb, kvb, kr)


def _band_bias(rel_table):
    n = 2 * REL_CLIP + 1
    period = 2 * n - 1
    heads = rel_table.shape[0]
    ext = jnp.concatenate([rel_table, jnp.broadcast_to(rel_table[:, n - 1:n], (heads, n - 1))], axis=1)
    kj = np.arange(Q_BLOCK)[:, None]
    qi = np.arange(Q_BLOCK)[None, :]
    out = []
    for d in range(C_KEY_BLOCKS):
        base = d * Q_BLOCK + REL_CLIP
        if base - (Q_BLOCK - 1) >= n - 1:
            bias = jnp.broadcast_to(rel_table[:, n - 1][:, None, None], (heads, Q_BLOCK, Q_BLOCK))
        else:
            shifted = jnp.roll(ext, -base, axis=1)
            bias = jnp.tile(shifted, (1, Q_BLOCK))[:, :Q_BLOCK * (period - 1)]
            bias = bias.reshape(heads, Q_BLOCK, period - 1)[:, :, :Q_BLOCK]
        cdiff = 2 * d + qi // CHUNK - kj // CHUNK
        valid = (cdiff >= 0) & (cdiff <= C_LEFT_CHUNKS)
        out.append(jnp.where(valid[None], bias.astype(F32), -jnp.inf))
    return jnp.swapaxes(jnp.stack(out, axis=1), 2, 3)


def _band_kernel(q_ref, kv_ref, bias_ref, o_ref):
    j = pl.program_id(1)
    scale = C_HEAD_DIM ** -0.5
    for h in range(C_HEADS):
        q = q_ref[:, h * C_HEAD_DIM:(h + 1) * C_HEAD_DIM]
        ss, vs = [], []
        for d in range(C_KEY_BLOCKS):
            kb = j - d
            row0 = pl.multiple_of(jnp.maximum(kb, 0) * Q_BLOCK, Q_BLOCK)
            k = kv_ref[pl.ds(row0, Q_BLOCK), h * C_HEAD_DIM:(h + 1) * C_HEAD_DIM]
            vs.append(kv_ref[pl.ds(row0, Q_BLOCK), (C_HEADS + h) * C_HEAD_DIM:(C_HEADS + h + 1) * C_HEAD_DIM])
            s = _nt_dot(q, k) * scale + bias_ref[h, d]
            ss.append(jnp.where(kb >= 0, s, -jnp.inf))
        s_all = jnp.concatenate(ss, axis=1)
        v_all = jnp.concatenate(vs, axis=0)
        m = jnp.max(s_all, axis=-1, keepdims=True)
        p = jnp.exp(s_all - m)
        l = jnp.sum(p, axis=-1, keepdims=True)
        o = jnp.dot(p.astype(BF16), v_all, preferred_element_type=F32) / l
        o_ref[:, h * C_HEAD_DIM:(h + 1) * C_HEAD_DIM] = o.astype(o_ref.dtype)


def _band_attention(zc, bias, bsz):
    return pl.pallas_call(
        _band_kernel,
        grid=(bsz, N_QB),
        in_specs=[
            pl.BlockSpec((None, Q_BLOCK, 512), lambda b, j: (b, j, 2)),
            pl.BlockSpec((None, SEQ, 1024), lambda b, j: (b, 0, 0)),
            pl.BlockSpec((C_HEADS, C_KEY_BLOCKS, Q_BLOCK, Q_BLOCK), lambda b, j: (0, 0, 0, 0)),
        ],
        out_specs=pl.BlockSpec((None, Q_BLOCK, 512), lambda b, j: (b, j, 0)),
        out_shape=jax.ShapeDtypeStruct((bsz, SEQ, 512), BF16),
        compiler_params=_cparams("parallel", "arbitrary"),
    )(zc, zc, bias)


def _merge_kernel(x_ref, g_ref, wgl_ref, oa_ref, ob_ref, oc_ref, wpa_ref, wpb_ref, wpc_ref, wout_ref, o_ref):
    x = x_ref[...]
    ms = jnp.mean(x * x, axis=-1, keepdims=True)
    h = (x * lax.rsqrt(ms + EPS) * g_ref[...]).astype(BF16)
    mix = jnp.zeros(x.shape, F32)
    for i, (o_in, wp) in enumerate(((oa_ref, wpa_ref), (ob_ref, wpb_ref), (oc_ref, wpc_ref))):
        gl = jnp.dot(h, wgl_ref[:, i * D_MODEL:(i + 1) * D_MODEL], preferred_element_type=F32)
        gate = jax.nn.sigmoid(gl)
        mix = mix + gate * jnp.dot(o_in[...], wp[...], preferred_element_type=F32)
    o_ref[...] = x + jnp.dot(mix.astype(BF16), wout_ref[...], preferred_element_type=F32)


def _merge(x, g, wgl, oa, ob, oc, wpa, wpb, wpc, wout, *, tm):
    t, d = x.shape
    row = lambda i: (i, 0)
    fixed = lambda i: (0, 0)
    return pl.pallas_call(
        _merge_kernel,
        grid=(t // tm,),
        in_specs=[
            pl.BlockSpec((tm, d), row),
            pl.BlockSpec((1, d), fixed),
            pl.BlockSpec((d, 3 * d), fixed),
            pl.BlockSpec((tm, 512), row),
            pl.BlockSpec((tm, 512), row),
            pl.BlockSpec((tm, 512), row),
            pl.BlockSpec((512, d), fixed),
            pl.BlockSpec((512, d), fixed),
            pl.BlockSpec((512, d), fixed),
            pl.BlockSpec((d, d), fixed),
        ],
        out_specs=pl.BlockSpec((tm, d), row),
        out_shape=jax.ShapeDtypeStruct((t, d), F32),
        compiler_params=_cparams("parallel"),
    )(x, g.reshape(1, d), wgl, oa, ob, oc, wpa, wpb, wpc, wout)


def _first_argmax(vals, lane):
    m = jnp.max(vals, axis=-1, keepdims=True)
    idx = jnp.min(jnp.where(vals == m, lane, LANES), axis=-1, keepdims=True)
    return m, idx


def _pack_bf16_pairs(h):
    n = h.shape[1] // 2
    bits = lax.bitcast_convert_type(h.astype(jnp.bfloat16).astype(F32), jnp.int32)
    return lax.shift_right_logical(bits[:, :n], 16) | bits[:, n:]


def _unpack_bf16_pairs(w):
    lo = lax.bitcast_convert_type(w << 16, F32).astype(BF16)
    hi = lax.bitcast_convert_type(w & HIGH_HALF, F32).astype(BF16)
    return lo, hi


def _router_kernel(x_ref, g_ref, w_ref, b_ref, hp_ref, meta_ref, seg_ref):
    x = x_ref[...]
    ms = jnp.mean(x * x, axis=-1, keepdims=True)
    h = x * lax.rsqrt(ms + EPS) * g_ref[...]
    hp_ref[...] = _pack_bf16_pairs(h)
    logits = jnp.dot(h, w_ref[...], preferred_element_type=F32, precision=lax.Precision.HIGHEST) + b_ref[...]
    lane = lax.broadcasted_iota(jnp.int32, logits.shape, 1)
    is_grp = (lane >= N_EXPERTS) & (lane < N_EXPERTS + N_GROUPS)
    gl = jnp.where(is_grp, logits, -jnp.inf)
    gmax, gidx = _first_argmax(gl, lane)
    pg = 1.0 / jnp.sum(jnp.exp(gl - gmax), axis=-1, keepdims=True)
    gsel = gidx - N_EXPERTS
    in_grp = (lane >> 3) == gsel
    el = jnp.where(in_grp, logits, -jnp.inf)
    m1, i1 = _first_argmax(el, lane)
    z = jnp.sum(jnp.exp(el - m1), axis=-1, keepdims=True)
    el2 = jnp.where(lane == i1, -jnp.inf, el)
    m2, i2 = _first_argmax(el2, lane)
    pe1 = 1.0 / z
    pe2 = jnp.exp(m2 - m1) / z
    den = pe1 + pe2
    w1 = pg * pe1 / den
    w2 = pg * pe2 / den

    sel1 = lane == i1
    sel2 = lane == i2
    onehot = jnp.where(sel1 | sel2, 1.0, 0.0)
    a = lax.broadcasted_iota(jnp.int32, (LANES, LANES), 0)
    b = lax.broadcasted_iota(jnp.int32, (LANES, LANES), 1)
    lower = jnp.where(b < a, 1.0, 0.0).astype(BF16)
    carry = jnp.zeros((1, LANES), F32)
    ranks = []
    for c in range(MOE_TILE // LANES):
        blk = onehot[c * LANES:(c + 1) * LANES]
        ranks.append(jnp.dot(lower, blk.astype(BF16), preferred_element_type=F32) + carry)
        carry = carry + jnp.sum(blk, axis=0, keepdims=True)
    rank = jnp.concatenate(ranks, axis=0)
    cnt = jnp.broadcast_to(carry, (8, LANES))
    seg = jnp.floor((cnt + 7.0) * 0.125) * 8.0
    lane8 = lax.broadcasted_iota(jnp.int32, (8, LANES), 1)
    scan = seg
    for k in (1, 2, 4, 8, 16, 32, 64):
        scan = scan + jnp.where(lane8 >= k, pltpu.roll(scan, k, 1), 0.0)
    off = scan - seg
    where_row = rank + off[0:1]
    pos1 = jnp.sum(jnp.where(sel1, where_row, 0.0), axis=-1, keepdims=True)
    pos2 = jnp.sum(jnp.where(sel2, where_row, 0.0), axis=-1, keepdims=True)
    meta_ref[...] = (jnp.where(lane == 0, pos1, 0.0) + jnp.where(lane == 1, pos2, 0.0)
                     + jnp.where(lane == 2, w1, 0.0) + jnp.where(lane == 3, w2, 0.0))
    row8 = lax.broadcasted_iota(jnp.int32, (8, LANES), 0)
    seg_ref[...] = jnp.where(row8 == 0, cnt, jnp.where(row8 == 1, off, 0.0)).astype(jnp.int32)


def _router(x, g, w, b):
    t, d = x.shape
    nt = t // MOE_TILE
    return pl.pallas_call(
        _router_kernel,
        grid=(nt,),
        in_specs=[
            pl.BlockSpec((MOE_TILE, d), lambda i: (i, 0)),
            pl.BlockSpec((1, d), lambda i: (0, 0)),
            pl.BlockSpec((d, LANES), lambda i: (0, 0)),
            pl.BlockSpec((1, LANES), lambda i: (0, 0)),
        ],
        out_specs=[
            pl.BlockSpec((MOE_TILE, d // 2), lambda i: (i, 0)),
            pl.BlockSpec((MOE_TILE, LANES), lambda i: (i, 0)),
            pl.BlockSpec((None, 8, LANES), lambda i: (i, 0, 0)),
        ],
        out_shape=[
            jax.ShapeDtypeStruct((t, d // 2), jnp.int32),
            jax.ShapeDtypeStruct((t, LANES), F32),
            jax.ShapeDtypeStruct((nt, 8, LANES), jnp.int32),
        ],
        compiler_params=_cparams("parallel"),
    )(x, g.reshape(1, d), w, b)


def _scatter_kernel(pos_ref, hp_ref, xs_ref):
    xs_ref[...] = jnp.zeros_like(xs_ref)

    def body(t, carry):
        row = hp_ref[pl.ds(t, 1), :]
        xs_ref[pl.ds(pos_ref[0, t], 1), :] = row
        xs_ref[pl.ds(pos_ref[0, MOE_TILE + t], 1), :] = row
        return carry

    lax.fori_loop(0, MOE_TILE, body, 0, unroll=8)


def _scatter_rows(pos, hp):
    nt = pos.shape[0]
    return pl.pallas_call(
        _scatter_kernel,
        grid=(nt,),
        in_specs=[
            pl.BlockSpec((None, 1, 2 * MOE_TILE), lambda i: (i, 0, 0), memory_space=pltpu.SMEM),
            pl.BlockSpec((MOE_TILE, hp.shape[1]), lambda i: (i, 0)),
        ],
        out_specs=pl.BlockSpec((None, XS_ROWS, hp.shape[1]), lambda i: (i, 0, 0)),
        out_shape=jax.ShapeDtypeStruct((nt, XS_ROWS, hp.shape[1]), jnp.int32),
        compiler_params=_cparams("parallel"),
    )(pos, hp)


def _experts_kernel(cnt_ref, off_ref, xs_ref, wg_ref, wu_ref, wd_ref, ys_ref):
    i = pl.program_id(0)
    e = pl.program_id(1)
    half = D_MODEL // 2

    @pl.when(e == 0)
    def _():
        ys_ref[2 * MOE_TILE:XS_ROWS, :] = jnp.zeros((XS_ROWS - 2 * MOE_TILE, D_MODEL), F32)

    n = cnt_ref[i, e]
    off = off_ref[i, e]

    def body(c, carry):
        start = pl.multiple_of(off + c * EXPERT_CHUNK, 8)
        lo, hi = _unpack_bf16_pairs(xs_ref[pl.ds(start, EXPERT_CHUNK), :])
        a = (jnp.dot(lo, wg_ref[0:half].astype(BF16), preferred_element_type=F32)
             + jnp.dot(hi, wg_ref[half:D_MODEL].astype(BF16), preferred_element_type=F32))
        u = (jnp.dot(lo, wu_ref[0:half].astype(BF16), preferred_element_type=F32)
             + jnp.dot(hi, wu_ref[half:D_MODEL].astype(BF16), preferred_element_type=F32))
        hh = (a * jax.nn.sigmoid(a)) * u
        ys_ref[pl.ds(start, EXPERT_CHUNK), :] = jnp.dot(hh.astype(BF16), wd_ref[...].astype(BF16),
                                                        preferred_element_type=F32)
        return carry

    lax.fori_loop(0, (n + EXPERT_CHUNK - 1) // EXPERT_CHUNK, body, 0)


def _experts(cnt, off, xs, wg, wu, wd, layer):
    nt = xs.shape[0]
    d = D_MODEL
    grid_spec = pltpu.PrefetchScalarGridSpec(
        num_scalar_prefetch=2,
        grid=(nt, N_EXPERTS),
        in_specs=[
            pl.BlockSpec((None, XS_ROWS, d // 2), lambda i, e, c, o: (i, 0, 0), pipeline_mode=pl.Buffered(1)),
            pl.BlockSpec((None, None, d, D_EXPERT), lambda i, e, c, o: (layer, e, 0, 0)),
            pl.BlockSpec((None, None, d, D_EXPERT), lambda i, e, c, o: (layer, e, 0, 0)),
            pl.BlockSpec((None, None, D_EXPERT, d), lambda i, e, c, o: (layer, e, 0, 0)),
        ],
        out_specs=pl.BlockSpec((None, XS_ROWS, d), lambda i, e, c, o: (i, 0, 0)),
    )
    return pl.pallas_call(
        _experts_kernel,
        grid_spec=grid_spec,
        out_shape=jax.ShapeDtypeStruct((nt, XS_ROWS, d), F32),
        compiler_params=_cparams("parallel", "arbitrary"),
    )(cnt, off, xs, wg, wu, wd)


def _combine_kernel(pos_ref, wt_ref, x_ref, ys_ref, g_ref, o_ref, *, final_norm):
    s = pl.program_id(1)

    def body(tl, carry):
        t = s * COMBINE_ROWS + tl
        y = (ys_ref[pl.ds(pos_ref[0, t], 1), :] * wt_ref[0, t]
             + ys_ref[pl.ds(pos_ref[0, MOE_TILE + t], 1), :] * wt_ref[0, MOE_TILE + t])
        o_ref[pl.ds(tl, 1), :] = x_ref[pl.ds(tl, 1), :] + y
        return carry

    lax.fori_loop(0, COMBINE_ROWS, body, 0, unroll=8)
    if final_norm:
        o_ref[...] = _rms(o_ref[...], g_ref[...])


def _combine(pos, wt, x, ys, final_g):
    t, d = x.shape
    nt = pos.shape[0]
    sub = MOE_TILE // COMBINE_ROWS
    g = jnp.ones((d,), F32) if final_g is None else final_g
    return pl.pallas_call(
        functools.partial(_combine_kernel, final_norm=final_g is not None),
        grid=(nt, sub),
        in_specs=[
            pl.BlockSpec((None, 1, 2 * MOE_TILE), lambda i, s: (i, 0, 0), memory_space=pltpu.SMEM),
            pl.BlockSpec((None, 1, 2 * MOE_TILE), lambda i, s: (i, 0, 0), memory_space=pltpu.SMEM),
            pl.BlockSpec((COMBINE_ROWS, d), lambda i, s: (i * sub + s, 0)),
            pl.BlockSpec((None, XS_ROWS, d), lambda i, s: (i, 0, 0)),
            pl.BlockSpec((1, d), lambda i, s: (0, 0)),
        ],
        out_specs=pl.BlockSpec((COMBINE_ROWS, d), lambda i, s: (i * sub + s, 0)),
        out_shape=jax.ShapeDtypeStruct((t, d), F32),
        compiler_params=_cparams("parallel", "arbitrary"),
    )(pos, wt, x, ys, g.reshape(1, d))


def _moe(x, g, w_r, b_r, wg, wu, wd, layer, final_g=None):
    t = x.shape[0]
    nt = t // MOE_TILE
    hp, meta, seg = _router(x, g, w_r, b_r)
    pair_major = lambda m: m.reshape(nt, MOE_TILE, 2).transpose(0, 2, 1).reshape(nt, 1, 2 * MOE_TILE)
    pos = pair_major(meta[:, 0:2].astype(jnp.int32))
    wt = pair_major(meta[:, 2:4])
    xs = _scatter_rows(pos, hp)
    ys = _experts(seg[:, 0, :N_EXPERTS], seg[:, 1, :N_EXPERTS], xs, wg, wu, wd, layer)
    return _combine(pos, wt, x, ys, final_g)


def _pad_cols(w, n):
    return jnp.pad(w, ((0, 0), (0, n - w.shape[1])))


def kernel(x, attn_norm_g, w_in, b_q_norm_g, b_w_uq, b_kv_norm_g, b_w_ukv, c_rel_bias, w_proj_a, w_proj_b, w_proj_c, w_out, ffn_norm_g, w_group, b_group, w_router, b_router, w_gate, w_up, w_down, final_norm_g):
    bsz, seq, d = x.shape
    assert (seq, d) == (SEQ, D_MODEL)
    t = bsz * seq
    depth = w_in.shape[0]
    tm = 512

    tab_a = _rope_table(A_ROT, A_HEAD_DIM)
    tab_i = _rope_table(IDX_ROT, IDX_DIM)
    tab_i_half = _rope_table(IDX_ROT, IDX_DIM, active_lanes=IDX_DIM)
    tab_b = _rope_table(B_ROPE, B_ROPE)
    tab_b_half = _rope_table(B_ROPE, B_ROPE, active_lanes=B_ROPE)

    tabs = (tab_a, tab_i, tab_i_half, tab_b, tab_b_half)
    xf = x.reshape(t, d)
    for l in range(depth):
        w = w_in[l]
        w_a = w[:, 0:768].astype(BF16)
        w_i = _pad_cols(w[:, 768:1352], 640).astype(BF16)
        w_b = _pad_cols(w[:, 1352:1800], 512).astype(BF16)
        w_c = jnp.concatenate([w[:, 2312:3336], w[:, 1800:2312]], axis=1).astype(BF16)
        w_g = w[:, 3336:6408].astype(BF16)
        g_attn = attn_norm_g[l]
        w_uq = b_w_uq[l].reshape(B_Q_RANK, B_HEADS, B_NOPE + B_ROPE)
        w_uq = jnp.concatenate([w_uq[:, :, :B_NOPE].reshape(B_Q_RANK, -1),
                                w_uq[:, :, B_NOPE:].reshape(B_Q_RANK, -1)], axis=1).astype(BF16)
        w_ukv = b_w_ukv[l].reshape(B_KV_RANK, B_HEADS, B_NOPE + B_V)
        w_ukv = jnp.concatenate([w_ukv[:, :, :B_NOPE].reshape(B_KV_RANK, -1),
                                 w_ukv[:, :, B_NOPE:].reshape(B_KV_RANK, -1)], axis=1).astype(BF16)

        za, zi, qb, kvb, kr, zc = _in_proj(xf, g_attn, w_a, w_i, w_b, w_c, b_q_norm_g[l], w_uq,
                                           b_kv_norm_g[l], w_ukv, tabs, tm=tm)
        per_seq = lambda z: z.reshape(bsz, seq, -1)
        o_a = _dsa_attention(per_seq(za), per_seq(zi), bsz)
        o_b = _mla_attention(per_seq(qb), per_seq(kvb), per_seq(kr), bsz)
        o_c = _band_attention(per_seq(zc), _band_bias(c_rel_bias[l]), bsz)

        xf = _merge(xf, g_attn, w_g, o_a.reshape(t, -1), o_b.reshape(t, -1), o_c.reshape(t, -1),
                    w_proj_a[l].astype(BF16), w_proj_b[l].astype(BF16), w_proj_c[l].astype(BF16),
                    w_out[l].astype(BF16), tm=tm)

        w_r = _pad_cols(jnp.concatenate([w_router[l], w_group[l]], axis=1), LANES)
        b_r = _pad_cols(jnp.concatenate([b_router[l], b_group[l]])[None, :], LANES)
        xf = _moe(xf, ffn_norm_g[l], w_r, b_r, w_gate, w_up, w_down, l,
                  final_g=final_norm_g if l == depth - 1 else None)

    return xf.reshape(bsz, seq, d)
```

```python
import functools

import numpy as np
import jax
import jax.numpy as jnp
from jax import lax
from jax.experimental import pallas as pl
from jax.experimental.pallas import tpu as pltpu

F32 = jnp.float32
BF16 = jnp.bfloat16

LANES = 128
D_MODEL = 1024
SEQ = 2048
CHUNK = 64
Q_BLOCK = 128
ROPE_THETA = 500000.0
EPS = 1e-6

A_HEADS = 4
A_HEAD_DIM = 128
A_ROT = 32
IDX_HEADS = 8
IDX_DIM = 64
IDX_ROT = 16
TOPK = 256
B_HEADS = 4
B_NOPE = 128
B_ROPE = 64
B_V = 128
B_Q_RANK = 256
B_KV_RANK = 128
C_HEADS = 4
C_HEAD_DIM = 128
C_LEFT_CHUNKS = 8
REL_CLIP = 128
N_GROUPS = 4
EXPERTS_PER_GROUP = 8
N_EXPERTS = 32
D_EXPERT = 256

C_KEY_BLOCKS = C_LEFT_CHUNKS * CHUNK // Q_BLOCK + 1
N_QB = SEQ // Q_BLOCK
KV_VARIANTS = 4
KV_STEP = SEQ // KV_VARIANTS

VMEM_LIMIT = 56 * 1024 * 1024

MOE_TILE = 2048
EXPERT_CHUNK = 256
XS_ROWS = 2 * MOE_TILE + 2 * EXPERT_CHUNK
assert XS_ROWS >= 2 * MOE_TILE + N_EXPERTS * 7 + EXPERT_CHUNK - 1
COMBINE_ROWS = 512
HIGH_HALF = -65536

INT_MIN = -2 ** 31
NEG_INF_KEY = int(np.array(0x807FFFFF, np.uint32).view(np.int32))

NT_DIMS = (((1,), (1,)), ((), ()))


def _nt_dot(a, b):
    return lax.dot_general(a, b, NT_DIMS, preferred_element_type=F32)


def _cparams(*sem):
    return pltpu.CompilerParams(dimension_semantics=sem, vmem_limit_bytes=VMEM_LIMIT)


def _rope_table(rot, period, active_lanes=LANES):
    half = rot // 2
    lane = np.arange(LANES)
    p = lane % period
    first = (p < half) & (lane < active_lanes)
    second = (p >= half) & (p < rot) & (lane < active_lanes)
    idx = np.where(first, p, np.where(second, p - half, 0))
    pos = jnp.arange(SEQ, dtype=F32)
    inv = ROPE_THETA ** (-jnp.arange(0, rot, 2, dtype=F32) / rot)
    ang = pos[:, None] * inv[idx][None, :]
    cos, sin = jnp.cos(ang), jnp.sin(ang)
    c = jnp.where(first | second, cos, 1.0)
    s_prev = jnp.where(second, sin, 0.0)
    s_next = jnp.where(first, -sin, 0.0)
    return jnp.stack([c, s_prev, s_next]).astype(F32)


def _rms(x, g):
    ms = jnp.mean(x * x, axis=-1, keepdims=True)
    return x * lax.rsqrt(ms + EPS) * g


def _rope_tiles(z, tile_tab, tabs, halves):
    out = []
    for c, t in enumerate(tile_tab):
        zt = z[:, c * LANES:(c + 1) * LANES]
        if t >= 0:
            tab, half = tabs[t], halves[t]
            zt = (zt * tab[0] + pltpu.roll(zt, half, 1) * tab[1]
                  + pltpu.roll(zt, LANES - half, 1) * tab[2])
        out.append(zt)
    return out


def _store_tiles(o_ref, tiles):
    for c, zt in enumerate(tiles):
        o_ref[:, c * LANES:(c + 1) * LANES] = zt.astype(o_ref.dtype)


ROPE_HALVES = (A_ROT // 2, IDX_ROT // 2, IDX_ROT // 2, B_ROPE // 2, B_ROPE // 2)


def _in_proj_kernel(x_ref, g_ref, wa_ref, wi_ref, wb_ref, wc_ref, gq_ref, wuq_ref, gkv_ref, wukv_ref,
                    ta_ref, ti_ref, tih_ref, tb_ref, tbh_ref,
                    za_ref, zi_ref, qb_ref, kvb_ref, kr_ref, zc_ref):
    tabs = (ta_ref, ti_ref, tih_ref, tb_ref, tbh_ref)
    rope = functools.partial(_rope_tiles, tabs=tabs, halves=ROPE_HALVES)
    h = _rms(x_ref[...], g_ref[...]).astype(BF16)
    dot = functools.partial(jnp.dot, preferred_element_type=F32)
    _store_tiles(za_ref, rope(dot(h, wa_ref[...]), (0, 0, 0, 0, 0, -1)))
    _store_tiles(zi_ref, rope(dot(h, wi_ref[...]), (1, 1, 1, 1, 2)))
    _store_tiles(zc_ref, rope(dot(h, wc_ref[...]), (-1,) * (3 * C_HEADS)))
    zb = dot(h, wb_ref[...])
    _store_tiles(kr_ref, rope(zb[:, B_Q_RANK + B_KV_RANK:], (4,)))
    cq = _rms(zb[:, :B_Q_RANK], gq_ref[...]).astype(BF16)
    _store_tiles(qb_ref, rope(dot(cq, wuq_ref[...]), (-1, -1, -1, -1, 3, 3)))
    ckv = _rms(zb[:, B_Q_RANK:B_Q_RANK + B_KV_RANK], gkv_ref[...]).astype(BF16)
    _store_tiles(kvb_ref, rope(dot(ckv, wukv_ref[...]), (-1,) * (2 * B_HEADS)))


def _in_proj(x, g, w_a, w_i, w_b, w_c, g_q, w_uq, g_kv, w_ukv, tabs, *, tm):
    t, d = x.shape
    seq_tiles = SEQ // tm
    fixed = lambda i: (0, 0)
    row = lambda i: (i, 0)
    weights = (w_a, w_i, w_b, w_c)
    outs = ((768, BF16), (640, F32), (768, BF16), (1024, BF16), (LANES, BF16), (1536, BF16))
    return pl.pallas_call(
        _in_proj_kernel,
        grid=(t // tm,),
        in_specs=[pl.BlockSpec((tm, d), row), pl.BlockSpec((1, d), fixed)]
        + [pl.BlockSpec(w.shape, fixed) for w in weights]
        + [pl.BlockSpec((1, B_Q_RANK), fixed), pl.BlockSpec(w_uq.shape, fixed),
           pl.BlockSpec((1, B_KV_RANK), fixed), pl.BlockSpec(w_ukv.shape, fixed)]
        + [pl.BlockSpec((3, tm, LANES), lambda i: (0, i % seq_tiles, 0)) for _ in tabs],
        out_specs=[pl.BlockSpec((tm, n), row) for n, _ in outs],
        out_shape=[jax.ShapeDtypeStruct((t, n), dt) for n, dt in outs],
        compiler_params=_cparams("parallel"),
    )(x, g.reshape(1, d), *weights, g_q.reshape(1, -1), w_uq, g_kv.reshape(1, -1), w_ukv, *tabs)


def _chunk_causal_mask(j, n_keys):
    qpos = j * Q_BLOCK + lax.broadcasted_iota(jnp.int32, (Q_BLOCK, n_keys), 0)
    kpos = lax.broadcasted_iota(jnp.int32, (Q_BLOCK, n_keys), 1)
    return (kpos >> 6) <= (qpos >> 6)


SEARCH_ROWS = 256
HALF_RANGE = 1 << 15


def _dsa_select(iq_ref, ikw_ref, key_ref, hi_ref, lo_ref, bias_ref, n_keys):
    j = pl.program_id(1)
    n_blk = n_keys // Q_BLOCK
    n_grp = n_keys // SEARCH_ROWS
    n_pack = n_keys // 16
    row0 = pl.multiple_of(j * Q_BLOCK, Q_BLOCK)
    sub = lax.broadcasted_iota(jnp.int32, (SEARCH_ROWS, Q_BLOCK), 0)
    lane = lax.broadcasted_iota(jnp.int32, (SEARCH_ROWS, Q_BLOCK), 1)
    q_chunk = (row0 + lane) >> 6

    iq = iq_ref[...].astype(BF16)
    iq_stack = jnp.concatenate([iq[:, h * IDX_DIM:(h + 1) * IDX_DIM] for h in range(IDX_HEADS)], axis=0)
    iw_t = ikw_ref[pl.ds(row0, Q_BLOCK), :].T * (IDX_HEADS ** -0.5)

    for g in range(n_grp):
        k0 = g * SEARCH_ROWS
        ik = ikw_ref[k0:k0 + SEARCH_ROWS, 0:IDX_DIM].astype(BF16)
        score = jnp.zeros((SEARCH_ROWS, Q_BLOCK), F32)
        for hp in range(IDX_HEADS // 2):
            r = _nt_dot(ik, iq_stack[hp * 2 * Q_BLOCK:(hp + 1) * 2 * Q_BLOCK])
            for u in range(2):
                h = 2 * hp + u
                rel = jnp.maximum(r[:, u * Q_BLOCK:(u + 1) * Q_BLOCK] * (IDX_DIM ** -0.5), 0.0)
                score = score + rel * iw_t[IDX_DIM + h:IDX_DIM + h + 1, :]
        allowed = ((k0 + sub) >> 6) <= q_chunk
        score = jnp.where(score == 0.0, 0.0, score)
        score = jnp.where(allowed, score, -jnp.inf)
        bits = lax.bitcast_convert_type(score, jnp.int32)
        key = bits ^ ((bits >> 31) & 0x7FFFFFFF)
        key_ref[k0:k0 + SEARCH_ROWS, :] = key
        hi_ref[k0:k0 + SEARCH_ROWS, :] = (key >> 16).astype(jnp.int16)
        lo_ref[k0:k0 + SEARCH_ROWS, :] = ((key & 0xFFFF) - HALF_RANGE).astype(jnp.int16)

    def count(pred):
        accs = [jnp.zeros((8, Q_BLOCK), F32)] * 4
        for r in range(n_keys // 8):
            accs[r % 4] = accs[r % 4] + jnp.where(pred(key_ref[8 * r:8 * (r + 1), :]), 1.0, 0.0)
        return jnp.sum(accs[0] + accs[1] + accs[2] + accs[3], axis=0, keepdims=True)

    one16 = jnp.ones((16, Q_BLOCK), jnp.int16)
    zero16 = jnp.zeros((16, Q_BLOCK), jnp.int16)

    def search16(ref):
        def rnd(i, base):
            cand = base + jnp.left_shift(jnp.int32(1), 15 - i)
            cand16 = jnp.broadcast_to(cand, (16, Q_BLOCK)).astype(jnp.int16)
            accs = [zero16] * 4
            for r in range(n_pack):
                accs[r % 4] = accs[r % 4] + jnp.where(ref[16 * r:16 * (r + 1), :] >= cand16, one16, zero16)
            total = (accs[0] + accs[1]) + (accs[2] + accs[3])
            cnt = jnp.sum(total.astype(jnp.int32), axis=0, keepdims=True)
            return jnp.where(cnt >= TOPK, cand, base)
        return lax.fori_loop(0, 16, rnd, jnp.full((1, Q_BLOCK), -HALF_RANGE, jnp.int32))

    top = search16(hi_ref)
    top16 = jnp.broadcast_to(top, (16, Q_BLOCK)).astype(jnp.int16)
    for r in range(n_pack):
        rows = slice(16 * r, 16 * (r + 1))
        hi = hi_ref[rows, :]
        lo_ref[rows, :] = jnp.where(hi > top16, jnp.int16(HALF_RANGE - 1),
                                    jnp.where(hi == top16, lo_ref[rows, :], jnp.int16(-HALF_RANGE)))
    thr = (top << 16) | (search16(lo_ref) + HALF_RANGE)
    thr8 = jnp.broadcast_to(thr, (8, Q_BLOCK))
    cnt_gt = count(lambda kk: kk > thr8)
    cnt_ge = count(lambda kk: kk >= thr8)
    tie_cols = (cnt_ge > TOPK) & (thr > NEG_INF_KEY)
    has_tie = jnp.max(jnp.where(tie_cols, 1.0, 0.0)) > 0.0

    for kb in range(n_blk):
        kk = key_ref[kb * Q_BLOCK:(kb + 1) * Q_BLOCK, :]
        bias_ref[kb * Q_BLOCK:(kb + 1) * Q_BLOCK, :] = jnp.where((kk >= thr) & (kk > NEG_INF_KEY), 0.0, -jnp.inf).T

    @pl.when(has_tie)
    def _():
        need = TOPK - cnt_gt
        row = lax.broadcasted_iota(jnp.int32, (Q_BLOCK, Q_BLOCK), 0)
        col = lax.broadcasted_iota(jnp.int32, (Q_BLOCK, Q_BLOCK), 1)
        lower = jnp.where(col < row, 1.0, 0.0).astype(BF16)

        def tie_block(kb, seen):
            k0 = pl.multiple_of(kb * Q_BLOCK, Q_BLOCK)
            kk = key_ref[pl.ds(k0, Q_BLOCK), :]
            eq = jnp.where(kk == thr, 1.0, 0.0)
            before = jnp.dot(lower, eq.astype(BF16), preferred_element_type=F32) + seen
            keep = (kk > thr) | ((kk == thr) & (before < need))
            bias_ref[pl.ds(k0, Q_BLOCK), :] = jnp.where(keep & (kk > NEG_INF_KEY), 0.0, -jnp.inf).T
            return seen + jnp.sum(eq, axis=0, keepdims=True)

        lax.fori_loop(0, n_blk, tie_block, jnp.zeros((1, Q_BLOCK), F32))


def _dsa_attend(q_ref, kv_ref, bias_ref, o_ref, n_keys):
    q = q_ref[...]
    k = kv_ref[0:n_keys, 0:A_HEAD_DIM]
    v = kv_ref[0:n_keys, A_HEAD_DIM:2 * A_HEAD_DIM]
    bias = jnp.concatenate([bias_ref[c * Q_BLOCK:(c + 1) * Q_BLOCK, :] for c in range(n_keys // Q_BLOCK)],
                           axis=1)
    for h in range(A_HEADS):
        logits = _nt_dot(q[:, h * A_HEAD_DIM:(h + 1) * A_HEAD_DIM], k) * (A_HEAD_DIM ** -0.5) + bias
        m = jnp.max(logits, axis=-1, keepdims=True)
        p = jnp.exp(logits - m)
        l = jnp.sum(p, axis=-1, keepdims=True)
        o = jnp.dot(p.astype(BF16), v, preferred_element_type=F32) / l
        o_ref[:, h * A_HEAD_DIM:(h + 1) * A_HEAD_DIM] = o.astype(o_ref.dtype)


def _dsa_kernel(q_ref, kv_ref, iq_ref, ikw_ref, o_ref, key_ref, hi_ref, lo_ref, bias_ref):
    j = pl.program_id(1)
    for v in range(KV_VARIANTS):
        @pl.when(j // (N_QB // KV_VARIANTS) == v)
        def _(v=v):
            n_keys = KV_STEP * (v + 1)
            _dsa_select(iq_ref, ikw_ref, key_ref, hi_ref, lo_ref, bias_ref, n_keys)
            _dsa_attend(q_ref, kv_ref, bias_ref, o_ref, n_keys)


def _dsa_attention(za, zi, bsz):
    return pl.pallas_call(
        _dsa_kernel,
        grid=(bsz, N_QB),
        in_specs=[
            pl.BlockSpec((None, Q_BLOCK, 512), lambda b, j: (b, j, 0)),
            pl.BlockSpec((None, SEQ, 256), lambda b, j: (b, 0, 2)),
            pl.BlockSpec((None, Q_BLOCK, 512), lambda b, j: (b, j, 0)),
            pl.BlockSpec((None, SEQ, LANES), lambda b, j: (b, 0, 4)),
        ],
        out_specs=pl.BlockSpec((None, Q_BLOCK, 512), lambda b, j: (b, j, 0)),
        out_shape=jax.ShapeDtypeStruct((bsz, SEQ, 512), BF16),
        scratch_shapes=[pltpu.VMEM((SEQ, Q_BLOCK), jnp.int32),
                        pltpu.VMEM((SEQ, Q_BLOCK), jnp.int16),
                        pltpu.VMEM((SEQ, Q_BLOCK), jnp.int16),
                        pltpu.VMEM((SEQ, Q_BLOCK), F32)],
        compiler_params=_cparams("parallel", "arbitrary"),
    )(za, za, zi, zi)


def _mla_body(q_ref, kv_ref, kr_ref, o_ref, n_keys):
    j = pl.program_id(1)
    mask = _chunk_causal_mask(j, n_keys)
    kr = kr_ref[0:n_keys, 0:B_ROPE]
    scale = (B_NOPE + B_ROPE) ** -0.5
    for h in range(B_HEADS):
        qcat = jnp.concatenate(
            [q_ref[:, h * B_NOPE:(h + 1) * B_NOPE],
             q_ref[:, B_HEADS * B_NOPE + h * B_ROPE:B_HEADS * B_NOPE + (h + 1) * B_ROPE]], axis=1)
        kcat = jnp.concatenate([kv_ref[0:n_keys, h * B_NOPE:(h + 1) * B_NOPE], kr], axis=1)
        vv = kv_ref[0:n_keys, B_HEADS * B_NOPE + h * B_V:B_HEADS * B_NOPE + (h + 1) * B_V]
        s = _nt_dot(qcat, kcat) * scale
        s = jnp.where(mask, s, -jnp.inf)
        m = jnp.max(s, axis=-1, keepdims=True)
        p = jnp.exp(s - m)
        l = jnp.sum(p, axis=-1, keepdims=True)
        o = jnp.dot(p.astype(BF16), vv, preferred_element_type=F32) / l
        o_ref[:, h * B_V:(h + 1) * B_V] = o.astype(o_ref.dtype)


def _mla_kernel(q_ref, kv_ref, kr_ref, o_ref):
    j = pl.program_id(1)
    for v in range(KV_VARIANTS):
        @pl.when(j // (N_QB // KV_VARIANTS) == v)
        def _(v=v):
            _mla_body(q_ref, kv_ref, kr_ref, o_ref, KV_STEP * (v + 1))


def _mla_attention(qb, kvb, kr, bsz):
    return pl.pallas_call(
        _mla_kernel,
        grid=(bsz, N_QB),
        in_specs=[
            pl.BlockSpec((None, Q_BLOCK, 768), lambda b, j: (b, j, 0)),
            pl.BlockSpec((None, SEQ, 1024), lambda b, j: (b, 0, 0)),
            pl.BlockSpec((None, SEQ, LANES), lambda b, j: (b, 0, 0)),
        ],
        out_specs=pl.BlockSpec((None, Q_BLOCK, 512), lambda b, j: (b, j, 0)),
        out_shape=jax.ShapeDtypeStruct((bsz, SEQ, 512), BF16),
        compiler_params=_cparams("parallel", "arbitrary"),
    )(qb, kvb, kr)


def _band_bias(rel_table):
    n = 2 * REL_CLIP + 1
    period = 2 * n - 1
    heads = rel_table.shape[0]
    ext = jnp.concatenate([rel_table, jnp.broadcast_to(rel_table[:, n - 1:n], (heads, n - 1))], axis=1)
    kj = np.arange(Q_BLOCK)[:, None]
    qi = np.arange(Q_BLOCK)[None, :]
    out = []
    for d in range(C_KEY_BLOCKS):
        base = d * Q_BLOCK + REL_CLIP
        if base - (Q_BLOCK - 1) >= n - 1:
            bias = jnp.broadcast_to(rel_table[:, n - 1][:, None, None], (heads, Q_BLOCK, Q_BLOCK))
        else:
            shifted = jnp.roll(ext, -base, axis=1)
            bias = jnp.tile(shifted, (1, Q_BLOCK))[:, :Q_BLOCK * (period - 1)]
            bias = bias.reshape(heads, Q_BLOCK, period - 1)[:, :, :Q_BLOCK]
        cdiff = 2 * d + qi // CHUNK - kj // CHUNK
        valid = (cdiff >= 0) & (cdiff <= C_LEFT_CHUNKS)
        out.append(jnp.where(valid[None], bias.astype(F32), -jnp.inf))
    return jnp.swapaxes(jnp.stack(out, axis=1), 2, 3)


def _band_kernel(q_ref, kv_ref, bias_ref, o_ref):
    j = pl.program_id(1)
    scale = C_HEAD_DIM ** -0.5
    for h in range(C_HEADS):
        q = q_ref[:, h * C_HEAD_DIM:(h + 1) * C_HEAD_DIM]
        ss, vs = [], []
        for d in range(C_KEY_BLOCKS):
            kb = j - d
            row0 = pl.multiple_of(jnp.maximum(kb, 0) * Q_BLOCK, Q_BLOCK)
            k = kv_ref[pl.ds(row0, Q_BLOCK), h * C_HEAD_DIM:(h + 1) * C_HEAD_DIM]
            vs.append(kv_ref[pl.ds(row0, Q_BLOCK), (C_HEADS + h) * C_HEAD_DIM:(C_HEADS + h + 1) * C_HEAD_DIM])
            s = _nt_dot(q, k) * scale + bias_ref[h, d]
            ss.append(jnp.where(kb >= 0, s, -jnp.inf))
        s_all = jnp.concatenate(ss, axis=1)
        v_all = jnp.concatenate(vs, axis=0)
        m = jnp.max(s_all, axis=-1, keepdims=True)
        p = jnp.exp(s_all - m)
        l = jnp.sum(p, axis=-1, keepdims=True)
        o = jnp.dot(p.astype(BF16), v_all, preferred_element_type=F32) / l
        o_ref[:, h * C_HEAD_DIM:(h + 1) * C_HEAD_DIM] = o.astype(o_ref.dtype)


def _band_attention(zc, bias, bsz):
    return pl.pallas_call(
        _band_kernel,
        grid=(bsz, N_QB),
        in_specs=[
            pl.BlockSpec((None, Q_BLOCK, 512), lambda b, j: (b, j, 2)),
            pl.BlockSpec((None, SEQ, 1024), lambda b, j: (b, 0, 0)),
            pl.BlockSpec((C_HEADS, C_KEY_BLOCKS, Q_BLOCK, Q_BLOCK), lambda b, j: (0, 0, 0, 0)),
        ],
        out_specs=pl.BlockSpec((None, Q_BLOCK, 512), lambda b, j: (b, j, 0)),
        out_shape=jax.ShapeDtypeStruct((bsz, SEQ, 512), BF16),
        compiler_params=_cparams("parallel", "arbitrary"),
    )(zc, zc, bias)


def _merge_kernel(x_ref, g_ref, wgl_ref, oa_ref, ob_ref, oc_ref, wpa_ref, wpb_ref, wpc_ref, wout_ref, o_ref):
    x = x_ref[...]
    ms = jnp.mean(x * x, axis=-1, keepdims=True)
    h = (x * lax.rsqrt(ms + EPS) * g_ref[...]).astype(BF16)
    mix = jnp.zeros(x.shape, F32)
    for i, (o_in, wp) in enumerate(((oa_ref, wpa_ref), (ob_ref, wpb_ref), (oc_ref, wpc_ref))):
        gl = jnp.dot(h, wgl_ref[:, i * D_MODEL:(i + 1) * D_MODEL], preferred_element_type=F32)
        gate = jax.nn.sigmoid(gl)
        mix = mix + gate * jnp.dot(o_in[...], wp[...], preferred_element_type=F32)
    o_ref[...] = x + jnp.dot(mix.astype(BF16), wout_ref[...], preferred_element_type=F32)


def _merge(x, g, wgl, oa, ob, oc, wpa, wpb, wpc, wout, *, tm):
    t, d = x.shape
    row = lambda i: (i, 0)
    fixed = lambda i: (0, 0)
    return pl.pallas_call(
        _merge_kernel,
        grid=(t // tm,),
        in_specs=[
            pl.BlockSpec((tm, d), row),
            pl.BlockSpec((1, d), fixed),
            pl.BlockSpec((d, 3 * d), fixed),
            pl.BlockSpec((tm, 512), row),
            pl.BlockSpec((tm, 512), row),
            pl.BlockSpec((tm, 512), row),
            pl.BlockSpec((512, d), fixed),
            pl.BlockSpec((512, d), fixed),
            pl.BlockSpec((512, d), fixed),
            pl.BlockSpec((d, d), fixed),
        ],
        out_specs=pl.BlockSpec((tm, d), row),
        out_shape=jax.ShapeDtypeStruct((t, d), F32),
        compiler_params=_cparams("parallel"),
    )(x, g.reshape(1, d), wgl, oa, ob, oc, wpa, wpb, wpc, wout)


def _first_argmax(vals, lane):
    m = jnp.max(vals, axis=-1, keepdims=True)
    idx = jnp.min(jnp.where(vals == m, lane, LANES), axis=-1, keepdims=True)
    return m, idx


def _pack_bf16_pairs(h):
    n = h.shape[1] // 2
    bits = lax.bitcast_convert_type(h.astype(jnp.bfloat16).astype(F32), jnp.int32)
    return lax.shift_right_logical(bits[:, :n], 16) | bits[:, n:]


def _unpack_bf16_pairs(w):
    lo = lax.bitcast_convert_type(w << 16, F32).astype(BF16)
    hi = lax.bitcast_convert_type(w & HIGH_HALF, F32).astype(BF16)
    return lo, hi


def _router_kernel(x_ref, g_ref, w_ref, b_ref, hp_ref, meta_ref, seg_ref):
    x = x_ref[...]
    ms = jnp.mean(x * x, axis=-1, keepdims=True)
    h = x * lax.rsqrt(ms + EPS) * g_ref[...]
    hp_ref[...] = _pack_bf16_pairs(h)
    logits = jnp.dot(h, w_ref[...], preferred_element_type=F32, precision=lax.Precision.HIGHEST) + b_ref[...]
    lane = lax.broadcasted_iota(jnp.int32, logits.shape, 1)
    is_grp = (lane >= N_EXPERTS) & (lane < N_EXPERTS + N_GROUPS)
    gl = jnp.where(is_grp, logits, -jnp.inf)
    gmax, gidx = _first_argmax(gl, lane)
    pg = 1.0 / jnp.sum(jnp.exp(gl - gmax), axis=-1, keepdims=True)
    gsel = gidx - N_EXPERTS
    in_grp = (lane >> 3) == gsel
    el = jnp.where(in_grp, logits, -jnp.inf)
    m1, i1 = _first_argmax(el, lane)
    z = jnp.sum(jnp.exp(el - m1), axis=-1, keepdims=True)
    el2 = jnp.where(lane == i1, -jnp.inf, el)
    m2, i2 = _first_argmax(el2, lane)
    pe1 = 1.0 / z
    pe2 = jnp.exp(m2 - m1) / z
    den = pe1 + pe2
    w1 = pg * pe1 / den
    w2 = pg * pe2 / den

    sel1 = lane == i1
    sel2 = lane == i2
    onehot = jnp.where(sel1 | sel2, 1.0, 0.0)
    a = lax.broadcasted_iota(jnp.int32, (LANES, LANES), 0)
    b = lax.broadcasted_iota(jnp.int32, (LANES, LANES), 1)
    lower = jnp.where(b < a, 1.0, 0.0).astype(BF16)
    carry = jnp.zeros((1, LANES), F32)
    ranks = []
    for c in range(MOE_TILE // LANES):
        blk = onehot[c * LANES:(c + 1) * LANES]
        ranks.append(jnp.dot(lower, blk.astype(BF16), preferred_element_type=F32) + carry)
        carry = carry + jnp.sum(blk, axis=0, keepdims=True)
    rank = jnp.concatenate(ranks, axis=0)
    cnt = jnp.broadcast_to(carry, (8, LANES))
    seg = jnp.floor((cnt + 7.0) * 0.125) * 8.0
    lane8 = lax.broadcasted_iota(jnp.int32, (8, LANES), 1)
    scan = seg
    for k in (1, 2, 4, 8, 16, 32, 64):
        scan = scan + jnp.where(lane8 >= k, pltpu.roll(scan, k, 1), 0.0)
    off = scan - seg
    where_row = rank + off[0:1]
    pos1 = jnp.sum(jnp.where(sel1, where_row, 0.0), axis=-1, keepdims=True)
    pos2 = jnp.sum(jnp.where(sel2, where_row, 0.0), axis=-1, keepdims=True)
    meta_ref[...] = (jnp.where(lane == 0, pos1, 0.0) + jnp.where(lane == 1, pos2, 0.0)
                     + jnp.where(lane == 2, w1, 0.0) + jnp.where(lane == 3, w2, 0.0))
    row8 = lax.broadcasted_iota(jnp.int32, (8, LANES), 0)
    seg_ref[...] = jnp.where(row8 == 0, cnt, jnp.where(row8 == 1, off, 0.0)).astype(jnp.int32)


def _router(x, g, w, b):
    t, d = x.shape
    nt = t // MOE_TILE
    return pl.pallas_call(
        _router_kernel,
        grid=(nt,),
        in_specs=[
            pl.BlockSpec((MOE_TILE, d), lambda i: (i, 0)),
            pl.BlockSpec((1, d), lambda i: (0, 0)),
            pl.BlockSpec((d, LANES), lambda i: (0, 0)),
            pl.BlockSpec((1, LANES), lambda i: (0, 0)),
        ],
        out_specs=[
            pl.BlockSpec((MOE_TILE, d // 2), lambda i: (i, 0)),
            pl.BlockSpec((MOE_TILE, LANES), lambda i: (i, 0)),
            pl.BlockSpec((None, 8, LANES), lambda i: (i, 0, 0)),
        ],
        out_shape=[
            jax.ShapeDtypeStruct((t, d // 2), jnp.int32),
            jax.ShapeDtypeStruct((t, LANES), F32),
            jax.ShapeDtypeStruct((nt, 8, LANES), jnp.int32),
        ],
        compiler_params=_cparams("parallel"),
    )(x, g.reshape(1, d), w, b)


def _scatter_kernel(pos_ref, hp_ref, xs_ref):
    xs_ref[...] = jnp.zeros_like(xs_ref)

    def body(t, carry):
        row = hp_ref[pl.ds(t, 1), :]
        xs_ref[pl.ds(pos_ref[0, t], 1), :] = row
        xs_ref[pl.ds(pos_ref[0, MOE_TILE + t], 1), :] = row
        return carry

    lax.fori_loop(0, MOE_TILE, body, 0, unroll=8)


def _scatter_rows(pos, hp):
    nt = pos.shape[0]
    return pl.pallas_call(
        _scatter_kernel,
        grid=(nt,),
        in_specs=[
            pl.BlockSpec((None, 1, 2 * MOE_TILE), lambda i: (i, 0, 0), memory_space=pltpu.SMEM),
            pl.BlockSpec((MOE_TILE, hp.shape[1]), lambda i: (i, 0)),
        ],
        out_specs=pl.BlockSpec((None, XS_ROWS, hp.shape[1]), lambda i: (i, 0, 0)),
        out_shape=jax.ShapeDtypeStruct((nt, XS_ROWS, hp.shape[1]), jnp.int32),
        compiler_params=_cparams("parallel"),
    )(pos, hp)


CAST_EXPERTS = 4


def _cast_kernel(w_ref, o_ref):
    o_ref[...] = w_ref[...].astype(o_ref.dtype)


def _to_bf16(w):
    n_l, n_e, a, b = w.shape
    spec = pl.BlockSpec((None, CAST_EXPERTS, a, b), lambda l, e: (l, e, 0, 0))
    return pl.pallas_call(
        _cast_kernel,
        grid=(n_l, n_e // CAST_EXPERTS),
        in_specs=[spec],
        out_specs=spec,
        out_shape=jax.ShapeDtypeStruct(w.shape, BF16),
        compiler_params=_cparams("parallel", "parallel"),
    )(w)


def _experts_kernel(cnt_ref, off_ref, xs_ref, wg_ref, wu_ref, wd_ref, ys_ref):
    i = pl.program_id(0)
    e = pl.program_id(1)
    half = D_MODEL // 2

    @pl.when(e == 0)
    def _():
        ys_ref[2 * MOE_TILE:XS_ROWS, :] = jnp.zeros((XS_ROWS - 2 * MOE_TILE, D_MODEL), F32)

    n = cnt_ref[i, e]
    off = off_ref[i, e]

    def body(c, carry):
        start = pl.multiple_of(off + c * EXPERT_CHUNK, 8)
        lo, hi = _unpack_bf16_pairs(xs_ref[pl.ds(start, EXPERT_CHUNK), :])
        a = (jnp.dot(lo, wg_ref[0:half], preferred_element_type=F32)
             + jnp.dot(hi, wg_ref[half:D_MODEL], preferred_element_type=F32))
        u = (jnp.dot(lo, wu_ref[0:half], preferred_element_type=F32)
             + jnp.dot(hi, wu_ref[half:D_MODEL], preferred_element_type=F32))
        hh = (a * jax.nn.sigmoid(a)) * u
        ys_ref[pl.ds(start, EXPERT_CHUNK), :] = jnp.dot(hh.astype(BF16), wd_ref[...], preferred_element_type=F32)
        return carry

    lax.fori_loop(0, (n + EXPERT_CHUNK - 1) // EXPERT_CHUNK, body, 0)


def _experts(cnt, off, xs, wg, wu, wd, layer):
    nt = xs.shape[0]
    d = D_MODEL
    grid_spec = pltpu.PrefetchScalarGridSpec(
        num_scalar_prefetch=2,
        grid=(nt, N_EXPERTS),
        in_specs=[
            pl.BlockSpec((None, XS_ROWS, d // 2), lambda i, e, c, o: (i, 0, 0), pipeline_mode=pl.Buffered(1)),
            pl.BlockSpec((None, None, d, D_EXPERT), lambda i, e, c, o: (layer, e, 0, 0)),
            pl.BlockSpec((None, None, d, D_EXPERT), lambda i, e, c, o: (layer, e, 0, 0)),
            pl.BlockSpec((None, None, D_EXPERT, d), lambda i, e, c, o: (layer, e, 0, 0)),
        ],
        out_specs=pl.BlockSpec((None, XS_ROWS, d), lambda i, e, c, o: (i, 0, 0)),
    )
    return pl.pallas_call(
        _experts_kernel,
        grid_spec=grid_spec,
        out_shape=jax.ShapeDtypeStruct((nt, XS_ROWS, d), F32),
        compiler_params=_cparams("parallel", "arbitrary"),
    )(cnt, off, xs, wg, wu, wd)


def _combine_kernel(pos_ref, wt_ref, x_ref, ys_ref, g_ref, o_ref, *, final_norm):
    s = pl.program_id(1)

    def body(tl, carry):
        t = s * COMBINE_ROWS + tl
        y = (ys_ref[pl.ds(pos_ref[0, t], 1), :] * wt_ref[0, t]
             + ys_ref[pl.ds(pos_ref[0, MOE_TILE + t], 1), :] * wt_ref[0, MOE_TILE + t])
        o_ref[pl.ds(tl, 1), :] = x_ref[pl.ds(tl, 1), :] + y
        return carry

    lax.fori_loop(0, COMBINE_ROWS, body, 0, unroll=8)
    if final_norm:
        o_ref[...] = _rms(o_ref[...], g_ref[...])


def _combine(pos, wt, x, ys, final_g):
    t, d = x.shape
    nt = pos.shape[0]
    sub = MOE_TILE // COMBINE_ROWS
    g = jnp.ones((d,), F32) if final_g is None else final_g
    return pl.pallas_call(
        functools.partial(_combine_kernel, final_norm=final_g is not None),
        grid=(nt, sub),
        in_specs=[
            pl.BlockSpec((None, 1, 2 * MOE_TILE), lambda i, s: (i, 0, 0), memory_space=pltpu.SMEM),
            pl.BlockSpec((None, 1, 2 * MOE_TILE), lambda i, s: (i, 0, 0), memory_space=pltpu.SMEM),
            pl.BlockSpec((COMBINE_ROWS, d), lambda i, s: (i * sub + s, 0)),
            pl.BlockSpec((None, XS_ROWS, d), lambda i, s: (i, 0, 0)),
            pl.BlockSpec((1, d), lambda i, s: (0, 0)),
        ],
        out_specs=pl.BlockSpec((COMBINE_ROWS, d), lambda i, s: (i * sub + s, 0)),
        out_shape=jax.ShapeDtypeStruct((t, d), F32),
        compiler_params=_cparams("parallel", "arbitrary"),
    )(pos, wt, x, ys, g.reshape(1, d))


def _moe(x, g, w_r, b_r, wg, wu, wd, layer, final_g=None):
    t = x.shape[0]
    nt = t // MOE_TILE
    hp, meta, seg = _router(x, g, w_r, b_r)
    pair_major = lambda m: m.reshape(nt, MOE_TILE, 2).transpose(0, 2, 1).reshape(nt, 1, 2 * MOE_TILE)
    pos = pair_major(meta[:, 0:2].astype(jnp.int32))
    wt = pair_major(meta[:, 2:4])
    xs = _scatter_rows(pos, hp)
    ys = _experts(seg[:, 0, :N_EXPERTS], seg[:, 1, :N_EXPERTS], xs, wg, wu, wd, layer)
    return _combine(pos, wt, x, ys, final_g)


def _pad_cols(w, n):
    return jnp.pad(w, ((0, 0), (0, n - w.shape[1])))


def kernel(x, attn_norm_g, w_in, b_q_norm_g, b_w_uq, b_kv_norm_g, b_w_ukv, c_rel_bias, w_proj_a, w_proj_b, w_proj_c, w_out, ffn_norm_g, w_group, b_group, w_router, b_router, w_gate, w_up, w_down, final_norm_g):
    bsz, seq, d = x.shape
    assert (seq, d) == (SEQ, D_MODEL)
    t = bsz * seq
    depth = w_in.shape[0]
    tm = 512

    tab_a = _rope_table(A_ROT, A_HEAD_DIM)
    tab_i = _rope_table(IDX_ROT, IDX_DIM)
    tab_i_half = _rope_table(IDX_ROT, IDX_DIM, active_lanes=IDX_DIM)
    tab_b = _rope_table(B_ROPE, B_ROPE)
    tab_b_half = _rope_table(B_ROPE, B_ROPE, active_lanes=B_ROPE)

    tabs = (tab_a, tab_i, tab_i_half, tab_b, tab_b_half)
    w_gate, w_up, w_down = _to_bf16(w_gate), _to_bf16(w_up), _to_bf16(w_down)
    xf = x.reshape(t, d)
    for l in range(depth):
        w = w_in[l]
        w_a = w[:, 0:768].astype(BF16)
        w_i = _pad_cols(w[:, 768:1352], 640).astype(BF16)
        w_b = _pad_cols(w[:, 1352:1800], 512).astype(BF16)
        w_c = jnp.concatenate([w[:, 2312:3336], w[:, 1800:2312]], axis=1).astype(BF16)
        w_g = w[:, 3336:6408].astype(BF16)
        g_attn = attn_norm_g[l]
        w_uq = b_w_uq[l].reshape(B_Q_RANK, B_HEADS, B_NOPE + B_ROPE)
        w_uq = jnp.concatenate([w_uq[:, :, :B_NOPE].reshape(B_Q_RANK, -1),
                                w_uq[:, :, B_NOPE:].reshape(B_Q_RANK, -1)], axis=1).astype(BF16)
        w_ukv = b_w_ukv[l].reshape(B_KV_RANK, B_HEADS, B_NOPE + B_V)
        w_ukv = jnp.concatenate([w_ukv[:, :, :B_NOPE].reshape(B_KV_RANK, -1),
                                 w_ukv[:, :, B_NOPE:].reshape(B_KV_RANK, -1)], axis=1).astype(BF16)

        za, zi, qb, kvb, kr, zc = _in_proj(xf, g_attn, w_a, w_i, w_b, w_c, b_q_norm_g[l], w_uq,
                                           b_kv_norm_g[l], w_ukv, tabs, tm=tm)
        per_seq = lambda z: z.reshape(bsz, seq, -1)
        o_a = _dsa_attention(per_seq(za), per_seq(zi), bsz)
        o_b = _mla_attention(per_seq(qb), per_seq(kvb), per_seq(kr), bsz)
        o_c = _band_attention(per_seq(zc), _band_bias(c_rel_bias[l]), bsz)

        xf = _merge(xf, g_attn, w_g, o_a.reshape(t, -1), o_b.reshape(t, -1), o_c.reshape(t, -1),
                    w_proj_a[l].astype(BF16), w_proj_b[l].astype(BF16), w_proj_c[l].astype(BF16),
                    w_out[l].astype(BF16), tm=tm)

        w_r = _pad_cols(jnp.concatenate([w_router[l], w_group[l]], axis=1), LANES)
        b_r = _pad_cols(jnp.concatenate([b_router[l], b_group[l]])[None, :], LANES)
        xf = _moe(xf, ffn_norm_g[l], w_r, b_r, w_gate, w_up, w_down, l,
                  final_g=final_norm_g if l == depth - 1 else None)

    return xf.reshape(bsz, seq, d)
```

```python
import functools

import numpy as np
import jax
import jax.numpy as jnp
from jax import lax
from jax.experimental import pallas as pl
from jax.experimental.pallas import tpu as pltpu

F32 = jnp.float32
BF16 = jnp.bfloat16

LANES = 128
D_MODEL = 1024
SEQ = 2048
CHUNK = 64
Q_BLOCK = 128
ROPE_THETA = 500000.0
EPS = 1e-6

A_HEADS = 4
A_HEAD_DIM = 128
A_ROT = 32
IDX_HEADS = 8
IDX_DIM = 64
IDX_ROT = 16
TOPK = 256
B_HEADS = 4
B_NOPE = 128
B_ROPE = 64
B_V = 128
B_Q_RANK = 256
B_KV_RANK = 128
C_HEADS = 4
C_HEAD_DIM = 128
C_LEFT_CHUNKS = 8
REL_CLIP = 128
N_GROUPS = 4
EXPERTS_PER_GROUP = 8
N_EXPERTS = 32
D_EXPERT = 256

C_KEY_BLOCKS = C_LEFT_CHUNKS * CHUNK // Q_BLOCK + 1
N_QB = SEQ // Q_BLOCK
KV_VARIANTS = 8
KV_STEP = SEQ // KV_VARIANTS

VMEM_LIMIT = 56 * 1024 * 1024

MOE_TILE = 2048
EXPERT_CHUNK = 256
XS_ROWS = 2 * MOE_TILE + 2 * EXPERT_CHUNK
assert XS_ROWS >= 2 * MOE_TILE + N_EXPERTS * 7 + EXPERT_CHUNK - 1
COMBINE_ROWS = 512
HIGH_HALF = -65536

INT_MIN = -2 ** 31
NEG_INF_KEY = int(np.array(0x807FFFFF, np.uint32).view(np.int32))

NT_DIMS = (((1,), (1,)), ((), ()))


def _nt_dot(a, b):
    return lax.dot_general(a, b, NT_DIMS, preferred_element_type=F32)


def _cparams(*sem):
    return pltpu.CompilerParams(dimension_semantics=sem, vmem_limit_bytes=VMEM_LIMIT)


def _rope_table(rot, period, active_lanes=LANES):
    half = rot // 2
    lane = np.arange(LANES)
    p = lane % period
    first = (p < half) & (lane < active_lanes)
    second = (p >= half) & (p < rot) & (lane < active_lanes)
    idx = np.where(first, p, np.where(second, p - half, 0))
    pos = jnp.arange(SEQ, dtype=F32)
    inv = ROPE_THETA ** (-jnp.arange(0, rot, 2, dtype=F32) / rot)
    ang = pos[:, None] * inv[idx][None, :]
    cos, sin = jnp.cos(ang), jnp.sin(ang)
    c = jnp.where(first | second, cos, 1.0)
    s_prev = jnp.where(second, sin, 0.0)
    s_next = jnp.where(first, -sin, 0.0)
    return jnp.stack([c, s_prev, s_next]).astype(F32)


def _rms(x, g):
    ms = jnp.mean(x * x, axis=-1, keepdims=True)
    return x * lax.rsqrt(ms + EPS) * g


def _rope_tiles(z, tile_tab, tabs, halves):
    out = []
    for c, t in enumerate(tile_tab):
        zt = z[:, c * LANES:(c + 1) * LANES]
        if t >= 0:
            tab, half = tabs[t], halves[t]
            zt = (zt * tab[0] + pltpu.roll(zt, half, 1) * tab[1]
                  + pltpu.roll(zt, LANES - half, 1) * tab[2])
        out.append(zt)
    return out


def _store_tiles(o_ref, tiles):
    for c, zt in enumerate(tiles):
        o_ref[:, c * LANES:(c + 1) * LANES] = zt.astype(o_ref.dtype)


ROPE_HALVES = (A_ROT // 2, IDX_ROT // 2, IDX_ROT // 2, B_ROPE // 2, B_ROPE // 2)


def _in_proj_kernel(x_ref, g_ref, wa_ref, wi_ref, wb_ref, wc_ref, gq_ref, wuq_ref, gkv_ref, wukv_ref,
                    ta_ref, ti_ref, tih_ref, tb_ref, tbh_ref,
                    za_ref, zi_ref, qb_ref, kvb_ref, kr_ref, zc_ref):
    tabs = (ta_ref, ti_ref, tih_ref, tb_ref, tbh_ref)
    rope = functools.partial(_rope_tiles, tabs=tabs, halves=ROPE_HALVES)
    h = _rms(x_ref[...], g_ref[...]).astype(BF16)
    dot = functools.partial(jnp.dot, preferred_element_type=F32)
    _store_tiles(za_ref, rope(dot(h, wa_ref[...]), (0, 0, 0, 0, 0, -1)))
    _store_tiles(zi_ref, rope(dot(h, wi_ref[...]), (1, 1, 1, 1, 2)))
    _store_tiles(zc_ref, rope(dot(h, wc_ref[...]), (-1,) * (3 * C_HEADS)))
    zb = dot(h, wb_ref[...])
    _store_tiles(kr_ref, rope(zb[:, B_Q_RANK + B_KV_RANK:], (4,)))
    cq = _rms(zb[:, :B_Q_RANK], gq_ref[...]).astype(BF16)
    _store_tiles(qb_ref, rope(dot(cq, wuq_ref[...]), (-1, -1, -1, -1, 3, 3)))
    ckv = _rms(zb[:, B_Q_RANK:B_Q_RANK + B_KV_RANK], gkv_ref[...]).astype(BF16)
    _store_tiles(kvb_ref, rope(dot(ckv, wukv_ref[...]), (-1,) * (2 * B_HEADS)))


def _in_proj(x, g, w_a, w_i, w_b, w_c, g_q, w_uq, g_kv, w_ukv, tabs, *, tm):
    t, d = x.shape
    seq_tiles = SEQ // tm
    fixed = lambda i: (0, 0)
    row = lambda i: (i, 0)
    weights = (w_a, w_i, w_b, w_c)
    outs = ((768, BF16), (640, F32), (768, BF16), (1024, BF16), (LANES, BF16), (1536, BF16))
    return pl.pallas_call(
        _in_proj_kernel,
        grid=(t // tm,),
        in_specs=[pl.BlockSpec((tm, d), row), pl.BlockSpec((1, d), fixed)]
        + [pl.BlockSpec(w.shape, fixed) for w in weights]
        + [pl.BlockSpec((1, B_Q_RANK), fixed), pl.BlockSpec(w_uq.shape, fixed),
           pl.BlockSpec((1, B_KV_RANK), fixed), pl.BlockSpec(w_ukv.shape, fixed)]
        + [pl.BlockSpec((3, tm, LANES), lambda i: (0, i % seq_tiles, 0)) for _ in tabs],
        out_specs=[pl.BlockSpec((tm, n), row) for n, _ in outs],
        out_shape=[jax.ShapeDtypeStruct((t, n), dt) for n, dt in outs],
        compiler_params=_cparams("parallel"),
    )(x, g.reshape(1, d), *weights, g_q.reshape(1, -1), w_uq, g_kv.reshape(1, -1), w_ukv, *tabs)


def _chunk_causal_mask(j, n_keys):
    qpos = j * Q_BLOCK + lax.broadcasted_iota(jnp.int32, (Q_BLOCK, n_keys), 0)
    kpos = lax.broadcasted_iota(jnp.int32, (Q_BLOCK, n_keys), 1)
    return (kpos >> 6) <= (qpos >> 6)


SEARCH_ROWS = 256
HALF_RANGE = 1 << 15


def _dsa_select(iq_ref, ikw_ref, key_ref, hi_ref, lo_ref, bias_ref, n_keys):
    j = pl.program_id(1)
    n_blk = n_keys // Q_BLOCK
    n_grp = n_keys // SEARCH_ROWS
    n_pack = n_keys // 16
    row0 = pl.multiple_of(j * Q_BLOCK, Q_BLOCK)
    sub = lax.broadcasted_iota(jnp.int32, (SEARCH_ROWS, Q_BLOCK), 0)
    lane = lax.broadcasted_iota(jnp.int32, (SEARCH_ROWS, Q_BLOCK), 1)
    q_chunk = (row0 + lane) >> 6

    iq = iq_ref[...].astype(BF16)
    iq_stack = jnp.concatenate([iq[:, h * IDX_DIM:(h + 1) * IDX_DIM] for h in range(IDX_HEADS)], axis=0)
    iw_t = ikw_ref[pl.ds(row0, Q_BLOCK), :].T * (IDX_HEADS ** -0.5)

    for g in range(n_grp):
        k0 = g * SEARCH_ROWS
        ik = ikw_ref[k0:k0 + SEARCH_ROWS, 0:IDX_DIM].astype(BF16)
        score = jnp.zeros((SEARCH_ROWS, Q_BLOCK), F32)
        for hp in range(IDX_HEADS // 2):
            r = _nt_dot(ik, iq_stack[hp * 2 * Q_BLOCK:(hp + 1) * 2 * Q_BLOCK])
            for u in range(2):
                h = 2 * hp + u
                rel = jnp.maximum(r[:, u * Q_BLOCK:(u + 1) * Q_BLOCK] * (IDX_DIM ** -0.5), 0.0)
                score = score + rel * iw_t[IDX_DIM + h:IDX_DIM + h + 1, :]
        allowed = ((k0 + sub) >> 6) <= q_chunk
        score = jnp.where(score == 0.0, 0.0, score)
        score = jnp.where(allowed, score, -jnp.inf)
        bits = lax.bitcast_convert_type(score, jnp.int32)
        key = bits ^ ((bits >> 31) & 0x7FFFFFFF)
        key_ref[k0:k0 + SEARCH_ROWS, :] = key
        hi_ref[k0:k0 + SEARCH_ROWS, :] = (key >> 16).astype(jnp.int16)
        lo_ref[k0:k0 + SEARCH_ROWS, :] = ((key & 0xFFFF) - HALF_RANGE).astype(jnp.int16)

    def count(pred):
        accs = [jnp.zeros((8, Q_BLOCK), F32)] * 4
        for r in range(n_keys // 8):
            accs[r % 4] = accs[r % 4] + jnp.where(pred(key_ref[8 * r:8 * (r + 1), :]), 1.0, 0.0)
        return jnp.sum(accs[0] + accs[1] + accs[2] + accs[3], axis=0, keepdims=True)

    one16 = jnp.ones((16, Q_BLOCK), jnp.int16)
    zero16 = jnp.zeros((16, Q_BLOCK), jnp.int16)

    def search16(ref):
        def rnd(i, base):
            cand = base + jnp.left_shift(jnp.int32(1), 15 - i)
            cand16 = jnp.broadcast_to(cand, (16, Q_BLOCK)).astype(jnp.int16)
            accs = [zero16] * 4
            for r in range(n_pack):
                accs[r % 4] = accs[r % 4] + jnp.where(ref[16 * r:16 * (r + 1), :] >= cand16, one16, zero16)
            total = (accs[0] + accs[1]) + (accs[2] + accs[3])
            cnt = jnp.sum(total.astype(jnp.int32), axis=0, keepdims=True)
            return jnp.where(cnt >= TOPK, cand, base)
        return lax.fori_loop(0, 16, rnd, jnp.full((1, Q_BLOCK), -HALF_RANGE, jnp.int32))

    top = search16(hi_ref)
    top16 = jnp.broadcast_to(top, (16, Q_BLOCK)).astype(jnp.int16)
    for r in range(n_pack):
        rows = slice(16 * r, 16 * (r + 1))
        hi = hi_ref[rows, :]
        lo_ref[rows, :] = jnp.where(hi > top16, jnp.int16(HALF_RANGE - 1),
                                    jnp.where(hi == top16, lo_ref[rows, :], jnp.int16(-HALF_RANGE)))
    thr = (top << 16) | (search16(lo_ref) + HALF_RANGE)
    thr8 = jnp.broadcast_to(thr, (8, Q_BLOCK))
    cnt_gt = count(lambda kk: kk > thr8)
    cnt_ge = count(lambda kk: kk >= thr8)
    tie_cols = (cnt_ge > TOPK) & (thr > NEG_INF_KEY)
    has_tie = jnp.max(jnp.where(tie_cols, 1.0, 0.0)) > 0.0

    for kb in range(n_blk):
        kk = key_ref[kb * Q_BLOCK:(kb + 1) * Q_BLOCK, :]
        bias_ref[kb * Q_BLOCK:(kb + 1) * Q_BLOCK, :] = jnp.where((kk >= thr) & (kk > NEG_INF_KEY), 0.0, -jnp.inf).T

    @pl.when(has_tie)
    def _():
        need = TOPK - cnt_gt
        row = lax.broadcasted_iota(jnp.int32, (Q_BLOCK, Q_BLOCK), 0)
        col = lax.broadcasted_iota(jnp.int32, (Q_BLOCK, Q_BLOCK), 1)
        lower = jnp.where(col < row, 1.0, 0.0).astype(BF16)

        def tie_block(kb, seen):
            k0 = pl.multiple_of(kb * Q_BLOCK, Q_BLOCK)
            kk = key_ref[pl.ds(k0, Q_BLOCK), :]
            eq = jnp.where(kk == thr, 1.0, 0.0)
            before = jnp.dot(lower, eq.astype(BF16), preferred_element_type=F32) + seen
            keep = (kk > thr) | ((kk == thr) & (before < need))
            bias_ref[pl.ds(k0, Q_BLOCK), :] = jnp.where(keep & (kk > NEG_INF_KEY), 0.0, -jnp.inf).T
            return seen + jnp.sum(eq, axis=0, keepdims=True)

        lax.fori_loop(0, n_blk, tie_block, jnp.zeros((1, Q_BLOCK), F32))


def _dsa_attend(q_ref, kv_ref, bias_ref, o_ref, n_keys):
    q = q_ref[...]
    k = kv_ref[0:n_keys, 0:A_HEAD_DIM]
    v = kv_ref[0:n_keys, A_HEAD_DIM:2 * A_HEAD_DIM]
    bias = jnp.concatenate([bias_ref[c * Q_BLOCK:(c + 1) * Q_BLOCK, :] for c in range(n_keys // Q_BLOCK)],
                           axis=1)
    for h in range(A_HEADS):
        logits = _nt_dot(q[:, h * A_HEAD_DIM:(h + 1) * A_HEAD_DIM], k) * (A_HEAD_DIM ** -0.5) + bias
        m = jnp.max(logits, axis=-1, keepdims=True)
        p = jnp.exp(logits - m)
        l = jnp.sum(p, axis=-1, keepdims=True)
        o = jnp.dot(p.astype(BF16), v, preferred_element_type=F32) / l
        o_ref[:, h * A_HEAD_DIM:(h + 1) * A_HEAD_DIM] = o.astype(o_ref.dtype)


def _dsa_kernel(q_ref, kv_ref, iq_ref, ikw_ref, o_ref, key_ref, hi_ref, lo_ref, bias_ref):
    j = pl.program_id(1)
    for v in range(KV_VARIANTS):
        @pl.when(j // (N_QB // KV_VARIANTS) == v)
        def _(v=v):
            n_keys = KV_STEP * (v + 1)
            _dsa_select(iq_ref, ikw_ref, key_ref, hi_ref, lo_ref, bias_ref, n_keys)
            _dsa_attend(q_ref, kv_ref, bias_ref, o_ref, n_keys)


def _dsa_attention(za, zi, bsz):
    return pl.pallas_call(
        _dsa_kernel,
        grid=(bsz, N_QB),
        in_specs=[
            pl.BlockSpec((None, Q_BLOCK, 512), lambda b, j: (b, j, 0)),
            pl.BlockSpec((None, SEQ, 256), lambda b, j: (b, 0, 2)),
            pl.BlockSpec((None, Q_BLOCK, 512), lambda b, j: (b, j, 0)),
            pl.BlockSpec((None, SEQ, LANES), lambda b, j: (b, 0, 4)),
        ],
        out_specs=pl.BlockSpec((None, Q_BLOCK, 512), lambda b, j: (b, j, 0)),
        out_shape=jax.ShapeDtypeStruct((bsz, SEQ, 512), BF16),
        scratch_shapes=[pltpu.VMEM((SEQ, Q_BLOCK), jnp.int32),
                        pltpu.VMEM((SEQ, Q_BLOCK), jnp.int16),
                        pltpu.VMEM((SEQ, Q_BLOCK), jnp.int16),
                        pltpu.VMEM((SEQ, Q_BLOCK), F32)],
        compiler_params=_cparams("parallel", "arbitrary"),
    )(za, za, zi, zi)


def _mla_body(q_ref, kv_ref, kr_ref, o_ref, n_keys):
    j = pl.program_id(1)
    mask = _chunk_causal_mask(j, n_keys)
    kr = kr_ref[0:n_keys, 0:B_ROPE]
    scale = (B_NOPE + B_ROPE) ** -0.5
    for h in range(B_HEADS):
        qcat = jnp.concatenate(
            [q_ref[:, h * B_NOPE:(h + 1) * B_NOPE],
             q_ref[:, B_HEADS * B_NOPE + h * B_ROPE:B_HEADS * B_NOPE + (h + 1) * B_ROPE]], axis=1)
        kcat = jnp.concatenate([kv_ref[0:n_keys, h * B_NOPE:(h + 1) * B_NOPE], kr], axis=1)
        vv = kv_ref[0:n_keys, B_HEADS * B_NOPE + h * B_V:B_HEADS * B_NOPE + (h + 1) * B_V]
        s = _nt_dot(qcat, kcat) * scale
        s = jnp.where(mask, s, -jnp.inf)
        m = jnp.max(s, axis=-1, keepdims=True)
        p = jnp.exp(s - m)
        l = jnp.sum(p, axis=-1, keepdims=True)
        o = jnp.dot(p.astype(BF16), vv, preferred_element_type=F32) / l
        o_ref[:, h * B_V:(h + 1) * B_V] = o.astype(o_ref.dtype)


def _mla_kernel(q_ref, kv_ref, kr_ref, o_ref):
    j = pl.program_id(1)
    for v in range(KV_VARIANTS):
        @pl.when(j // (N_QB // KV_VARIANTS) == v)
        def _(v=v):
            _mla_body(q_ref, kv_ref, kr_ref, o_ref, KV_STEP * (v + 1))


def _mla_attention(qb, kvb, kr, bsz):
    return pl.pallas_call(
        _mla_kernel,
        grid=(bsz, N_QB),
        in_specs=[
            pl.BlockSpec((None, Q_BLOCK, 768), lambda b, j: (b, j, 0)),
            pl.BlockSpec((None, SEQ, 1024), lambda b, j: (b, 0, 0)),
            pl.BlockSpec((None, SEQ, LANES), lambda b, j: (b, 0, 0)),
        ],
        out_specs=pl.BlockSpec((None, Q_BLOCK, 512), lambda b, j: (b, j, 0)),
        out_shape=jax.ShapeDtypeStruct((bsz, SEQ, 512), BF16),
        compiler_params=_cparams("parallel", "arbitrary"),
    )(qb, kvb, kr)


def _band_bias(rel_table):
    n = 2 * REL_CLIP + 1
    period = 2 * n - 1
    heads = rel_table.shape[0]
    ext = jnp.concatenate([rel_table, jnp.broadcast_to(rel_table[:, n - 1:n], (heads, n - 1))], axis=1)
    kj = np.arange(Q_BLOCK)[:, None]
    qi = np.arange(Q_BLOCK)[None, :]
    out = []
    for d in range(C_KEY_BLOCKS):
        base = d * Q_BLOCK + REL_CLIP
        if base - (Q_BLOCK - 1) >= n - 1:
            bias = jnp.broadcast_to(rel_table[:, n - 1][:, None, None], (heads, Q_BLOCK, Q_BLOCK))
        else:
            shifted = jnp.roll(ext, -base, axis=1)
            bias = jnp.tile(shifted, (1, Q_BLOCK))[:, :Q_BLOCK * (period - 1)]
            bias = bias.reshape(heads, Q_BLOCK, period - 1)[:, :, :Q_BLOCK]
        cdiff = 2 * d + qi // CHUNK - kj // CHUNK
        valid = (cdiff >= 0) & (cdiff <= C_LEFT_CHUNKS)
        out.append(jnp.where(valid[None], bias.astype(F32), -jnp.inf))
    return jnp.swapaxes(jnp.stack(out, axis=1), 2, 3)


BAND_QB = 2


def _band_kernel(q_ref, kv_ref, bias_ref, o_ref):
    scale = C_HEAD_DIM ** -0.5
    for sb in range(BAND_QB):
        j = pl.program_id(1) * BAND_QB + sb
        rows = slice(sb * Q_BLOCK, (sb + 1) * Q_BLOCK)
        for h in range(C_HEADS):
            q = q_ref[rows, h * C_HEAD_DIM:(h + 1) * C_HEAD_DIM]
            ss, vs = [], []
            for d in range(C_KEY_BLOCKS):
                kb = j - d
                row0 = pl.multiple_of(jnp.maximum(kb, 0) * Q_BLOCK, Q_BLOCK)
                k = kv_ref[pl.ds(row0, Q_BLOCK), h * C_HEAD_DIM:(h + 1) * C_HEAD_DIM]
                vs.append(kv_ref[pl.ds(row0, Q_BLOCK), (C_HEADS + h) * C_HEAD_DIM:(C_HEADS + h + 1) * C_HEAD_DIM])
                s = _nt_dot(q, k) * scale + bias_ref[h, d]
                ss.append(jnp.where(kb >= 0, s, -jnp.inf))
            s_all = jnp.concatenate(ss, axis=1)
            v_all = jnp.concatenate(vs, axis=0)
            m = jnp.max(s_all, axis=-1, keepdims=True)
            p = jnp.exp(s_all - m)
            l = jnp.sum(p, axis=-1, keepdims=True)
            o = jnp.dot(p.astype(BF16), v_all, preferred_element_type=F32) / l
            o_ref[rows, h * C_HEAD_DIM:(h + 1) * C_HEAD_DIM] = o.astype(o_ref.dtype)


def _band_attention(zc, bias, bsz):
    rows = BAND_QB * Q_BLOCK
    return pl.pallas_call(
        _band_kernel,
        grid=(bsz, SEQ // rows),
        in_specs=[
            pl.BlockSpec((None, rows, 512), lambda b, j: (b, j, 2)),
            pl.BlockSpec((None, SEQ, 1024), lambda b, j: (b, 0, 0)),
            pl.BlockSpec((C_HEADS, C_KEY_BLOCKS, Q_BLOCK, Q_BLOCK), lambda b, j: (0, 0, 0, 0)),
        ],
        out_specs=pl.BlockSpec((None, rows, 512), lambda b, j: (b, j, 0)),
        out_shape=jax.ShapeDtypeStruct((bsz, SEQ, 512), BF16),
        compiler_params=_cparams("parallel", "arbitrary"),
    )(zc, zc, bias)


def _merge_kernel(x_ref, g_ref, wgl_ref, oa_ref, ob_ref, oc_ref, wpa_ref, wpb_ref, wpc_ref, wout_ref, o_ref):
    x = x_ref[...]
    ms = jnp.mean(x * x, axis=-1, keepdims=True)
    h = (x * lax.rsqrt(ms + EPS) * g_ref[...]).astype(BF16)
    mix = jnp.zeros(x.shape, F32)
    for i, (o_in, wp) in enumerate(((oa_ref, wpa_ref), (ob_ref, wpb_ref), (oc_ref, wpc_ref))):
        gl = jnp.dot(h, wgl_ref[:, i * D_MODEL:(i + 1) * D_MODEL], preferred_element_type=F32)
        gate = jax.nn.sigmoid(gl)
        mix = mix + gate * jnp.dot(o_in[...], wp[...], preferred_element_type=F32)
    o_ref[...] = x + jnp.dot(mix.astype(BF16), wout_ref[...], preferred_element_type=F32)


def _merge(x, g, wgl, oa, ob, oc, wpa, wpb, wpc, wout, *, tm):
    t, d = x.shape
    row = lambda i: (i, 0)
    fixed = lambda i: (0, 0)
    return pl.pallas_call(
        _merge_kernel,
        grid=(t // tm,),
        in_specs=[
            pl.BlockSpec((tm, d), row),
            pl.BlockSpec((1, d), fixed),
            pl.BlockSpec((d, 3 * d), fixed),
            pl.BlockSpec((tm, 512), row),
            pl.BlockSpec((tm, 512), row),
            pl.BlockSpec((tm, 512), row),
            pl.BlockSpec((512, d), fixed),
            pl.BlockSpec((512, d), fixed),
            pl.BlockSpec((512, d), fixed),
            pl.BlockSpec((d, d), fixed),
        ],
        out_specs=pl.BlockSpec((tm, d), row),
        out_shape=jax.ShapeDtypeStruct((t, d), F32),
        compiler_params=_cparams("parallel"),
    )(x, g.reshape(1, d), wgl, oa, ob, oc, wpa, wpb, wpc, wout)


def _first_argmax(vals, lane):
    m = jnp.max(vals, axis=-1, keepdims=True)
    idx = jnp.min(jnp.where(vals == m, lane, LANES), axis=-1, keepdims=True)
    return m, idx


def _pack_bf16_pairs(h):
    n = h.shape[1] // 2
    bits = lax.bitcast_convert_type(h.astype(jnp.bfloat16).astype(F32), jnp.int32)
    return lax.shift_right_logical(bits[:, :n], 16) | bits[:, n:]


def _unpack_bf16_pairs(w):
    lo = lax.bitcast_convert_type(w << 16, F32).astype(BF16)
    hi = lax.bitcast_convert_type(w & HIGH_HALF, F32).astype(BF16)
    return lo, hi


ROUTER_ROWS = 256


def _router_kernel(x_ref, g_ref, w_ref, b_ref, hp_ref, meta_ref, seg_ref, sel_ref):
    lane = lax.broadcasted_iota(jnp.int32, (ROUTER_ROWS, LANES), 1)
    is_grp = (lane >= N_EXPERTS) & (lane < N_EXPERTS + N_GROUPS)
    a = lax.broadcasted_iota(jnp.int32, (LANES, LANES), 0)
    b = lax.broadcasted_iota(jnp.int32, (LANES, LANES), 1)
    lower = jnp.where(b < a, 1.0, 0.0).astype(BF16)

    def route(c, carry):
        r0 = pl.multiple_of(c * ROUTER_ROWS, ROUTER_ROWS)
        rows = pl.ds(r0, ROUTER_ROWS)
        h = _rms(x_ref[rows, :], g_ref[...])
        hp_ref[rows, :] = _pack_bf16_pairs(h)
        logits = jnp.dot(h, w_ref[...], preferred_element_type=F32, precision=lax.Precision.HIGHEST) + b_ref[...]
        gl = jnp.where(is_grp, logits, -jnp.inf)
        gmax, gidx = _first_argmax(gl, lane)
        pg = 1.0 / jnp.sum(jnp.exp(gl - gmax), axis=-1, keepdims=True)
        gsel = gidx - N_EXPERTS
        in_grp = (lane >> 3) == gsel
        el = jnp.where(in_grp, logits, -jnp.inf)
        m1, i1 = _first_argmax(el, lane)
        z = jnp.sum(jnp.exp(el - m1), axis=-1, keepdims=True)
        el2 = jnp.where(lane == i1, -jnp.inf, el)
        m2, i2 = _first_argmax(el2, lane)
        pe1 = 1.0 / z
        pe2 = jnp.exp(m2 - m1) / z
        den = pe1 + pe2
        w1 = pg * pe1 / den
        w2 = pg * pe2 / den
        sel1 = lane == i1
        sel2 = lane == i2
        onehot = jnp.where(sel1 | sel2, 1.0, 0.0)
        ranks = []
        for k in range(ROUTER_ROWS // LANES):
            blk = onehot[k * LANES:(k + 1) * LANES]
            ranks.append(jnp.dot(lower, blk.astype(BF16), preferred_element_type=F32) + carry)
            carry = carry + jnp.sum(blk, axis=0, keepdims=True)
        rank = jnp.concatenate(ranks, axis=0)
        rank1 = jnp.sum(jnp.where(sel1, rank, 0.0), axis=-1, keepdims=True)
        rank2 = jnp.sum(jnp.where(sel2, rank, 0.0), axis=-1, keepdims=True)
        meta_ref[rows, :] = (jnp.where(lane == 0, rank1, 0.0) + jnp.where(lane == 1, rank2, 0.0)
                             + jnp.where(lane == 2, w1, 0.0) + jnp.where(lane == 3, w2, 0.0))
        sel_ref[rows, :] = jnp.where(lane == 0, i1, jnp.where(lane == 1, i2, 0))
        return carry

    counts = lax.fori_loop(0, MOE_TILE // ROUTER_ROWS, route, jnp.zeros((1, LANES), F32))

    cnt = jnp.broadcast_to(counts, (8, LANES))
    seg = jnp.floor((cnt + 7.0) * 0.125) * 8.0
    lane8 = lax.broadcasted_iota(jnp.int32, (8, LANES), 1)
    scan = seg
    for k in (1, 2, 4, 8, 16, 32, 64):
        scan = scan + jnp.where(lane8 >= k, pltpu.roll(scan, k, 1), 0.0)
    off = scan - seg
    row8 = lax.broadcasted_iota(jnp.int32, (8, LANES), 0)
    seg_ref[...] = jnp.where(row8 == 0, cnt, jnp.where(row8 == 1, off, 0.0)).astype(jnp.int32)

    def place(c, carry):
        r0 = pl.multiple_of(c * ROUTER_ROWS, ROUTER_ROWS)
        rows = pl.ds(r0, ROUTER_ROWS)
        sel = sel_ref[rows, :]
        off1 = jnp.sum(jnp.where(lane == sel[:, 0:1], off[0:1], 0.0), axis=-1, keepdims=True)
        off2 = jnp.sum(jnp.where(lane == sel[:, 1:2], off[0:1], 0.0), axis=-1, keepdims=True)
        meta_ref[rows, :] = meta_ref[rows, :] + jnp.where(lane == 0, off1, 0.0) + jnp.where(lane == 1, off2, 0.0)
        return carry

    lax.fori_loop(0, MOE_TILE // ROUTER_ROWS, place, 0)


def _router(x, g, w, b):
    t, d = x.shape
    nt = t // MOE_TILE
    return pl.pallas_call(
        _router_kernel,
        grid=(nt,),
        in_specs=[
            pl.BlockSpec((MOE_TILE, d), lambda i: (i, 0)),
            pl.BlockSpec((1, d), lambda i: (0, 0)),
            pl.BlockSpec((d, LANES), lambda i: (0, 0)),
            pl.BlockSpec((1, LANES), lambda i: (0, 0)),
        ],
        out_specs=[
            pl.BlockSpec((MOE_TILE, d // 2), lambda i: (i, 0)),
            pl.BlockSpec((MOE_TILE, LANES), lambda i: (i, 0)),
            pl.BlockSpec((None, 8, LANES), lambda i: (i, 0, 0)),
        ],
        out_shape=[
            jax.ShapeDtypeStruct((t, d // 2), jnp.int32),
            jax.ShapeDtypeStruct((t, LANES), F32),
            jax.ShapeDtypeStruct((nt, 8, LANES), jnp.int32),
        ],
        scratch_shapes=[pltpu.VMEM((MOE_TILE, LANES), jnp.int32)],
        compiler_params=_cparams("parallel"),
    )(x, g.reshape(1, d), w, b)


def _scatter_kernel(pos_ref, hp_ref, xs_ref):
    xs_ref[...] = jnp.zeros_like(xs_ref)

    def body(t, carry):
        row = hp_ref[pl.ds(t, 1), :]
        xs_ref[pl.ds(pos_ref[0, t], 1), :] = row
        xs_ref[pl.ds(pos_ref[0, MOE_TILE + t], 1), :] = row
        return carry

    lax.fori_loop(0, MOE_TILE, body, 0, unroll=8)


def _scatter_rows(pos, hp):
    nt = pos.shape[0]
    return pl.pallas_call(
        _scatter_kernel,
        grid=(nt,),
        in_specs=[
            pl.BlockSpec((None, 1, 2 * MOE_TILE), lambda i: (i, 0, 0), memory_space=pltpu.SMEM),
            pl.BlockSpec((MOE_TILE, hp.shape[1]), lambda i: (i, 0)),
        ],
        out_specs=pl.BlockSpec((None, XS_ROWS, hp.shape[1]), lambda i: (i, 0, 0)),
        out_shape=jax.ShapeDtypeStruct((nt, XS_ROWS, hp.shape[1]), jnp.int32),
        compiler_params=_cparams("parallel"),
    )(pos, hp)


CAST_EXPERTS = 4


def _cast_kernel(w_ref, o_ref):
    o_ref[...] = w_ref[...].astype(o_ref.dtype)


def _to_bf16(w):
    n_l, n_e, a, b = w.shape
    spec = pl.BlockSpec((None, CAST_EXPERTS, a, b), lambda l, e: (l, e, 0, 0))
    return pl.pallas_call(
        _cast_kernel,
        grid=(n_l, n_e // CAST_EXPERTS),
        in_specs=[spec],
        out_specs=spec,
        out_shape=jax.ShapeDtypeStruct(w.shape, BF16),
        compiler_params=_cparams("parallel", "parallel"),
    )(w)


def _experts_kernel(cnt_ref, off_ref, xs_ref, wg_ref, wu_ref, wd_ref, ys_ref):
    i = pl.program_id(0)
    e = pl.program_id(1)
    half = D_MODEL // 2

    @pl.when(e == 0)
    def _():
        ys_ref[2 * MOE_TILE:XS_ROWS, :] = jnp.zeros((XS_ROWS - 2 * MOE_TILE, D_MODEL), F32)

    n = cnt_ref[i, e]
    off = off_ref[i, e]

    def body(c, carry):
        start = pl.multiple_of(off + c * EXPERT_CHUNK, 8)
        lo, hi = _unpack_bf16_pairs(xs_ref[pl.ds(start, EXPERT_CHUNK), :])
        a = (jnp.dot(lo, wg_ref[0:half], preferred_element_type=F32)
             + jnp.dot(hi, wg_ref[half:D_MODEL], preferred_element_type=F32))
        u = (jnp.dot(lo, wu_ref[0:half], preferred_element_type=F32)
             + jnp.dot(hi, wu_ref[half:D_MODEL], preferred_element_type=F32))
        hh = (a * jax.nn.sigmoid(a)) * u
        ys_ref[pl.ds(start, EXPERT_CHUNK), :] = jnp.dot(hh.astype(BF16), wd_ref[...], preferred_element_type=F32)
        return carry

    lax.fori_loop(0, (n + EXPERT_CHUNK - 1) // EXPERT_CHUNK, body, 0)


def _experts(cnt, off, xs, wg, wu, wd, layer):
    nt = xs.shape[0]
    d = D_MODEL
    grid_spec = pltpu.PrefetchScalarGridSpec(
        num_scalar_prefetch=2,
        grid=(nt, N_EXPERTS),
        in_specs=[
            pl.BlockSpec((None, XS_ROWS, d // 2), lambda i, e, c, o: (i, 0, 0), pipeline_mode=pl.Buffered(1)),
            pl.BlockSpec((None, None, d, D_EXPERT), lambda i, e, c, o: (layer, e, 0, 0)),
            pl.BlockSpec((None, None, d, D_EXPERT), lambda i, e, c, o: (layer, e, 0, 0)),
            pl.BlockSpec((None, None, D_EXPERT, d), lambda i, e, c, o: (layer, e, 0, 0)),
        ],
        out_specs=pl.BlockSpec((None, XS_ROWS, d), lambda i, e, c, o: (i, 0, 0)),
    )
    return pl.pallas_call(
        _experts_kernel,
        grid_spec=grid_spec,
        out_shape=jax.ShapeDtypeStruct((nt, XS_ROWS, d), F32),
        compiler_params=_cparams("parallel", "arbitrary"),
    )(cnt, off, xs, wg, wu, wd)


def _combine_kernel(pos_ref, wt_ref, x_ref, ys_ref, g_ref, o_ref, *, final_norm):
    s = pl.program_id(1)

    def body(tl, carry):
        t = s * COMBINE_ROWS + tl
        y = (ys_ref[pl.ds(pos_ref[0, t], 1), :] * wt_ref[0, t]
             + ys_ref[pl.ds(pos_ref[0, MOE_TILE + t], 1), :] * wt_ref[0, MOE_TILE + t])
        o_ref[pl.ds(tl, 1), :] = x_ref[pl.ds(tl, 1), :] + y
        return carry

    lax.fori_loop(0, COMBINE_ROWS, body, 0, unroll=8)
    if final_norm:
        o_ref[...] = _rms(o_ref[...], g_ref[...])


def _combine(pos, wt, x, ys, final_g):
    t, d = x.shape
    nt = pos.shape[0]
    sub = MOE_TILE // COMBINE_ROWS
    g = jnp.ones((d,), F32) if final_g is None else final_g
    return pl.pallas_call(
        functools.partial(_combine_kernel, final_norm=final_g is not None),
        grid=(nt, sub),
        in_specs=[
            pl.BlockSpec((None, 1, 2 * MOE_TILE), lambda i, s: (i, 0, 0), memory_space=pltpu.SMEM),
            pl.BlockSpec((None, 1, 2 * MOE_TILE), lambda i, s: (i, 0, 0), memory_space=pltpu.SMEM),
            pl.BlockSpec((COMBINE_ROWS, d), lambda i, s: (i * sub + s, 0)),
            pl.BlockSpec((None, XS_ROWS, d), lambda i, s: (i, 0, 0)),
            pl.BlockSpec((1, d), lambda i, s: (0, 0)),
        ],
        out_specs=pl.BlockSpec((COMBINE_ROWS, d), lambda i, s: (i * sub + s, 0)),
        out_shape=jax.ShapeDtypeStruct((t, d), F32),
        compiler_params=_cparams("parallel", "arbitrary"),
    )(pos, wt, x, ys, g.reshape(1, d))


def _moe(x, g, w_r, b_r, wg, wu, wd, layer, final_g=None):
    t = x.shape[0]
    nt = t // MOE_TILE
    hp, meta, seg = _router(x, g, w_r, b_r)
    pair_major = lambda m: m.reshape(nt, MOE_TILE, 2).transpose(0, 2, 1).reshape(nt, 1, 2 * MOE_TILE)
    pos = pair_major(meta[:, 0:2].astype(jnp.int32))
    wt = pair_major(meta[:, 2:4])
    xs = _scatter_rows(pos, hp)
    ys = _experts(seg[:, 0, :N_EXPERTS], seg[:, 1, :N_EXPERTS], xs, wg, wu, wd, layer)
    return _combine(pos, wt, x, ys, final_g)


def _pad_cols(w, n):
    return jnp.pad(w, ((0, 0), (0, n - w.shape[1])))


def kernel(x, attn_norm_g, w_in, b_q_norm_g, b_w_uq, b_kv_norm_g, b_w_ukv, c_rel_bias, w_proj_a, w_proj_b, w_proj_c, w_out, ffn_norm_g, w_group, b_group, w_router, b_router, w_gate, w_up, w_down, final_norm_g):
    bsz, seq, d = x.shape
    assert (seq, d) == (SEQ, D_MODEL)
    t = bsz * seq
    depth = w_in.shape[0]
    tm = 512

    tab_a = _rope_table(A_ROT, A_HEAD_DIM)
    tab_i = _rope_table(IDX_ROT, IDX_DIM)
    tab_i_half = _rope_table(IDX_ROT, IDX_DIM, active_lanes=IDX_DIM)
    tab_b = _rope_table(B_ROPE, B_ROPE)
    tab_b_half = _rope_table(B_ROPE, B_ROPE, active_lanes=B_ROPE)

    tabs = (tab_a, tab_i, tab_i_half, tab_b, tab_b_half)
    w_gate, w_up, w_down = _to_bf16(w_gate), _to_bf16(w_up), _to_bf16(w_down)
    xf = x.reshape(t, d)
    for l in range(depth):
        w = w_in[l]
        w_a = w[:, 0:768].astype(BF16)
        w_i = _pad_cols(w[:, 768:1352], 640).astype(BF16)
        w_b = _pad_cols(w[:, 1352:1800], 512).astype(BF16)
        w_c = jnp.concatenate([w[:, 2312:3336], w[:, 1800:2312]], axis=1).astype(BF16)
        w_g = w[:, 3336:6408].astype(BF16)
        g_attn = attn_norm_g[l]
        w_uq = b_w_uq[l].reshape(B_Q_RANK, B_HEADS, B_NOPE + B_ROPE)
        w_uq = jnp.concatenate([w_uq[:, :, :B_NOPE].reshape(B_Q_RANK, -1),
                                w_uq[:, :, B_NOPE:].reshape(B_Q_RANK, -1)], axis=1).astype(BF16)
        w_ukv = b_w_ukv[l].reshape(B_KV_RANK, B_HEADS, B_NOPE + B_V)
        w_ukv = jnp.concatenate([w_ukv[:, :, :B_NOPE].reshape(B_KV_RANK, -1),
                                 w_ukv[:, :, B_NOPE:].reshape(B_KV_RANK, -1)], axis=1).astype(BF16)

        za, zi, qb, kvb, kr, zc = _in_proj(xf, g_attn, w_a, w_i, w_b, w_c, b_q_norm_g[l], w_uq,
                                           b_kv_norm_g[l], w_ukv, tabs, tm=tm)
        per_seq = lambda z: z.reshape(bsz, seq, -1)
        o_a = _dsa_attention(per_seq(za), per_seq(zi), bsz)
        o_b = _mla_attention(per_seq(qb), per_seq(kvb), per_seq(kr), bsz)
        o_c = _band_attention(per_seq(zc), _band_bias(c_rel_bias[l]), bsz)

        xf = _merge(xf, g_attn, w_g, o_a.reshape(t, -1), o_b.reshape(t, -1), o_c.reshape(t, -1),
                    w_proj_a[l].astype(BF16), w_proj_b[l].astype(BF16), w_proj_c[l].astype(BF16),
                    w_out[l].astype(BF16), tm=tm)

        w_r = _pad_cols(jnp.concatenate([w_router[l], w_group[l]], axis=1), LANES)
        b_r = _pad_cols(jnp.concatenate([b_router[l], b_group[l]])[None, :], LANES)
        xf = _moe(xf, ffn_norm_g[l], w_r, b_r, w_gate, w_up, w_down, l,
                  final_g=final_norm_g if l == depth - 1 else None)

    return xf.reshape(bsz, seq, d)
```

```python
import functools

import numpy as np
import jax
import jax.numpy as jnp
from jax import lax
from jax.experimental import pallas as pl
from jax.experimental.pallas import tpu as pltpu

F32 = jnp.float32
BF16 = jnp.bfloat16

LANES = 128
D_MODEL = 1024
SEQ = 2048
CHUNK = 64
Q_BLOCK = 128
ROPE_THETA = 500000.0
EPS = 1e-6

A_HEADS = 4
A_HEAD_DIM = 128
A_ROT = 32
IDX_HEADS = 8
IDX_DIM = 64
IDX_ROT = 16
TOPK = 256
B_HEADS = 4
B_NOPE = 128
B_ROPE = 64
B_V = 128
B_Q_RANK = 256
B_KV_RANK = 128
C_HEADS = 4
C_HEAD_DIM = 128
C_LEFT_CHUNKS = 8
REL_CLIP = 128
N_GROUPS = 4
EXPERTS_PER_GROUP = 8
N_EXPERTS = 32
D_EXPERT = 256

C_KEY_BLOCKS = C_LEFT_CHUNKS * CHUNK // Q_BLOCK + 1
N_QB = SEQ // Q_BLOCK
KV_VARIANTS = 16
KV_STEP = SEQ // KV_VARIANTS

VMEM_LIMIT = 56 * 1024 * 1024

MOE_TILE = 2048
EXPERT_CHUNK = 256
XS_ROWS = 2 * MOE_TILE + 2 * EXPERT_CHUNK
assert XS_ROWS >= 2 * MOE_TILE + N_EXPERTS * 7 + EXPERT_CHUNK - 1
COMBINE_ROWS = 512
HIGH_HALF = -65536

INT_MIN = -2 ** 31
NEG_INF_KEY = int(np.array(0x807FFFFF, np.uint32).view(np.int32))

NT_DIMS = (((1,), (1,)), ((), ()))


def _nt_dot(a, b):
    return lax.dot_general(a, b, NT_DIMS, preferred_element_type=F32)


def _cparams(*sem):
    return pltpu.CompilerParams(dimension_semantics=sem, vmem_limit_bytes=VMEM_LIMIT)


def _rope_table(rot, period, active_lanes=LANES):
    half = rot // 2
    lane = np.arange(LANES)
    p = lane % period
    first = (p < half) & (lane < active_lanes)
    second = (p >= half) & (p < rot) & (lane < active_lanes)
    idx = np.where(first, p, np.where(second, p - half, 0))
    pos = jnp.arange(SEQ, dtype=F32)
    inv = ROPE_THETA ** (-jnp.arange(0, rot, 2, dtype=F32) / rot)
    ang = pos[:, None] * inv[idx][None, :]
    cos, sin = jnp.cos(ang), jnp.sin(ang)
    c = jnp.where(first | second, cos, 1.0)
    s_prev = jnp.where(second, sin, 0.0)
    s_next = jnp.where(first, -sin, 0.0)
    return jnp.stack([c, s_prev, s_next]).astype(F32)


def _rms(x, g):
    ms = jnp.mean(x * x, axis=-1, keepdims=True)
    return x * lax.rsqrt(ms + EPS) * g


def _rope_tiles(z, tile_tab, tabs, halves):
    out = []
    for c, t in enumerate(tile_tab):
        zt = z[:, c * LANES:(c + 1) * LANES]
        if t >= 0:
            tab, half = tabs[t], halves[t]
            zt = (zt * tab[0] + pltpu.roll(zt, half, 1) * tab[1]
                  + pltpu.roll(zt, LANES - half, 1) * tab[2])
        out.append(zt)
    return out


def _store_tiles(o_ref, tiles):
    for c, zt in enumerate(tiles):
        o_ref[:, c * LANES:(c + 1) * LANES] = zt.astype(o_ref.dtype)


ROPE_HALVES = (A_ROT // 2, IDX_ROT // 2, IDX_ROT // 2, B_ROPE // 2, B_ROPE // 2)


def _in_proj_kernel(x_ref, g_ref, wa_ref, wi_ref, wb_ref, wc_ref, gq_ref, wuq_ref, gkv_ref, wukv_ref,
                    ta_ref, ti_ref, tih_ref, tb_ref, tbh_ref,
                    za_ref, zi_ref, qb_ref, kvb_ref, kr_ref, zc_ref):
    tabs = (ta_ref, ti_ref, tih_ref, tb_ref, tbh_ref)
    rope = functools.partial(_rope_tiles, tabs=tabs, halves=ROPE_HALVES)
    h = _rms(x_ref[...], g_ref[...]).astype(BF16)
    dot = functools.partial(jnp.dot, preferred_element_type=F32)
    _store_tiles(za_ref, rope(dot(h, wa_ref[...]), (0, 0, 0, 0, 0, -1)))
    _store_tiles(zi_ref, rope(dot(h, wi_ref[...]), (1, 1, 1, 1, 2)))
    _store_tiles(zc_ref, rope(dot(h, wc_ref[...]), (-1,) * (3 * C_HEADS)))
    zb = dot(h, wb_ref[...])
    _store_tiles(kr_ref, rope(zb[:, B_Q_RANK + B_KV_RANK:], (4,)))
    cq = _rms(zb[:, :B_Q_RANK], gq_ref[...]).astype(BF16)
    _store_tiles(qb_ref, rope(dot(cq, wuq_ref[...]), (-1, -1, -1, -1, 3, 3)))
    ckv = _rms(zb[:, B_Q_RANK:B_Q_RANK + B_KV_RANK], gkv_ref[...]).astype(BF16)
    _store_tiles(kvb_ref, rope(dot(ckv, wukv_ref[...]), (-1,) * (2 * B_HEADS)))


def _in_proj(x, g, w_a, w_i, w_b, w_c, g_q, w_uq, g_kv, w_ukv, tabs, *, tm):
    t, d = x.shape
    seq_tiles = SEQ // tm
    fixed = lambda i: (0, 0)
    row = lambda i: (i, 0)
    weights = (w_a, w_i, w_b, w_c)
    outs = ((768, BF16), (640, F32), (768, BF16), (1024, BF16), (LANES, BF16), (1536, BF16))
    return pl.pallas_call(
        _in_proj_kernel,
        grid=(t // tm,),
        in_specs=[pl.BlockSpec((tm, d), row), pl.BlockSpec((1, d), fixed)]
        + [pl.BlockSpec(w.shape, fixed) for w in weights]
        + [pl.BlockSpec((1, B_Q_RANK), fixed), pl.BlockSpec(w_uq.shape, fixed),
           pl.BlockSpec((1, B_KV_RANK), fixed), pl.BlockSpec(w_ukv.shape, fixed)]
        + [pl.BlockSpec((3, tm, LANES), lambda i: (0, i % seq_tiles, 0)) for _ in tabs],
        out_specs=[pl.BlockSpec((tm, n), row) for n, _ in outs],
        out_shape=[jax.ShapeDtypeStruct((t, n), dt) for n, dt in outs],
        compiler_params=_cparams("parallel"),
    )(x, g.reshape(1, d), *weights, g_q.reshape(1, -1), w_uq, g_kv.reshape(1, -1), w_ukv, *tabs)


def _chunk_causal_mask(j, n_keys):
    qpos = j * Q_BLOCK + lax.broadcasted_iota(jnp.int32, (Q_BLOCK, n_keys), 0)
    kpos = lax.broadcasted_iota(jnp.int32, (Q_BLOCK, n_keys), 1)
    return (kpos >> 6) <= (qpos >> 6)


SEARCH_ROWS = 256
HALF_RANGE = 1 << 15


def _dsa_select(iq_ref, ikw_ref, key_ref, hi_ref, lo_ref, bias_ref, n_keys):
    j = pl.program_id(1)
    n_blk = n_keys // Q_BLOCK
    n_grp = n_keys // SEARCH_ROWS
    n_pack = n_keys // 16
    row0 = pl.multiple_of(j * Q_BLOCK, Q_BLOCK)
    sub = lax.broadcasted_iota(jnp.int32, (SEARCH_ROWS, Q_BLOCK), 0)
    lane = lax.broadcasted_iota(jnp.int32, (SEARCH_ROWS, Q_BLOCK), 1)
    q_chunk = (row0 + lane) >> 6

    iq = iq_ref[...].astype(BF16)
    iq_stack = jnp.concatenate([iq[:, h * IDX_DIM:(h + 1) * IDX_DIM] for h in range(IDX_HEADS)], axis=0)
    iw_t = ikw_ref[pl.ds(row0, Q_BLOCK), :].T * (IDX_HEADS ** -0.5)

    for g in range(n_grp):
        k0 = g * SEARCH_ROWS
        ik = ikw_ref[k0:k0 + SEARCH_ROWS, 0:IDX_DIM].astype(BF16)
        score = jnp.zeros((SEARCH_ROWS, Q_BLOCK), F32)
        for hp in range(IDX_HEADS // 2):
            r = _nt_dot(ik, iq_stack[hp * 2 * Q_BLOCK:(hp + 1) * 2 * Q_BLOCK])
            for u in range(2):
                h = 2 * hp + u
                rel = jnp.maximum(r[:, u * Q_BLOCK:(u + 1) * Q_BLOCK] * (IDX_DIM ** -0.5), 0.0)
                score = score + rel * iw_t[IDX_DIM + h:IDX_DIM + h + 1, :]
        allowed = ((k0 + sub) >> 6) <= q_chunk
        score = jnp.where(score == 0.0, 0.0, score)
        score = jnp.where(allowed, score, -jnp.inf)
        bits = lax.bitcast_convert_type(score, jnp.int32)
        key = bits ^ ((bits >> 31) & 0x7FFFFFFF)
        key_ref[k0:k0 + SEARCH_ROWS, :] = key
        hi_ref[k0:k0 + SEARCH_ROWS, :] = (key >> 16).astype(jnp.int16)
        lo_ref[k0:k0 + SEARCH_ROWS, :] = ((key & 0xFFFF) - HALF_RANGE).astype(jnp.int16)

    def count(pred):
        accs = [jnp.zeros((8, Q_BLOCK), F32)] * 4
        for r in range(n_keys // 8):
            accs[r % 4] = accs[r % 4] + jnp.where(pred(key_ref[8 * r:8 * (r + 1), :]), 1.0, 0.0)
        return jnp.sum(accs[0] + accs[1] + accs[2] + accs[3], axis=0, keepdims=True)

    one16 = jnp.ones((16, Q_BLOCK), jnp.int16)
    zero16 = jnp.zeros((16, Q_BLOCK), jnp.int16)

    def search16(ref):
        def rnd(i, base):
            cand = base + jnp.left_shift(jnp.int32(1), 15 - i)
            cand16 = jnp.broadcast_to(cand, (16, Q_BLOCK)).astype(jnp.int16)
            accs = [zero16] * 4
            for r in range(n_pack):
                accs[r % 4] = accs[r % 4] + jnp.where(ref[16 * r:16 * (r + 1), :] >= cand16, one16, zero16)
            total = (accs[0] + accs[1]) + (accs[2] + accs[3])
            cnt = jnp.sum(total.astype(jnp.int32), axis=0, keepdims=True)
            return jnp.where(cnt >= TOPK, cand, base)
        return lax.fori_loop(0, 16, rnd, jnp.full((1, Q_BLOCK), -HALF_RANGE, jnp.int32))

    top = search16(hi_ref)
    top16 = jnp.broadcast_to(top, (16, Q_BLOCK)).astype(jnp.int16)
    for r in range(n_pack):
        rows = slice(16 * r, 16 * (r + 1))
        hi = hi_ref[rows, :]
        lo_ref[rows, :] = jnp.where(hi > top16, jnp.int16(HALF_RANGE - 1),
                                    jnp.where(hi == top16, lo_ref[rows, :], jnp.int16(-HALF_RANGE)))
    thr = (top << 16) | (search16(lo_ref) + HALF_RANGE)
    thr8 = jnp.broadcast_to(thr, (8, Q_BLOCK))
    cnt_gt = count(lambda kk: kk > thr8)
    cnt_ge = count(lambda kk: kk >= thr8)
    tie_cols = (cnt_ge > TOPK) & (thr > NEG_INF_KEY)
    has_tie = jnp.max(jnp.where(tie_cols, 1.0, 0.0)) > 0.0

    for kb in range(n_blk):
        kk = key_ref[kb * Q_BLOCK:(kb + 1) * Q_BLOCK, :]
        bias_ref[kb * Q_BLOCK:(kb + 1) * Q_BLOCK, :] = jnp.where((kk >= thr) & (kk > NEG_INF_KEY), 0.0, -jnp.inf).T

    @pl.when(has_tie)
    def _():
        need = TOPK - cnt_gt
        row = lax.broadcasted_iota(jnp.int32, (Q_BLOCK, Q_BLOCK), 0)
        col = lax.broadcasted_iota(jnp.int32, (Q_BLOCK, Q_BLOCK), 1)
        lower = jnp.where(col < row, 1.0, 0.0).astype(BF16)

        def tie_block(kb, seen):
            k0 = pl.multiple_of(kb * Q_BLOCK, Q_BLOCK)
            kk = key_ref[pl.ds(k0, Q_BLOCK), :]
            eq = jnp.where(kk == thr, 1.0, 0.0)
            before = jnp.dot(lower, eq.astype(BF16), preferred_element_type=F32) + seen
            keep = (kk > thr) | ((kk == thr) & (before < need))
            bias_ref[pl.ds(k0, Q_BLOCK), :] = jnp.where(keep & (kk > NEG_INF_KEY), 0.0, -jnp.inf).T
            return seen + jnp.sum(eq, axis=0, keepdims=True)

        lax.fori_loop(0, n_blk, tie_block, jnp.zeros((1, Q_BLOCK), F32))


def _dsa_attend(q_ref, kv_ref, bias_ref, o_ref, n_keys):
    q = q_ref[...]
    k = kv_ref[0:n_keys, 0:A_HEAD_DIM]
    v = kv_ref[0:n_keys, A_HEAD_DIM:2 * A_HEAD_DIM]
    bias = jnp.concatenate([bias_ref[c * Q_BLOCK:(c + 1) * Q_BLOCK, :] for c in range(n_keys // Q_BLOCK)],
                           axis=1)
    for h in range(A_HEADS):
        logits = _nt_dot(q[:, h * A_HEAD_DIM:(h + 1) * A_HEAD_DIM], k) * (A_HEAD_DIM ** -0.5) + bias
        m = jnp.max(logits, axis=-1, keepdims=True)
        p = jnp.exp(logits - m)
        l = jnp.sum(p, axis=-1, keepdims=True)
        o = jnp.dot(p.astype(BF16), v, preferred_element_type=F32) / l
        o_ref[:, h * A_HEAD_DIM:(h + 1) * A_HEAD_DIM] = o.astype(o_ref.dtype)


def _dsa_kernel(q_ref, kv_ref, iq_ref, ikw_ref, o_ref, key_ref, hi_ref, lo_ref, bias_ref):
    j = pl.program_id(1)
    for v in range(KV_VARIANTS):
        @pl.when(j // (N_QB // KV_VARIANTS) == v)
        def _(v=v):
            n_keys = KV_STEP * (v + 1)
            n_scored = -(-n_keys // SEARCH_ROWS) * SEARCH_ROWS
            _dsa_select(iq_ref, ikw_ref, key_ref, hi_ref, lo_ref, bias_ref, n_scored)
            _dsa_attend(q_ref, kv_ref, bias_ref, o_ref, n_keys)


def _dsa_attention(za, zi, bsz):
    return pl.pallas_call(
        _dsa_kernel,
        grid=(bsz, N_QB),
        in_specs=[
            pl.BlockSpec((None, Q_BLOCK, 512), lambda b, j: (b, j, 0)),
            pl.BlockSpec((None, SEQ, 256), lambda b, j: (b, 0, 2)),
            pl.BlockSpec((None, Q_BLOCK, 512), lambda b, j: (b, j, 0)),
            pl.BlockSpec((None, SEQ, LANES), lambda b, j: (b, 0, 4)),
        ],
        out_specs=pl.BlockSpec((None, Q_BLOCK, 512), lambda b, j: (b, j, 0)),
        out_shape=jax.ShapeDtypeStruct((bsz, SEQ, 512), BF16),
        scratch_shapes=[pltpu.VMEM((SEQ, Q_BLOCK), jnp.int32),
                        pltpu.VMEM((SEQ, Q_BLOCK), jnp.int16),
                        pltpu.VMEM((SEQ, Q_BLOCK), jnp.int16),
                        pltpu.VMEM((SEQ, Q_BLOCK), F32)],
        compiler_params=_cparams("parallel", "arbitrary"),
    )(za, za, zi, zi)


def _mla_body(q_ref, kv_ref, kr_ref, o_ref, n_keys):
    j = pl.program_id(1)
    mask = _chunk_causal_mask(j, n_keys)
    kr = kr_ref[0:n_keys, 0:B_ROPE]
    scale = (B_NOPE + B_ROPE) ** -0.5
    for h in range(B_HEADS):
        qcat = jnp.concatenate(
            [q_ref[:, h * B_NOPE:(h + 1) * B_NOPE],
             q_ref[:, B_HEADS * B_NOPE + h * B_ROPE:B_HEADS * B_NOPE + (h + 1) * B_ROPE]], axis=1)
        kcat = jnp.concatenate([kv_ref[0:n_keys, h * B_NOPE:(h + 1) * B_NOPE], kr], axis=1)
        vv = kv_ref[0:n_keys, B_HEADS * B_NOPE + h * B_V:B_HEADS * B_NOPE + (h + 1) * B_V]
        s = _nt_dot(qcat, kcat) * scale
        s = jnp.where(mask, s, -jnp.inf)
        m = jnp.max(s, axis=-1, keepdims=True)
        p = jnp.exp(s - m)
        l = jnp.sum(p, axis=-1, keepdims=True)
        o = jnp.dot(p.astype(BF16), vv, preferred_element_type=F32) / l
        o_ref[:, h * B_V:(h + 1) * B_V] = o.astype(o_ref.dtype)


def _mla_kernel(q_ref, kv_ref, kr_ref, o_ref):
    j = pl.program_id(1)
    for v in range(KV_VARIANTS):
        @pl.when(j // (N_QB // KV_VARIANTS) == v)
        def _(v=v):
            _mla_body(q_ref, kv_ref, kr_ref, o_ref, KV_STEP * (v + 1))


def _mla_attention(qb, kvb, kr, bsz):
    return pl.pallas_call(
        _mla_kernel,
        grid=(bsz, N_QB),
        in_specs=[
            pl.BlockSpec((None, Q_BLOCK, 768), lambda b, j: (b, j, 0)),
            pl.BlockSpec((None, SEQ, 1024), lambda b, j: (b, 0, 0)),
            pl.BlockSpec((None, SEQ, LANES), lambda b, j: (b, 0, 0)),
        ],
        out_specs=pl.BlockSpec((None, Q_BLOCK, 512), lambda b, j: (b, j, 0)),
        out_shape=jax.ShapeDtypeStruct((bsz, SEQ, 512), BF16),
        compiler_params=_cparams("parallel", "arbitrary"),
    )(qb, kvb, kr)


def _band_bias(rel_table):
    n = 2 * REL_CLIP + 1
    period = 2 * n - 1
    heads = rel_table.shape[0]
    ext = jnp.concatenate([rel_table, jnp.broadcast_to(rel_table[:, n - 1:n], (heads, n - 1))], axis=1)
    kj = np.arange(Q_BLOCK)[:, None]
    qi = np.arange(Q_BLOCK)[None, :]
    out = []
    for d in range(C_KEY_BLOCKS):
        base = d * Q_BLOCK + REL_CLIP
        if base - (Q_BLOCK - 1) >= n - 1:
            bias = jnp.broadcast_to(rel_table[:, n - 1][:, None, None], (heads, Q_BLOCK, Q_BLOCK))
        else:
            shifted = jnp.roll(ext, -base, axis=1)
            bias = jnp.tile(shifted, (1, Q_BLOCK))[:, :Q_BLOCK * (period - 1)]
            bias = bias.reshape(heads, Q_BLOCK, period - 1)[:, :, :Q_BLOCK]
        cdiff = 2 * d + qi // CHUNK - kj // CHUNK
        valid = (cdiff >= 0) & (cdiff <= C_LEFT_CHUNKS)
        out.append(jnp.where(valid[None], bias.astype(F32), -jnp.inf))
    return jnp.swapaxes(jnp.stack(out, axis=1), 2, 3)


BAND_QB = 2


def _band_kernel(q_ref, kv_ref, bias_ref, o_ref):
    scale = C_HEAD_DIM ** -0.5
    window = C_KEY_BLOCKS * Q_BLOCK
    for sb in range(BAND_QB):
        j = pl.program_id(1) * BAND_QB + sb
        rows = slice(sb * Q_BLOCK, (sb + 1) * Q_BLOCK)
        first = jnp.maximum(j - (C_KEY_BLOCKS - 1), 0)
        win = pl.ds(pl.multiple_of(first * Q_BLOCK, Q_BLOCK), window)
        for h in range(C_HEADS):
            q = q_ref[rows, h * C_HEAD_DIM:(h + 1) * C_HEAD_DIM]
            k = kv_ref[win, h * C_HEAD_DIM:(h + 1) * C_HEAD_DIM]
            v = kv_ref[win, (C_HEADS + h) * C_HEAD_DIM:(C_HEADS + h + 1) * C_HEAD_DIM]
            s = _nt_dot(q, k) * scale
            parts = []
            for c in range(C_KEY_BLOCKS):
                d = j - (first + c)
                bias = bias_ref[h, jnp.clip(d, 0, C_KEY_BLOCKS - 1)]
                parts.append(jnp.where(d >= 0, s[:, c * Q_BLOCK:(c + 1) * Q_BLOCK] + bias, -jnp.inf))
            s = jnp.concatenate(parts, axis=1)
            m = jnp.max(s, axis=-1, keepdims=True)
            p = jnp.exp(s - m)
            l = jnp.sum(p, axis=-1, keepdims=True)
            o = jnp.dot(p.astype(BF16), v, preferred_element_type=F32) / l
            o_ref[rows, h * C_HEAD_DIM:(h + 1) * C_HEAD_DIM] = o.astype(o_ref.dtype)


def _band_attention(zc, bias, bsz):
    rows = BAND_QB * Q_BLOCK
    return pl.pallas_call(
        _band_kernel,
        grid=(bsz, SEQ // rows),
        in_specs=[
            pl.BlockSpec((None, rows, 512), lambda b, j: (b, j, 2)),
            pl.BlockSpec((None, SEQ, 1024), lambda b, j: (b, 0, 0)),
            pl.BlockSpec((C_HEADS, C_KEY_BLOCKS, Q_BLOCK, Q_BLOCK), lambda b, j: (0, 0, 0, 0)),
        ],
        out_specs=pl.BlockSpec((None, rows, 512), lambda b, j: (b, j, 0)),
        out_shape=jax.ShapeDtypeStruct((bsz, SEQ, 512), BF16),
        compiler_params=_cparams("parallel", "arbitrary"),
    )(zc, zc, bias)


def _merge_kernel(x_ref, g_ref, wgl_ref, oa_ref, ob_ref, oc_ref, wpa_ref, wpb_ref, wpc_ref, wout_ref, o_ref):
    x = x_ref[...]
    ms = jnp.mean(x * x, axis=-1, keepdims=True)
    h = (x * lax.rsqrt(ms + EPS) * g_ref[...]).astype(BF16)
    mix = jnp.zeros(x.shape, F32)
    for i, (o_in, wp) in enumerate(((oa_ref, wpa_ref), (ob_ref, wpb_ref), (oc_ref, wpc_ref))):
        gl = jnp.dot(h, wgl_ref[:, i * D_MODEL:(i + 1) * D_MODEL], preferred_element_type=F32)
        gate = jax.nn.sigmoid(gl)
        mix = mix + gate * jnp.dot(o_in[...], wp[...], preferred_element_type=F32)
    o_ref[...] = x + jnp.dot(mix.astype(BF16), wout_ref[...], preferred_element_type=F32)


def _merge(x, g, wgl, oa, ob, oc, wpa, wpb, wpc, wout, *, tm):
    t, d = x.shape
    row = lambda i: (i, 0)
    fixed = lambda i: (0, 0)
    return pl.pallas_call(
        _merge_kernel,
        grid=(t // tm,),
        in_specs=[
            pl.BlockSpec((tm, d), row),
            pl.BlockSpec((1, d), fixed),
            pl.BlockSpec((d, 3 * d), fixed),
            pl.BlockSpec((tm, 512), row),
            pl.BlockSpec((tm, 512), row),
            pl.BlockSpec((tm, 512), row),
            pl.BlockSpec((512, d), fixed),
            pl.BlockSpec((512, d), fixed),
            pl.BlockSpec((512, d), fixed),
            pl.BlockSpec((d, d), fixed),
        ],
        out_specs=pl.BlockSpec((tm, d), row),
        out_shape=jax.ShapeDtypeStruct((t, d), F32),
        compiler_params=_cparams("parallel"),
    )(x, g.reshape(1, d), wgl, oa, ob, oc, wpa, wpb, wpc, wout)


def _first_argmax(vals, lane):
    m = jnp.max(vals, axis=-1, keepdims=True)
    idx = jnp.min(jnp.where(vals == m, lane, LANES), axis=-1, keepdims=True)
    return m, idx


def _pack_bf16_pairs(h):
    n = h.shape[1] // 2
    bits = lax.bitcast_convert_type(h.astype(jnp.bfloat16).astype(F32), jnp.int32)
    return lax.shift_right_logical(bits[:, :n], 16) | bits[:, n:]


def _unpack_bf16_pairs(w):
    lo = lax.bitcast_convert_type(w << 16, F32).astype(BF16)
    hi = lax.bitcast_convert_type(w & HIGH_HALF, F32).astype(BF16)
    return lo, hi


def _router_kernel(x_ref, g_ref, w_ref, b_ref, hp_ref, meta_ref, seg_ref):
    x = x_ref[...]
    ms = jnp.mean(x * x, axis=-1, keepdims=True)
    h = x * lax.rsqrt(ms + EPS) * g_ref[...]
    hp_ref[...] = _pack_bf16_pairs(h)
    logits = jnp.dot(h, w_ref[...], preferred_element_type=F32, precision=lax.Precision.HIGHEST) + b_ref[...]
    lane = lax.broadcasted_iota(jnp.int32, logits.shape, 1)
    is_grp = (lane >= N_EXPERTS) & (lane < N_EXPERTS + N_GROUPS)
    gl = jnp.where(is_grp, logits, -jnp.inf)
    gmax, gidx = _first_argmax(gl, lane)
    pg = 1.0 / jnp.sum(jnp.exp(gl - gmax), axis=-1, keepdims=True)
    gsel = gidx - N_EXPERTS
    in_grp = (lane >> 3) == gsel
    el = jnp.where(in_grp, logits, -jnp.inf)
    m1, i1 = _first_argmax(el, lane)
    z = jnp.sum(jnp.exp(el - m1), axis=-1, keepdims=True)
    el2 = jnp.where(lane == i1, -jnp.inf, el)
    m2, i2 = _first_argmax(el2, lane)
    pe1 = 1.0 / z
    pe2 = jnp.exp(m2 - m1) / z
    den = pe1 + pe2
    w1 = pg * pe1 / den
    w2 = pg * pe2 / den

    sel1 = lane == i1
    sel2 = lane == i2
    onehot = jnp.where(sel1 | sel2, 1.0, 0.0)
    a = lax.broadcasted_iota(jnp.int32, (LANES, LANES), 0)
    b = lax.broadcasted_iota(jnp.int32, (LANES, LANES), 1)
    lower = jnp.where(b < a, 1.0, 0.0).astype(BF16)
    carry = jnp.zeros((1, LANES), F32)
    ranks = []
    for c in range(MOE_TILE // LANES):
        blk = onehot[c * LANES:(c + 1) * LANES]
        ranks.append(jnp.dot(lower, blk.astype(BF16), preferred_element_type=F32) + carry)
        carry = carry + jnp.sum(blk, axis=0, keepdims=True)
    rank = jnp.concatenate(ranks, axis=0)
    cnt = jnp.broadcast_to(carry, (8, LANES))
    seg = jnp.floor((cnt + 7.0) * 0.125) * 8.0
    lane8 = lax.broadcasted_iota(jnp.int32, (8, LANES), 1)
    scan = seg
    for k in (1, 2, 4, 8, 16, 32, 64):
        scan = scan + jnp.where(lane8 >= k, pltpu.roll(scan, k, 1), 0.0)
    off = scan - seg
    where_row = rank + off[0:1]
    pos1 = jnp.sum(jnp.where(sel1, where_row, 0.0), axis=-1, keepdims=True)
    pos2 = jnp.sum(jnp.where(sel2, where_row, 0.0), axis=-1, keepdims=True)
    meta_ref[...] = (jnp.where(lane == 0, pos1, 0.0) + jnp.where(lane == 1, pos2, 0.0)
                     + jnp.where(lane == 2, w1, 0.0) + jnp.where(lane == 3, w2, 0.0))
    row8 = lax.broadcasted_iota(jnp.int32, (8, LANES), 0)
    seg_ref[...] = jnp.where(row8 == 0, cnt, jnp.where(row8 == 1, off, 0.0)).astype(jnp.int32)


def _router(x, g, w, b):
    t, d = x.shape
    nt = t // MOE_TILE
    return pl.pallas_call(
        _router_kernel,
        grid=(nt,),
        in_specs=[
            pl.BlockSpec((MOE_TILE, d), lambda i: (i, 0)),
            pl.BlockSpec((1, d), lambda i: (0, 0)),
            pl.BlockSpec((d, LANES), lambda i: (0, 0)),
            pl.BlockSpec((1, LANES), lambda i: (0, 0)),
        ],
        out_specs=[
            pl.BlockSpec((MOE_TILE, d // 2), lambda i: (i, 0)),
            pl.BlockSpec((MOE_TILE, LANES), lambda i: (i, 0)),
            pl.BlockSpec((None, 8, LANES), lambda i: (i, 0, 0)),
        ],
        out_shape=[
            jax.ShapeDtypeStruct((t, d // 2), jnp.int32),
            jax.ShapeDtypeStruct((t, LANES), F32),
            jax.ShapeDtypeStruct((nt, 8, LANES), jnp.int32),
        ],
        compiler_params=_cparams("parallel"),
    )(x, g.reshape(1, d), w, b)


def _scatter_kernel(pos_ref, hp_ref, xs_ref):
    xs_ref[...] = jnp.zeros_like(xs_ref)

    def body(t, carry):
        row = hp_ref[pl.ds(t, 1), :]
        xs_ref[pl.ds(pos_ref[0, t], 1), :] = row
        xs_ref[pl.ds(pos_ref[0, MOE_TILE + t], 1), :] = row
        return carry

    lax.fori_loop(0, MOE_TILE, body, 0, unroll=8)


def _scatter_rows(pos, hp):
    nt = pos.shape[0]
    return pl.pallas_call(
        _scatter_kernel,
        grid=(nt,),
        in_specs=[
            pl.BlockSpec((None, 1, 2 * MOE_TILE), lambda i: (i, 0, 0), memory_space=pltpu.SMEM),
            pl.BlockSpec((MOE_TILE, hp.shape[1]), lambda i: (i, 0)),
        ],
        out_specs=pl.BlockSpec((None, XS_ROWS, hp.shape[1]), lambda i: (i, 0, 0)),
        out_shape=jax.ShapeDtypeStruct((nt, XS_ROWS, hp.shape[1]), jnp.int32),
        compiler_params=_cparams("parallel"),
    )(pos, hp)


CAST_EXPERTS = 4


def _cast_kernel(w_ref, o_ref):
    o_ref[...] = w_ref[...].astype(o_ref.dtype)


def _to_bf16(w):
    n_l, n_e, a, b = w.shape
    spec = pl.BlockSpec((None, CAST_EXPERTS, a, b), lambda l, e: (l, e, 0, 0))
    return pl.pallas_call(
        _cast_kernel,
        grid=(n_l, n_e // CAST_EXPERTS),
        in_specs=[spec],
        out_specs=spec,
        out_shape=jax.ShapeDtypeStruct(w.shape, BF16),
        compiler_params=_cparams("parallel", "parallel"),
    )(w)


def _experts_kernel(cnt_ref, off_ref, xs_ref, wg_ref, wu_ref, wd_ref, ys_ref):
    i = pl.program_id(0)
    e = pl.program_id(1)
    half = D_MODEL // 2

    @pl.when(e == 0)
    def _():
        ys_ref[2 * MOE_TILE:XS_ROWS, :] = jnp.zeros((XS_ROWS - 2 * MOE_TILE, D_MODEL), F32)

    n = cnt_ref[i, e]
    off = off_ref[i, e]

    def body(c, carry):
        start = pl.multiple_of(off + c * EXPERT_CHUNK, 8)
        lo, hi = _unpack_bf16_pairs(xs_ref[pl.ds(start, EXPERT_CHUNK), :])
        a = (jnp.dot(lo, wg_ref[0:half], preferred_element_type=F32)
             + jnp.dot(hi, wg_ref[half:D_MODEL], preferred_element_type=F32))
        u = (jnp.dot(lo, wu_ref[0:half], preferred_element_type=F32)
             + jnp.dot(hi, wu_ref[half:D_MODEL], preferred_element_type=F32))
        hh = (a * jax.nn.sigmoid(a)) * u
        ys_ref[pl.ds(start, EXPERT_CHUNK), :] = jnp.dot(hh.astype(BF16), wd_ref[...], preferred_element_type=F32)
        return carry

    lax.fori_loop(0, (n + EXPERT_CHUNK - 1) // EXPERT_CHUNK, body, 0)


def _experts(cnt, off, xs, wg, wu, wd, layer):
    nt = xs.shape[0]
    d = D_MODEL
    grid_spec = pltpu.PrefetchScalarGridSpec(
        num_scalar_prefetch=2,
        grid=(nt, N_EXPERTS),
        in_specs=[
            pl.BlockSpec((None, XS_ROWS, d // 2), lambda i, e, c, o: (i, 0, 0), pipeline_mode=pl.Buffered(1)),
            pl.BlockSpec((None, None, d, D_EXPERT), lambda i, e, c, o: (layer, e, 0, 0)),
            pl.BlockSpec((None, None, d, D_EXPERT), lambda i, e, c, o: (layer, e, 0, 0)),
            pl.BlockSpec((None, None, D_EXPERT, d), lambda i, e, c, o: (layer, e, 0, 0)),
        ],
        out_specs=pl.BlockSpec((None, XS_ROWS, d), lambda i, e, c, o: (i, 0, 0)),
    )
    return pl.pallas_call(
        _experts_kernel,
        grid_spec=grid_spec,
        out_shape=jax.ShapeDtypeStruct((nt, XS_ROWS, d), F32),
        compiler_params=_cparams("parallel", "arbitrary"),
    )(cnt, off, xs, wg, wu, wd)


def _combine_kernel(pos_ref, wt_ref, x_ref, ys_ref, g_ref, o_ref, *, final_norm):
    s = pl.program_id(1)

    def body(tl, carry):
        t = s * COMBINE_ROWS + tl
        y = (ys_ref[pl.ds(pos_ref[0, t], 1), :] * wt_ref[0, t]
             + ys_ref[pl.ds(pos_ref[0, MOE_TILE + t], 1), :] * wt_ref[0, MOE_TILE + t])
        o_ref[pl.ds(tl, 1), :] = x_ref[pl.ds(tl, 1), :] + y
        return carry

    lax.fori_loop(0, COMBINE_ROWS, body, 0, unroll=8)
    if final_norm:
        o_ref[...] = _rms(o_ref[...], g_ref[...])


def _combine(pos, wt, x, ys, final_g):
    t, d = x.shape
    nt = pos.shape[0]
    sub = MOE_TILE // COMBINE_ROWS
    g = jnp.ones((d,), F32) if final_g is None else final_g
    return pl.pallas_call(
        functools.partial(_combine_kernel, final_norm=final_g is not None),
        grid=(nt, sub),
        in_specs=[
            pl.BlockSpec((None, 1, 2 * MOE_TILE), lambda i, s: (i, 0, 0), memory_space=pltpu.SMEM),
            pl.BlockSpec((None, 1, 2 * MOE_TILE), lambda i, s: (i, 0, 0), memory_space=pltpu.SMEM),
            pl.BlockSpec((COMBINE_ROWS, d), lambda i, s: (i * sub + s, 0)),
            pl.BlockSpec((None, XS_ROWS, d), lambda i, s: (i, 0, 0)),
            pl.BlockSpec((1, d), lambda i, s: (0, 0)),
        ],
        out_specs=pl.BlockSpec((COMBINE_ROWS, d), lambda i, s: (i * sub + s, 0)),
        out_shape=jax.ShapeDtypeStruct((t, d), F32),
        compiler_params=_cparams("parallel", "arbitrary"),
    )(pos, wt, x, ys, g.reshape(1, d))


def _moe(x, g, w_r, b_r, wg, wu, wd, layer, final_g=None):
    t = x.shape[0]
    nt = t // MOE_TILE
    hp, meta, seg = _router(x, g, w_r, b_r)
    pair_major = lambda m: m.reshape(nt, MOE_TILE, 2).transpose(0, 2, 1).reshape(nt, 1, 2 * MOE_TILE)
    pos = pair_major(meta[:, 0:2].astype(jnp.int32))
    wt = pair_major(meta[:, 2:4])
    xs = _scatter_rows(pos, hp)
    ys = _experts(seg[:, 0, :N_EXPERTS], seg[:, 1, :N_EXPERTS], xs, wg, wu, wd, layer)
    return _combine(pos, wt, x, ys, final_g)


def _pad_cols(w, n):
    return jnp.pad(w, ((0, 0), (0, n - w.shape[1])))


def kernel(x, attn_norm_g, w_in, b_q_norm_g, b_w_uq, b_kv_norm_g, b_w_ukv, c_rel_bias, w_proj_a, w_proj_b, w_proj_c, w_out, ffn_norm_g, w_group, b_group, w_router, b_router, w_gate, w_up, w_down, final_norm_g):
    bsz, seq, d = x.shape
    assert (seq, d) == (SEQ, D_MODEL)
    t = bsz * seq
    depth = w_in.shape[0]
    tm = 512

    tab_a = _rope_table(A_ROT, A_HEAD_DIM)
    tab_i = _rope_table(IDX_ROT, IDX_DIM)
    tab_i_half = _rope_table(IDX_ROT, IDX_DIM, active_lanes=IDX_DIM)
    tab_b = _rope_table(B_ROPE, B_ROPE)
    tab_b_half = _rope_table(B_ROPE, B_ROPE, active_lanes=B_ROPE)

    tabs = (tab_a, tab_i, tab_i_half, tab_b, tab_b_half)
    w_gate, w_up, w_down = _to_bf16(w_gate), _to_bf16(w_up), _to_bf16(w_down)
    xf = x.reshape(t, d)
    for l in range(depth):
        w = w_in[l]
        w_a = w[:, 0:768].astype(BF16)
        w_i = _pad_cols(w[:, 768:1352], 640).astype(BF16)
        w_b = _pad_cols(w[:, 1352:1800], 512).astype(BF16)
        w_c = jnp.concatenate([w[:, 2312:3336], w[:, 1800:2312]], axis=1).astype(BF16)
        w_g = w[:, 3336:6408].astype(BF16)
        g_attn = attn_norm_g[l]
        w_uq = b_w_uq[l].reshape(B_Q_RANK, B_HEADS, B_NOPE + B_ROPE)
        w_uq = jnp.concatenate([w_uq[:, :, :B_NOPE].reshape(B_Q_RANK, -1),
                                w_uq[:, :, B_NOPE:].reshape(B_Q_RANK, -1)], axis=1).astype(BF16)
        w_ukv = b_w_ukv[l].reshape(B_KV_RANK, B_HEADS, B_NOPE + B_V)
        w_ukv = jnp.concatenate([w_ukv[:, :, :B_NOPE].reshape(B_KV_RANK, -1),
                                 w_ukv[:, :, B_NOPE:].reshape(B_KV_RANK, -1)], axis=1).astype(BF16)

        za, zi, qb, kvb, kr, zc = _in_proj(xf, g_attn, w_a, w_i, w_b, w_c, b_q_norm_g[l], w_uq,
                                           b_kv_norm_g[l], w_ukv, tabs, tm=tm)
        per_seq = lambda z: z.reshape(bsz, seq, -1)
        o_a = _dsa_attention(per_seq(za), per_seq(zi), bsz)
        o_b = _mla_attention(per_seq(qb), per_seq(kvb), per_seq(kr), bsz)
        o_c = _band_attention(per_seq(zc), _band_bias(c_rel_bias[l]), bsz)

        xf = _merge(xf, g_attn, w_g, o_a.reshape(t, -1), o_b.reshape(t, -1), o_c.reshape(t, -1),
                    w_proj_a[l].astype(BF16), w_proj_b[l].astype(BF16), w_proj_c[l].astype(BF16),
                    w_out[l].astype(BF16), tm=tm)

        w_r = _pad_cols(jnp.concatenate([w_router[l], w_group[l]], axis=1), LANES)
        b_r = _pad_cols(jnp.concatenate([b_router[l], b_group[l]])[None, :], LANES)
        xf = _moe(xf, ffn_norm_g[l], w_r, b_r, w_gate, w_up, w_down, l,
                  final_g=final_norm_g if l == depth - 1 else None)

    return xf.reshape(bsz, seq, d)
```

```python
import functools

import numpy as np
import jax
import jax.numpy as jnp
from jax import lax
from jax.experimental import pallas as pl
from jax.experimental.pallas import tpu as pltpu

F32 = jnp.float32
BF16 = jnp.bfloat16

LANES = 128
D_MODEL = 1024
SEQ = 2048
CHUNK = 64
Q_BLOCK = 128
ROPE_THETA = 500000.0
EPS = 1e-6

A_HEADS = 4
A_HEAD_DIM = 128
A_ROT = 32
IDX_HEADS = 8
IDX_DIM = 64
IDX_ROT = 16
TOPK = 256
B_HEADS = 4
B_NOPE = 128
B_ROPE = 64
B_V = 128
B_Q_RANK = 256
B_KV_RANK = 128
C_HEADS = 4
C_HEAD_DIM = 128
C_LEFT_CHUNKS = 8
REL_CLIP = 128
N_GROUPS = 4
EXPERTS_PER_GROUP = 8
N_EXPERTS = 32
D_EXPERT = 256

C_KEY_BLOCKS = C_LEFT_CHUNKS * CHUNK // Q_BLOCK + 1
N_QB = SEQ // Q_BLOCK
KV_VARIANTS = 8
KV_STEP = SEQ // KV_VARIANTS

VMEM_LIMIT = 56 * 1024 * 1024

MOE_TILE = 2048
EXPERT_CHUNK = 256
XS_ROWS = 2 * MOE_TILE + 2 * EXPERT_CHUNK
assert XS_ROWS >= 2 * MOE_TILE + N_EXPERTS * 7 + EXPERT_CHUNK - 1
COMBINE_ROWS = 512
HIGH_HALF = -65536

INT_MIN = -2 ** 31
NEG_INF_KEY = int(np.array(0x807FFFFF, np.uint32).view(np.int32))

NT_DIMS = (((1,), (1,)), ((), ()))


def _nt_dot(a, b):
    return lax.dot_general(a, b, NT_DIMS, preferred_element_type=F32)


def _cparams(*sem):
    return pltpu.CompilerParams(dimension_semantics=sem, vmem_limit_bytes=VMEM_LIMIT)


def _rope_table(rot, period, active_lanes=LANES):
    half = rot // 2
    lane = np.arange(LANES)
    p = lane % period
    first = (p < half) & (lane < active_lanes)
    second = (p >= half) & (p < rot) & (lane < active_lanes)
    idx = np.where(first, p, np.where(second, p - half, 0))
    pos = jnp.arange(SEQ, dtype=F32)
    inv = ROPE_THETA ** (-jnp.arange(0, rot, 2, dtype=F32) / rot)
    ang = pos[:, None] * inv[idx][None, :]
    cos, sin = jnp.cos(ang), jnp.sin(ang)
    c = jnp.where(first | second, cos, 1.0)
    s_prev = jnp.where(second, sin, 0.0)
    s_next = jnp.where(first, -sin, 0.0)
    return jnp.stack([c, s_prev, s_next]).astype(F32)


def _rms(x, g):
    ms = jnp.mean(x * x, axis=-1, keepdims=True)
    return x * lax.rsqrt(ms + EPS) * g


def _rope_tiles(z, tile_tab, tabs, halves):
    out = []
    for c, t in enumerate(tile_tab):
        zt = z[:, c * LANES:(c + 1) * LANES]
        if t >= 0:
            tab, half = tabs[t], halves[t]
            zt = (zt * tab[0] + pltpu.roll(zt, half, 1) * tab[1]
                  + pltpu.roll(zt, LANES - half, 1) * tab[2])
        out.append(zt)
    return out


def _store_tiles(o_ref, tiles):
    for c, zt in enumerate(tiles):
        o_ref[:, c * LANES:(c + 1) * LANES] = zt.astype(o_ref.dtype)


ROPE_HALVES = (A_ROT // 2, IDX_ROT // 2, IDX_ROT // 2, B_ROPE // 2, B_ROPE // 2)


def _in_proj_kernel(x_ref, g_ref, wa_ref, wi_ref, wb_ref, wc_ref, gq_ref, wuq_ref, gkv_ref, wukv_ref,
                    ta_ref, ti_ref, tih_ref, tb_ref, tbh_ref,
                    za_ref, zi_ref, qb_ref, kvb_ref, kr_ref, zc_ref):
    tabs = (ta_ref, ti_ref, tih_ref, tb_ref, tbh_ref)
    rope = functools.partial(_rope_tiles, tabs=tabs, halves=ROPE_HALVES)
    h = _rms(x_ref[...], g_ref[...]).astype(BF16)
    dot = functools.partial(jnp.dot, preferred_element_type=F32)
    _store_tiles(za_ref, rope(dot(h, wa_ref[...]), (0, 0, 0, 0, 0, -1)))
    _store_tiles(zi_ref, rope(dot(h, wi_ref[...]), (1, 1, 1, 1, 2)))
    _store_tiles(zc_ref, rope(dot(h, wc_ref[...]), (-1,) * (3 * C_HEADS)))
    zb = dot(h, wb_ref[...])
    _store_tiles(kr_ref, rope(zb[:, B_Q_RANK + B_KV_RANK:], (4,)))
    cq = _rms(zb[:, :B_Q_RANK], gq_ref[...]).astype(BF16)
    _store_tiles(qb_ref, rope(dot(cq, wuq_ref[...]), (-1, -1, -1, -1, 3, 3)))
    ckv = _rms(zb[:, B_Q_RANK:B_Q_RANK + B_KV_RANK], gkv_ref[...]).astype(BF16)
    _store_tiles(kvb_ref, rope(dot(ckv, wukv_ref[...]), (-1,) * (2 * B_HEADS)))


def _in_proj(x, g, w_a, w_i, w_b, w_c, g_q, w_uq, g_kv, w_ukv, tabs, *, tm):
    t, d = x.shape
    seq_tiles = SEQ // tm
    fixed = lambda i: (0, 0)
    row = lambda i: (i, 0)
    weights = (w_a, w_i, w_b, w_c)
    outs = ((768, BF16), (640, F32), (768, BF16), (1024, BF16), (LANES, BF16), (1536, BF16))
    return pl.pallas_call(
        _in_proj_kernel,
        grid=(t // tm,),
        in_specs=[pl.BlockSpec((tm, d), row), pl.BlockSpec((1, d), fixed)]
        + [pl.BlockSpec(w.shape, fixed) for w in weights]
        + [pl.BlockSpec((1, B_Q_RANK), fixed), pl.BlockSpec(w_uq.shape, fixed),
           pl.BlockSpec((1, B_KV_RANK), fixed), pl.BlockSpec(w_ukv.shape, fixed)]
        + [pl.BlockSpec((3, tm, LANES), lambda i: (0, i % seq_tiles, 0)) for _ in tabs],
        out_specs=[pl.BlockSpec((tm, n), row) for n, _ in outs],
        out_shape=[jax.ShapeDtypeStruct((t, n), dt) for n, dt in outs],
        compiler_params=_cparams("parallel"),
    )(x, g.reshape(1, d), *weights, g_q.reshape(1, -1), w_uq, g_kv.reshape(1, -1), w_ukv, *tabs)


def _chunk_causal_mask(j, n_keys):
    qpos = j * Q_BLOCK + lax.broadcasted_iota(jnp.int32, (Q_BLOCK, n_keys), 0)
    kpos = lax.broadcasted_iota(jnp.int32, (Q_BLOCK, n_keys), 1)
    return (kpos >> 6) <= (qpos >> 6)


SEARCH_ROWS = 256
HALF_RANGE = 1 << 15


def _dsa_select(iq_ref, ikw_ref, key_ref, hi_ref, lo_ref, bias_ref, n_keys):
    j = pl.program_id(1)
    n_blk = n_keys // Q_BLOCK
    n_grp = n_keys // SEARCH_ROWS
    n_pack = n_keys // 16
    row0 = pl.multiple_of(j * Q_BLOCK, Q_BLOCK)
    sub = lax.broadcasted_iota(jnp.int32, (SEARCH_ROWS, Q_BLOCK), 0)
    lane = lax.broadcasted_iota(jnp.int32, (SEARCH_ROWS, Q_BLOCK), 1)
    q_chunk = (row0 + lane) >> 6

    iq = iq_ref[...].astype(BF16)
    iq_stack = jnp.concatenate([iq[:, h * IDX_DIM:(h + 1) * IDX_DIM] for h in range(IDX_HEADS)], axis=0)
    iw_t = ikw_ref[pl.ds(row0, Q_BLOCK), :].T * (IDX_HEADS ** -0.5)

    for g in range(n_grp):
        k0 = g * SEARCH_ROWS
        ik = ikw_ref[k0:k0 + SEARCH_ROWS, 0:IDX_DIM].astype(BF16)
        score = jnp.zeros((SEARCH_ROWS, Q_BLOCK), F32)
        for hp in range(IDX_HEADS // 2):
            r = _nt_dot(ik, iq_stack[hp * 2 * Q_BLOCK:(hp + 1) * 2 * Q_BLOCK])
            for u in range(2):
                h = 2 * hp + u
                rel = jnp.maximum(r[:, u * Q_BLOCK:(u + 1) * Q_BLOCK] * (IDX_DIM ** -0.5), 0.0)
                score = score + rel * iw_t[IDX_DIM + h:IDX_DIM + h + 1, :]
        allowed = ((k0 + sub) >> 6) <= q_chunk
        score = jnp.where(score == 0.0, 0.0, score)
        score = jnp.where(allowed, score, -jnp.inf)
        bits = lax.bitcast_convert_type(score, jnp.int32)
        key = bits ^ ((bits >> 31) & 0x7FFFFFFF)
        key_ref[k0:k0 + SEARCH_ROWS, :] = key
        hi_ref[k0:k0 + SEARCH_ROWS, :] = (key >> 16).astype(jnp.int16)
        lo_ref[k0:k0 + SEARCH_ROWS, :] = ((key & 0xFFFF) - HALF_RANGE).astype(jnp.int16)

    def count(pred):
        accs = [jnp.zeros((8, Q_BLOCK), F32)] * 4
        for r in range(n_keys // 8):
            accs[r % 4] = accs[r % 4] + jnp.where(pred(key_ref[8 * r:8 * (r + 1), :]), 1.0, 0.0)
        return jnp.sum(accs[0] + accs[1] + accs[2] + accs[3], axis=0, keepdims=True)

    one16 = jnp.ones((16, Q_BLOCK), jnp.int16)
    zero16 = jnp.zeros((16, Q_BLOCK), jnp.int16)

    def search16(ref):
        def rnd(i, base):
            cand = base + jnp.left_shift(jnp.int32(1), 15 - i)
            cand16 = jnp.broadcast_to(cand, (16, Q_BLOCK)).astype(jnp.int16)
            accs = [zero16] * 4
            for r in range(n_pack):
                accs[r % 4] = accs[r % 4] + jnp.where(ref[16 * r:16 * (r + 1), :] >= cand16, one16, zero16)
            total = (accs[0] + accs[1]) + (accs[2] + accs[3])
            cnt = jnp.sum(total.astype(jnp.int32), axis=0, keepdims=True)
            return jnp.where(cnt >= TOPK, cand, base)
        return lax.fori_loop(0, 16, rnd, jnp.full((1, Q_BLOCK), -HALF_RANGE, jnp.int32))

    top = search16(hi_ref)
    top16 = jnp.broadcast_to(top, (16, Q_BLOCK)).astype(jnp.int16)
    for r in range(n_pack):
        rows = slice(16 * r, 16 * (r + 1))
        hi = hi_ref[rows, :]
        lo_ref[rows, :] = jnp.where(hi > top16, jnp.int16(HALF_RANGE - 1),
                                    jnp.where(hi == top16, lo_ref[rows, :], jnp.int16(-HALF_RANGE)))
    thr = (top << 16) | (search16(lo_ref) + HALF_RANGE)
    thr8 = jnp.broadcast_to(thr, (8, Q_BLOCK))
    cnt_gt = count(lambda kk: kk > thr8)
    cnt_ge = count(lambda kk: kk >= thr8)
    tie_cols = (cnt_ge > TOPK) & (thr > NEG_INF_KEY)
    has_tie = jnp.max(jnp.where(tie_cols, 1.0, 0.0)) > 0.0

    for kb in range(n_blk):
        kk = key_ref[kb * Q_BLOCK:(kb + 1) * Q_BLOCK, :]
        bias_ref[kb * Q_BLOCK:(kb + 1) * Q_BLOCK, :] = jnp.where((kk >= thr) & (kk > NEG_INF_KEY), 0.0, -jnp.inf).T

    @pl.when(has_tie)
    def _():
        need = TOPK - cnt_gt
        row = lax.broadcasted_iota(jnp.int32, (Q_BLOCK, Q_BLOCK), 0)
        col = lax.broadcasted_iota(jnp.int32, (Q_BLOCK, Q_BLOCK), 1)
        lower = jnp.where(col < row, 1.0, 0.0).astype(BF16)

        def tie_block(kb, seen):
            k0 = pl.multiple_of(kb * Q_BLOCK, Q_BLOCK)
            kk = key_ref[pl.ds(k0, Q_BLOCK), :]
            eq = jnp.where(kk == thr, 1.0, 0.0)
            before = jnp.dot(lower, eq.astype(BF16), preferred_element_type=F32) + seen
            keep = (kk > thr) | ((kk == thr) & (before < need))
            bias_ref[pl.ds(k0, Q_BLOCK), :] = jnp.where(keep & (kk > NEG_INF_KEY), 0.0, -jnp.inf).T
            return seen + jnp.sum(eq, axis=0, keepdims=True)

        lax.fori_loop(0, n_blk, tie_block, jnp.zeros((1, Q_BLOCK), F32))


def _dsa_attend(q_ref, kv_ref, bias_ref, o_ref, n_keys):
    q = q_ref[...]
    k = kv_ref[0:n_keys, 0:A_HEAD_DIM]
    v = kv_ref[0:n_keys, A_HEAD_DIM:2 * A_HEAD_DIM]
    bias = jnp.concatenate([bias_ref[c * Q_BLOCK:(c + 1) * Q_BLOCK, :] for c in range(n_keys // Q_BLOCK)],
                           axis=1)
    for h in range(A_HEADS):
        logits = _nt_dot(q[:, h * A_HEAD_DIM:(h + 1) * A_HEAD_DIM], k) * (A_HEAD_DIM ** -0.5) + bias
        m = jnp.max(logits, axis=-1, keepdims=True)
        p = jnp.exp(logits - m)
        l = jnp.sum(p, axis=-1, keepdims=True)
        o = jnp.dot(p.astype(BF16), v, preferred_element_type=F32) / l
        o_ref[:, h * A_HEAD_DIM:(h + 1) * A_HEAD_DIM] = o.astype(o_ref.dtype)


def _dsa_kernel(q_ref, kv_ref, iq_ref, ikw_ref, o_ref, key_ref, hi_ref, lo_ref, bias_ref):
    j = pl.program_id(1)
    for v in range(KV_VARIANTS):
        @pl.when(j // (N_QB // KV_VARIANTS) == v)
        def _(v=v):
            n_keys = KV_STEP * (v + 1)
            n_scored = -(-n_keys // SEARCH_ROWS) * SEARCH_ROWS
            _dsa_select(iq_ref, ikw_ref, key_ref, hi_ref, lo_ref, bias_ref, n_scored)
            _dsa_attend(q_ref, kv_ref, bias_ref, o_ref, n_keys)


def _dsa_attention(za, zi, bsz):
    return pl.pallas_call(
        _dsa_kernel,
        grid=(bsz, N_QB),
        in_specs=[
            pl.BlockSpec((None, Q_BLOCK, 512), lambda b, j: (b, j, 0)),
            pl.BlockSpec((None, SEQ, 256), lambda b, j: (b, 0, 2)),
            pl.BlockSpec((None, Q_BLOCK, 512), lambda b, j: (b, j, 0)),
            pl.BlockSpec((None, SEQ, LANES), lambda b, j: (b, 0, 4)),
        ],
        out_specs=pl.BlockSpec((None, Q_BLOCK, 512), lambda b, j: (b, j, 0)),
        out_shape=jax.ShapeDtypeStruct((bsz, SEQ, 512), BF16),
        scratch_shapes=[pltpu.VMEM((SEQ, Q_BLOCK), jnp.int32),
                        pltpu.VMEM((SEQ, Q_BLOCK), jnp.int16),
                        pltpu.VMEM((SEQ, Q_BLOCK), jnp.int16),
                        pltpu.VMEM((SEQ, Q_BLOCK), F32)],
        compiler_params=_cparams("parallel", "arbitrary"),
    )(za, za, zi, zi)


def _mla_body(q_ref, kv_ref, kr_ref, o_ref, n_keys):
    j = pl.program_id(1)
    mask = _chunk_causal_mask(j, n_keys)
    kr = kr_ref[0:n_keys, 0:B_ROPE]
    scale = (B_NOPE + B_ROPE) ** -0.5
    for h in range(B_HEADS):
        qcat = jnp.concatenate(
            [q_ref[:, h * B_NOPE:(h + 1) * B_NOPE],
             q_ref[:, B_HEADS * B_NOPE + h * B_ROPE:B_HEADS * B_NOPE + (h + 1) * B_ROPE]], axis=1)
        kcat = jnp.concatenate([kv_ref[0:n_keys, h * B_NOPE:(h + 1) * B_NOPE], kr], axis=1)
        vv = kv_ref[0:n_keys, B_HEADS * B_NOPE + h * B_V:B_HEADS * B_NOPE + (h + 1) * B_V]
        s = _nt_dot(qcat, kcat) * scale
        s = jnp.where(mask, s, -jnp.inf)
        m = jnp.max(s, axis=-1, keepdims=True)
        p = jnp.exp(s - m)
        l = jnp.sum(p, axis=-1, keepdims=True)
        o = jnp.dot(p.astype(BF16), vv, preferred_element_type=F32) / l
        o_ref[:, h * B_V:(h + 1) * B_V] = o.astype(o_ref.dtype)


def _mla_kernel(q_ref, kv_ref, kr_ref, o_ref):
    j = pl.program_id(1)
    for v in range(KV_VARIANTS):
        @pl.when(j // (N_QB // KV_VARIANTS) == v)
        def _(v=v):
            _mla_body(q_ref, kv_ref, kr_ref, o_ref, KV_STEP * (v + 1))


def _mla_attention(qb, kvb, kr, bsz):
    return pl.pallas_call(
        _mla_kernel,
        grid=(bsz, N_QB),
        in_specs=[
            pl.BlockSpec((None, Q_BLOCK, 768), lambda b, j: (b, j, 0)),
            pl.BlockSpec((None, SEQ, 1024), lambda b, j: (b, 0, 0)),
            pl.BlockSpec((None, SEQ, LANES), lambda b, j: (b, 0, 0)),
        ],
        out_specs=pl.BlockSpec((None, Q_BLOCK, 512), lambda b, j: (b, j, 0)),
        out_shape=jax.ShapeDtypeStruct((bsz, SEQ, 512), BF16),
        compiler_params=_cparams("parallel", "arbitrary"),
    )(qb, kvb, kr)


def _band_bias(rel_table):
    n = 2 * REL_CLIP + 1
    period = 2 * n - 1
    heads = rel_table.shape[0]
    ext = jnp.concatenate([rel_table, jnp.broadcast_to(rel_table[:, n - 1:n], (heads, n - 1))], axis=1)
    kj = np.arange(Q_BLOCK)[:, None]
    qi = np.arange(Q_BLOCK)[None, :]
    out = []
    for d in range(C_KEY_BLOCKS):
        base = d * Q_BLOCK + REL_CLIP
        if base - (Q_BLOCK - 1) >= n - 1:
            bias = jnp.broadcast_to(rel_table[:, n - 1][:, None, None], (heads, Q_BLOCK, Q_BLOCK))
        else:
            shifted = jnp.roll(ext, -base, axis=1)
            bias = jnp.tile(shifted, (1, Q_BLOCK))[:, :Q_BLOCK * (period - 1)]
            bias = bias.reshape(heads, Q_BLOCK, period - 1)[:, :, :Q_BLOCK]
        cdiff = 2 * d + qi // CHUNK - kj // CHUNK
        valid = (cdiff >= 0) & (cdiff <= C_LEFT_CHUNKS)
        out.append(jnp.where(valid[None], bias.astype(F32), -jnp.inf))
    return jnp.swapaxes(jnp.stack(out, axis=1), 2, 3)


BAND_QB = 2


def _band_kernel(q_ref, kv_ref, bias_ref, o_ref):
    scale = C_HEAD_DIM ** -0.5
    window = C_KEY_BLOCKS * Q_BLOCK
    for sb in range(BAND_QB):
        j = pl.program_id(1) * BAND_QB + sb
        rows = slice(sb * Q_BLOCK, (sb + 1) * Q_BLOCK)
        first = jnp.maximum(j - (C_KEY_BLOCKS - 1), 0)
        win = pl.ds(pl.multiple_of(first * Q_BLOCK, Q_BLOCK), window)
        for h in range(C_HEADS):
            q = q_ref[rows, h * C_HEAD_DIM:(h + 1) * C_HEAD_DIM]
            k = kv_ref[win, h * C_HEAD_DIM:(h + 1) * C_HEAD_DIM]
            v = kv_ref[win, (C_HEADS + h) * C_HEAD_DIM:(C_HEADS + h + 1) * C_HEAD_DIM]
            s = _nt_dot(q, k) * scale
            parts = []
            for c in range(C_KEY_BLOCKS):
                d = j - (first + c)
                bias = bias_ref[h, jnp.clip(d, 0, C_KEY_BLOCKS - 1)]
                parts.append(jnp.where(d >= 0, s[:, c * Q_BLOCK:(c + 1) * Q_BLOCK] + bias, -jnp.inf))
            s = jnp.concatenate(parts, axis=1)
            m = jnp.max(s, axis=-1, keepdims=True)
            p = jnp.exp(s - m)
            l = jnp.sum(p, axis=-1, keepdims=True)
            o = jnp.dot(p.astype(BF16), v, preferred_element_type=F32) / l
            o_ref[rows, h * C_HEAD_DIM:(h + 1) * C_HEAD_DIM] = o.astype(o_ref.dtype)


def _band_attention(zc, bias, bsz):
    rows = BAND_QB * Q_BLOCK
    return pl.pallas_call(
        _band_kernel,
        grid=(bsz, SEQ // rows),
        in_specs=[
            pl.BlockSpec((None, rows, 512), lambda b, j: (b, j, 2)),
            pl.BlockSpec((None, SEQ, 1024), lambda b, j: (b, 0, 0)),
            pl.BlockSpec((C_HEADS, C_KEY_BLOCKS, Q_BLOCK, Q_BLOCK), lambda b, j: (0, 0, 0, 0)),
        ],
        out_specs=pl.BlockSpec((None, rows, 512), lambda b, j: (b, j, 0)),
        out_shape=jax.ShapeDtypeStruct((bsz, SEQ, 512), BF16),
        compiler_params=_cparams("parallel", "arbitrary"),
    )(zc, zc, bias)


def _merge_kernel(x_ref, g_ref, wgl_ref, oa_ref, ob_ref, oc_ref, wpa_ref, wpb_ref, wpc_ref, wout_ref, o_ref):
    x = x_ref[...]
    ms = jnp.mean(x * x, axis=-1, keepdims=True)
    h = (x * lax.rsqrt(ms + EPS) * g_ref[...]).astype(BF16)
    mix = jnp.zeros(x.shape, F32)
    for i, (o_in, wp) in enumerate(((oa_ref, wpa_ref), (ob_ref, wpb_ref), (oc_ref, wpc_ref))):
        gl = jnp.dot(h, wgl_ref[:, i * D_MODEL:(i + 1) * D_MODEL], preferred_element_type=F32)
        gate = jax.nn.sigmoid(gl)
        mix = mix + gate * jnp.dot(o_in[...], wp[...], preferred_element_type=F32)
    o_ref[...] = x + jnp.dot(mix.astype(BF16), wout_ref[...], preferred_element_type=F32)


def _merge(x, g, wgl, oa, ob, oc, wpa, wpb, wpc, wout, *, tm):
    t, d = x.shape
    row = lambda i: (i, 0)
    fixed = lambda i: (0, 0)
    return pl.pallas_call(
        _merge_kernel,
        grid=(t // tm,),
        in_specs=[
            pl.BlockSpec((tm, d), row),
            pl.BlockSpec((1, d), fixed),
            pl.BlockSpec((d, 3 * d), fixed),
            pl.BlockSpec((tm, 512), row),
            pl.BlockSpec((tm, 512), row),
            pl.BlockSpec((tm, 512), row),
            pl.BlockSpec((512, d), fixed),
            pl.BlockSpec((512, d), fixed),
            pl.BlockSpec((512, d), fixed),
            pl.BlockSpec((d, d), fixed),
        ],
        out_specs=pl.BlockSpec((tm, d), row),
        out_shape=jax.ShapeDtypeStruct((t, d), F32),
        compiler_params=_cparams("parallel"),
    )(x, g.reshape(1, d), wgl, oa, ob, oc, wpa, wpb, wpc, wout)


def _first_argmax(vals, lane):
    m = jnp.max(vals, axis=-1, keepdims=True)
    idx = jnp.min(jnp.where(vals == m, lane, LANES), axis=-1, keepdims=True)
    return m, idx


def _pack_bf16_pairs(h):
    n = h.shape[1] // 2
    bits = lax.bitcast_convert_type(h.astype(jnp.bfloat16).astype(F32), jnp.int32)
    return lax.shift_right_logical(bits[:, :n], 16) | bits[:, n:]


def _unpack_bf16_pairs(w):
    lo = lax.bitcast_convert_type(w << 16, F32).astype(BF16)
    hi = lax.bitcast_convert_type(w & HIGH_HALF, F32).astype(BF16)
    return lo, hi


def _router_kernel(x_ref, g_ref, w_ref, b_ref, hp_ref, meta_ref, seg_ref):
    x = x_ref[...]
    ms = jnp.mean(x * x, axis=-1, keepdims=True)
    h = x * lax.rsqrt(ms + EPS) * g_ref[...]
    hp_ref[...] = _pack_bf16_pairs(h)
    logits = jnp.dot(h, w_ref[...], preferred_element_type=F32, precision=lax.Precision.HIGHEST) + b_ref[...]
    lane = lax.broadcasted_iota(jnp.int32, logits.shape, 1)
    is_grp = (lane >= N_EXPERTS) & (lane < N_EXPERTS + N_GROUPS)
    gl = jnp.where(is_grp, logits, -jnp.inf)
    gmax, gidx = _first_argmax(gl, lane)
    pg = 1.0 / jnp.sum(jnp.exp(gl - gmax), axis=-1, keepdims=True)
    gsel = gidx - N_EXPERTS
    in_grp = (lane >> 3) == gsel
    el = jnp.where(in_grp, logits, -jnp.inf)
    m1, i1 = _first_argmax(el, lane)
    z = jnp.sum(jnp.exp(el - m1), axis=-1, keepdims=True)
    el2 = jnp.where(lane == i1, -jnp.inf, el)
    m2, i2 = _first_argmax(el2, lane)
    pe1 = 1.0 / z
    pe2 = jnp.exp(m2 - m1) / z
    den = pe1 + pe2
    w1 = pg * pe1 / den
    w2 = pg * pe2 / den

    sel1 = lane == i1
    sel2 = lane == i2
    onehot = jnp.where(sel1 | sel2, 1.0, 0.0)
    a = lax.broadcasted_iota(jnp.int32, (LANES, LANES), 0)
    b = lax.broadcasted_iota(jnp.int32, (LANES, LANES), 1)
    lower = jnp.where(b < a, 1.0, 0.0).astype(BF16)
    carry = jnp.zeros((1, LANES), F32)
    ranks = []
    for c in range(MOE_TILE // LANES):
        blk = onehot[c * LANES:(c + 1) * LANES]
        ranks.append(jnp.dot(lower, blk.astype(BF16), preferred_element_type=F32) + carry)
        carry = carry + jnp.sum(blk, axis=0, keepdims=True)
    rank = jnp.concatenate(ranks, axis=0)
    cnt = jnp.broadcast_to(carry, (8, LANES))
    seg = jnp.floor((cnt + 7.0) * 0.125) * 8.0
    lane8 = lax.broadcasted_iota(jnp.int32, (8, LANES), 1)
    scan = seg
    for k in (1, 2, 4, 8, 16, 32, 64):
        scan = scan + jnp.where(lane8 >= k, pltpu.roll(scan, k, 1), 0.0)
    off = scan - seg
    where_row = rank + off[0:1]
    pos1 = jnp.sum(jnp.where(sel1, where_row, 0.0), axis=-1, keepdims=True)
    pos2 = jnp.sum(jnp.where(sel2, where_row, 0.0), axis=-1, keepdims=True)
    meta_ref[...] = (jnp.where(lane == 0, pos1, 0.0) + jnp.where(lane == 1, pos2, 0.0)
                     + jnp.where(lane == 2, w1, 0.0) + jnp.where(lane == 3, w2, 0.0))
    row8 = lax.broadcasted_iota(jnp.int32, (8, LANES), 0)
    seg_ref[...] = jnp.where(row8 == 0, cnt, jnp.where(row8 == 1, off, 0.0)).astype(jnp.int32)


def _router(x, g, w, b):
    t, d = x.shape
    nt = t // MOE_TILE
    return pl.pallas_call(
        _router_kernel,
        grid=(nt,),
        in_specs=[
            pl.BlockSpec((MOE_TILE, d), lambda i: (i, 0)),
            pl.BlockSpec((1, d), lambda i: (0, 0)),
            pl.BlockSpec((d, LANES), lambda i: (0, 0)),
            pl.BlockSpec((1, LANES), lambda i: (0, 0)),
        ],
        out_specs=[
            pl.BlockSpec((MOE_TILE, d // 2), lambda i: (i, 0)),
            pl.BlockSpec((MOE_TILE, LANES), lambda i: (i, 0)),
            pl.BlockSpec((None, 8, LANES), lambda i: (i, 0, 0)),
        ],
        out_shape=[
            jax.ShapeDtypeStruct((t, d // 2), jnp.int32),
            jax.ShapeDtypeStruct((t, LANES), F32),
            jax.ShapeDtypeStruct((nt, 8, LANES), jnp.int32),
        ],
        compiler_params=_cparams("parallel"),
    )(x, g.reshape(1, d), w, b)


def _scatter_kernel(pos_ref, hp_ref, xs_ref):
    xs_ref[...] = jnp.zeros_like(xs_ref)

    def body(t, carry):
        row = hp_ref[pl.ds(t, 1), :]
        xs_ref[pl.ds(pos_ref[0, t], 1), :] = row
        xs_ref[pl.ds(pos_ref[0, MOE_TILE + t], 1), :] = row
        return carry

    lax.fori_loop(0, MOE_TILE, body, 0, unroll=8)


def _scatter_rows(pos, hp):
    nt = pos.shape[0]
    return pl.pallas_call(
        _scatter_kernel,
        grid=(nt,),
        in_specs=[
            pl.BlockSpec((None, 1, 2 * MOE_TILE), lambda i: (i, 0, 0), memory_space=pltpu.SMEM),
            pl.BlockSpec((MOE_TILE, hp.shape[1]), lambda i: (i, 0)),
        ],
        out_specs=pl.BlockSpec((None, XS_ROWS, hp.shape[1]), lambda i: (i, 0, 0)),
        out_shape=jax.ShapeDtypeStruct((nt, XS_ROWS, hp.shape[1]), jnp.int32),
        compiler_params=_cparams("parallel"),
    )(pos, hp)


CAST_EXPERTS = 4


def _cast_kernel(w_ref, o_ref):
    o_ref[...] = w_ref[...].astype(o_ref.dtype)


def _to_bf16(w):
    n_l, n_e, a, b = w.shape
    spec = pl.BlockSpec((None, CAST_EXPERTS, a, b), lambda l, e: (l, e, 0, 0))
    return pl.pallas_call(
        _cast_kernel,
        grid=(n_l, n_e // CAST_EXPERTS),
        in_specs=[spec],
        out_specs=spec,
        out_shape=jax.ShapeDtypeStruct(w.shape, BF16),
        compiler_params=_cparams("parallel", "parallel"),
    )(w)


def _experts_kernel(cnt_ref, off_ref, xs_ref, wg_ref, wu_ref, wd_ref, ys_ref):
    i = pl.program_id(0)
    e = pl.program_id(1)
    half = D_MODEL // 2

    @pl.when(e == 0)
    def _():
        ys_ref[2 * MOE_TILE:XS_ROWS, :] = jnp.zeros((XS_ROWS - 2 * MOE_TILE, D_MODEL), F32)

    n = cnt_ref[i, e]
    off = off_ref[i, e]

    def body(c, carry):
        start = pl.multiple_of(off + c * EXPERT_CHUNK, 8)
        lo, hi = _unpack_bf16_pairs(xs_ref[pl.ds(start, EXPERT_CHUNK), :])
        a = (jnp.dot(lo, wg_ref[0:half], preferred_element_type=F32)
             + jnp.dot(hi, wg_ref[half:D_MODEL], preferred_element_type=F32))
        u = (jnp.dot(lo, wu_ref[0:half], preferred_element_type=F32)
             + jnp.dot(hi, wu_ref[half:D_MODEL], preferred_element_type=F32))
        hh = (a * jax.nn.sigmoid(a)) * u
        ys_ref[pl.ds(start, EXPERT_CHUNK), :] = jnp.dot(hh.astype(BF16), wd_ref[...], preferred_element_type=F32)
        return carry

    lax.fori_loop(0, (n + EXPERT_CHUNK - 1) // EXPERT_CHUNK, body, 0)


def _experts(cnt, off, xs, wg, wu, wd, layer):
    nt = xs.shape[0]
    d = D_MODEL
    grid_spec = pltpu.PrefetchScalarGridSpec(
        num_scalar_prefetch=2,
        grid=(nt, N_EXPERTS),
        in_specs=[
            pl.BlockSpec((None, XS_ROWS, d // 2), lambda i, e, c, o: (i, 0, 0), pipeline_mode=pl.Buffered(1)),
            pl.BlockSpec((None, None, d, D_EXPERT), lambda i, e, c, o: (layer, e, 0, 0)),
            pl.BlockSpec((None, None, d, D_EXPERT), lambda i, e, c, o: (layer, e, 0, 0)),
            pl.BlockSpec((None, None, D_EXPERT, d), lambda i, e, c, o: (layer, e, 0, 0)),
        ],
        out_specs=pl.BlockSpec((None, XS_ROWS, d), lambda i, e, c, o: (i, 0, 0)),
    )
    return pl.pallas_call(
        _experts_kernel,
        grid_spec=grid_spec,
        out_shape=jax.ShapeDtypeStruct((nt, XS_ROWS, d), F32),
        compiler_params=_cparams("parallel", "arbitrary"),
    )(cnt, off, xs, wg, wu, wd)


def _combine_kernel(pos_ref, wt_ref, x_ref, ys_ref, g_ref, o_ref, *, final_norm):
    s = pl.program_id(1)

    def body(tl, carry):
        t = s * COMBINE_ROWS + tl
        y = (ys_ref[pl.ds(pos_ref[0, t], 1), :] * wt_ref[0, t]
             + ys_ref[pl.ds(pos_ref[0, MOE_TILE + t], 1), :] * wt_ref[0, MOE_TILE + t])
        o_ref[pl.ds(tl, 1), :] = x_ref[pl.ds(tl, 1), :] + y
        return carry

    lax.fori_loop(0, COMBINE_ROWS, body, 0, unroll=8)
    if final_norm:
        o_ref[...] = _rms(o_ref[...], g_ref[...])


def _combine(pos, wt, x, ys, final_g):
    t, d = x.shape
    nt = pos.shape[0]
    sub = MOE_TILE // COMBINE_ROWS
    g = jnp.ones((d,), F32) if final_g is None else final_g
    return pl.pallas_call(
        functools.partial(_combine_kernel, final_norm=final_g is not None),
        grid=(nt, sub),
        in_specs=[
            pl.BlockSpec((None, 1, 2 * MOE_TILE), lambda i, s: (i, 0, 0), memory_space=pltpu.SMEM),
            pl.BlockSpec((None, 1, 2 * MOE_TILE), lambda i, s: (i, 0, 0), memory_space=pltpu.SMEM),
            pl.BlockSpec((COMBINE_ROWS, d), lambda i, s: (i * sub + s, 0)),
            pl.BlockSpec((None, XS_ROWS, d), lambda i, s: (i, 0, 0)),
            pl.BlockSpec((1, d), lambda i, s: (0, 0)),
        ],
        out_specs=pl.BlockSpec((COMBINE_ROWS, d), lambda i, s: (i * sub + s, 0)),
        out_shape=jax.ShapeDtypeStruct((t, d), F32),
        compiler_params=_cparams("parallel", "arbitrary"),
    )(pos, wt, x, ys, g.reshape(1, d))


def _moe(x, g, w_r, b_r, wg, wu, wd, layer, final_g=None):
    t = x.shape[0]
    nt = t // MOE_TILE
    hp, meta, seg = _router(x, g, w_r, b_r)
    pair_major = lambda m: m.reshape(nt, MOE_TILE, 2).transpose(0, 2, 1).reshape(nt, 1, 2 * MOE_TILE)
    pos = pair_major(meta[:, 0:2].astype(jnp.int32))
    wt = pair_major(meta[:, 2:4])
    xs = _scatter_rows(pos, hp)
    ys = _experts(seg[:, 0, :N_EXPERTS], seg[:, 1, :N_EXPERTS], xs, wg, wu, wd, layer)
    return _combine(pos, wt, x, ys, final_g)


def _pad_cols(w, n):
    return jnp.pad(w, ((0, 0), (0, n - w.shape[1])))


def kernel(x, attn_norm_g, w_in, b_q_norm_g, b_w_uq, b_kv_norm_g, b_w_ukv, c_rel_bias, w_proj_a, w_proj_b, w_proj_c, w_out, ffn_norm_g, w_group, b_group, w_router, b_router, w_gate, w_up, w_down, final_norm_g):
    bsz, seq, d = x.shape
    assert (seq, d) == (SEQ, D_MODEL)
    t = bsz * seq
    depth = w_in.shape[0]
    tm = 512

    tab_a = _rope_table(A_ROT, A_HEAD_DIM)
    tab_i = _rope_table(IDX_ROT, IDX_DIM)
    tab_i_half = _rope_table(IDX_ROT, IDX_DIM, active_lanes=IDX_DIM)
    tab_b = _rope_table(B_ROPE, B_ROPE)
    tab_b_half = _rope_table(B_ROPE, B_ROPE, active_lanes=B_ROPE)

    tabs = (tab_a, tab_i, tab_i_half, tab_b, tab_b_half)
    w_gate, w_up, w_down = _to_bf16(w_gate), _to_bf16(w_up), _to_bf16(w_down)
    xf = x.reshape(t, d)
    for l in range(depth):
        w = w_in[l]
        w_a = w[:, 0:768].astype(BF16)
        w_i = _pad_cols(w[:, 768:1352], 640).astype(BF16)
        w_b = _pad_cols(w[:, 1352:1800], 512).astype(BF16)
        w_c = jnp.concatenate([w[:, 2312:3336], w[:, 1800:2312]], axis=1).astype(BF16)
        w_g = w[:, 3336:6408].astype(BF16)
        g_attn = attn_norm_g[l]
        w_uq = b_w_uq[l].reshape(B_Q_RANK, B_HEADS, B_NOPE + B_ROPE)
        w_uq = jnp.concatenate([w_uq[:, :, :B_NOPE].reshape(B_Q_RANK, -1),
                                w_uq[:, :, B_NOPE:].reshape(B_Q_RANK, -1)], axis=1).astype(BF16)
        w_ukv = b_w_ukv[l].reshape(B_KV_RANK, B_HEADS, B_NOPE + B_V)
        w_ukv = jnp.concatenate([w_ukv[:, :, :B_NOPE].reshape(B_KV_RANK, -1),
                                 w_ukv[:, :, B_NOPE:].reshape(B_KV_RANK, -1)], axis=1).astype(BF16)

        za, zi, qb, kvb, kr, zc = _in_proj(xf, g_attn, w_a, w_i, w_b, w_c, b_q_norm_g[l], w_uq,
                                           b_kv_norm_g[l], w_ukv, tabs, tm=tm)
        per_seq = lambda z: z.reshape(bsz, seq, -1)
        o_a = _dsa_attention(per_seq(za), per_seq(zi), bsz)
        o_b = _mla_attention(per_seq(qb), per_seq(kvb), per_seq(kr), bsz)
        o_c = _band_attention(per_seq(zc), _band_bias(c_rel_bias[l]), bsz)

        xf = _merge(xf, g_attn, w_g, o_a.reshape(t, -1), o_b.reshape(t, -1), o_c.reshape(t, -1),
                    w_proj_a[l].astype(BF16), w_proj_b[l].astype(BF16), w_proj_c[l].astype(BF16),
                    w_out[l].astype(BF16), tm=tm)

        w_r = _pad_cols(jnp.concatenate([w_router[l], w_group[l]], axis=1), LANES)
        b_r = _pad_cols(jnp.concatenate([b_router[l], b_group[l]])[None, :], LANES)
        xf = _moe(xf, ffn_norm_g[l], w_r, b_r, w_gate, w_up, w_down, l,
                  final_g=final_norm_g if l == depth - 1 else None)

    return xf.reshape(bsz, seq, d)
```

```python
import functools

import numpy as np
import jax
import jax.numpy as jnp
from jax import lax
from jax.experimental import pallas as pl
from jax.experimental.pallas import tpu as pltpu

F32 = jnp.float32
BF16 = jnp.bfloat16

LANES = 128
D_MODEL = 1024
SEQ = 2048
CHUNK = 64
Q_BLOCK = 128
ROPE_THETA = 500000.0
EPS = 1e-6

A_HEADS = 4
A_HEAD_DIM = 128
A_ROT = 32
IDX_HEADS = 8
IDX_DIM = 64
IDX_ROT = 16
TOPK = 256
B_HEADS = 4
B_NOPE = 128
B_ROPE = 64
B_V = 128
B_Q_RANK = 256
B_KV_RANK = 128
C_HEADS = 4
C_HEAD_DIM = 128
C_LEFT_CHUNKS = 8
REL_CLIP = 128
N_GROUPS = 4
EXPERTS_PER_GROUP = 8
N_EXPERTS = 32
D_EXPERT = 256

C_KEY_BLOCKS = C_LEFT_CHUNKS * CHUNK // Q_BLOCK + 1
N_QB = SEQ // Q_BLOCK
KV_VARIANTS = 8
KV_STEP = SEQ // KV_VARIANTS

VMEM_LIMIT = 56 * 1024 * 1024

MOE_TILE = 2048
EXPERT_CHUNK = 256
XS_ROWS = 2 * MOE_TILE + 2 * EXPERT_CHUNK
assert XS_ROWS >= 2 * MOE_TILE + N_EXPERTS * 7 + EXPERT_CHUNK - 1
COMBINE_ROWS = 512
HIGH_HALF = -65536

CHUNK_SHIFT = CHUNK.bit_length() - 1
GROUP_SHIFT = EXPERTS_PER_GROUP.bit_length() - 1
assert (1 << CHUNK_SHIFT, 1 << GROUP_SHIFT) == (CHUNK, EXPERTS_PER_GROUP)
NEG_INF_KEY = int(np.array(0x807FFFFF, np.uint32).view(np.int32))

NT_DIMS = (((1,), (1,)), ((), ()))


def _nt_dot(a, b):
    return lax.dot_general(a, b, NT_DIMS, preferred_element_type=F32)


def _cparams(*sem):
    return pltpu.CompilerParams(dimension_semantics=sem, vmem_limit_bytes=VMEM_LIMIT)


def _rope_table(rot, period, active_lanes=LANES):
    half = rot // 2
    lane = np.arange(LANES)
    p = lane % period
    first = (p < half) & (lane < active_lanes)
    second = (p >= half) & (p < rot) & (lane < active_lanes)
    idx = np.where(first, p, np.where(second, p - half, 0))
    pos = jnp.arange(SEQ, dtype=F32)
    inv = ROPE_THETA ** (-jnp.arange(0, rot, 2, dtype=F32) / rot)
    ang = pos[:, None] * inv[idx][None, :]
    cos, sin = jnp.cos(ang), jnp.sin(ang)
    c = jnp.where(first | second, cos, 1.0)
    s_prev = jnp.where(second, sin, 0.0)
    s_next = jnp.where(first, -sin, 0.0)
    return jnp.stack([c, s_prev, s_next]).astype(F32)


def _rms(x, g):
    ms = jnp.mean(x * x, axis=-1, keepdims=True)
    return x * lax.rsqrt(ms + EPS) * g


def _rope_tiles(z, tile_tab, tabs, halves):
    out = []
    for c, t in enumerate(tile_tab):
        zt = z[:, c * LANES:(c + 1) * LANES]
        if t >= 0:
            tab, half = tabs[t], halves[t]
            zt = (zt * tab[0] + pltpu.roll(zt, half, 1) * tab[1]
                  + pltpu.roll(zt, LANES - half, 1) * tab[2])
        out.append(zt)
    return out


def _store_tiles(o_ref, tiles):
    for c, zt in enumerate(tiles):
        o_ref[:, c * LANES:(c + 1) * LANES] = zt.astype(o_ref.dtype)


ROPE_HALVES = (A_ROT // 2, IDX_ROT // 2, IDX_ROT // 2, B_ROPE // 2, B_ROPE // 2)


def _in_proj_kernel(x_ref, g_ref, wa_ref, wi_ref, wb_ref, wc_ref, gq_ref, wuq_ref, gkv_ref, wukv_ref,
                    ta_ref, ti_ref, tih_ref, tb_ref, tbh_ref,
                    za_ref, zi_ref, qb_ref, kvb_ref, kr_ref, zc_ref):
    tabs = (ta_ref, ti_ref, tih_ref, tb_ref, tbh_ref)
    rope = functools.partial(_rope_tiles, tabs=tabs, halves=ROPE_HALVES)
    h = _rms(x_ref[...], g_ref[...]).astype(BF16)
    dot = functools.partial(jnp.dot, preferred_element_type=F32)
    _store_tiles(za_ref, rope(dot(h, wa_ref[...]), (0, 0, 0, 0, 0, -1)))
    _store_tiles(zi_ref, rope(dot(h, wi_ref[...]), (1, 1, 1, 1, 2)))
    _store_tiles(zc_ref, rope(dot(h, wc_ref[...]), (-1,) * (3 * C_HEADS)))
    zb = dot(h, wb_ref[...])
    _store_tiles(kr_ref, rope(zb[:, B_Q_RANK + B_KV_RANK:], (4,)))
    cq = _rms(zb[:, :B_Q_RANK], gq_ref[...]).astype(BF16)
    _store_tiles(qb_ref, rope(dot(cq, wuq_ref[...]), (-1, -1, -1, -1, 3, 3)))
    ckv = _rms(zb[:, B_Q_RANK:B_Q_RANK + B_KV_RANK], gkv_ref[...]).astype(BF16)
    _store_tiles(kvb_ref, rope(dot(ckv, wukv_ref[...]), (-1,) * (2 * B_HEADS)))


def _in_proj(x, g, w_a, w_i, w_b, w_c, g_q, w_uq, g_kv, w_ukv, tabs, *, tm):
    t, d = x.shape
    seq_tiles = SEQ // tm
    fixed = lambda i: (0, 0)
    row = lambda i: (i, 0)
    weights = (w_a, w_i, w_b, w_c)
    outs = ((768, BF16), (640, F32), (768, BF16), (1024, BF16), (LANES, BF16), (1536, BF16))
    return pl.pallas_call(
        _in_proj_kernel,
        grid=(t // tm,),
        in_specs=[pl.BlockSpec((tm, d), row), pl.BlockSpec((1, d), fixed)]
        + [pl.BlockSpec(w.shape, fixed) for w in weights]
        + [pl.BlockSpec((1, B_Q_RANK), fixed), pl.BlockSpec(w_uq.shape, fixed),
           pl.BlockSpec((1, B_KV_RANK), fixed), pl.BlockSpec(w_ukv.shape, fixed)]
        + [pl.BlockSpec((3, tm, LANES), lambda i: (0, i % seq_tiles, 0)) for _ in tabs],
        out_specs=[pl.BlockSpec((tm, n), row) for n, _ in outs],
        out_shape=[jax.ShapeDtypeStruct((t, n), dt) for n, dt in outs],
        compiler_params=_cparams("parallel"),
    )(x, g.reshape(1, d), *weights, g_q.reshape(1, -1), w_uq, g_kv.reshape(1, -1), w_ukv, *tabs)


def _chunk_causal_mask(j, n_keys):
    qpos = j * Q_BLOCK + lax.broadcasted_iota(jnp.int32, (Q_BLOCK, n_keys), 0)
    kpos = lax.broadcasted_iota(jnp.int32, (Q_BLOCK, n_keys), 1)
    return (kpos >> CHUNK_SHIFT) <= (qpos >> CHUNK_SHIFT)


SEARCH_ROWS = 256
HALF_RANGE = 1 << 15


def _dsa_select(iq_ref, ikw_ref, key_ref, hi_ref, lo_ref, bias_ref, n_keys):
    j = pl.program_id(1)
    n_blk = n_keys // Q_BLOCK
    n_grp = n_keys // SEARCH_ROWS
    n_pack = n_keys // 16
    row0 = pl.multiple_of(j * Q_BLOCK, Q_BLOCK)
    sub = lax.broadcasted_iota(jnp.int32, (SEARCH_ROWS, Q_BLOCK), 0)
    lane = lax.broadcasted_iota(jnp.int32, (SEARCH_ROWS, Q_BLOCK), 1)
    q_chunk = (row0 + lane) >> CHUNK_SHIFT

    iq = iq_ref[...].astype(BF16)
    iq_stack = jnp.concatenate([iq[:, h * IDX_DIM:(h + 1) * IDX_DIM] for h in range(IDX_HEADS)], axis=0)
    iw_t = ikw_ref[pl.ds(row0, Q_BLOCK), :].T * (IDX_HEADS ** -0.5)

    for g in range(n_grp):
        k0 = g * SEARCH_ROWS
        ik = ikw_ref[k0:k0 + SEARCH_ROWS, 0:IDX_DIM].astype(BF16)
        score = jnp.zeros((SEARCH_ROWS, Q_BLOCK), F32)
        for hp in range(IDX_HEADS // 2):
            r = _nt_dot(ik, iq_stack[hp * 2 * Q_BLOCK:(hp + 1) * 2 * Q_BLOCK])
            for u in range(2):
                h = 2 * hp + u
                rel = jnp.maximum(r[:, u * Q_BLOCK:(u + 1) * Q_BLOCK] * (IDX_DIM ** -0.5), 0.0)
                score = score + rel * iw_t[IDX_DIM + h:IDX_DIM + h + 1, :]
        allowed = ((k0 + sub) >> CHUNK_SHIFT) <= q_chunk
        score = jnp.where(score == 0.0, 0.0, score)
        score = jnp.where(allowed, score, -jnp.inf)
        bits = lax.bitcast_convert_type(score, jnp.int32)
        key = bits ^ ((bits >> 31) & 0x7FFFFFFF)
        key_ref[k0:k0 + SEARCH_ROWS, :] = key
        hi_ref[k0:k0 + SEARCH_ROWS, :] = (key >> 16).astype(jnp.int16)
        lo_ref[k0:k0 + SEARCH_ROWS, :] = ((key & 0xFFFF) - HALF_RANGE).astype(jnp.int16)

    def count(pred):
        accs = [jnp.zeros((8, Q_BLOCK), F32)] * 4
        for r in range(n_keys // 8):
            accs[r % 4] = accs[r % 4] + jnp.where(pred(key_ref[8 * r:8 * (r + 1), :]), 1.0, 0.0)
        return jnp.sum(accs[0] + accs[1] + accs[2] + accs[3], axis=0, keepdims=True)

    one16 = jnp.ones((16, Q_BLOCK), jnp.int16)
    zero16 = jnp.zeros((16, Q_BLOCK), jnp.int16)

    def search16(ref):
        def rnd(i, base):
            cand = base + jnp.left_shift(jnp.int32(1), 15 - i)
            cand16 = jnp.broadcast_to(cand, (16, Q_BLOCK)).astype(jnp.int16)
            accs = [zero16] * 4
            for r in range(n_pack):
                accs[r % 4] = accs[r % 4] + jnp.where(ref[16 * r:16 * (r + 1), :] >= cand16, one16, zero16)
            total = (accs[0] + accs[1]) + (accs[2] + accs[3])
            cnt = jnp.sum(total.astype(jnp.int32), axis=0, keepdims=True)
            return jnp.where(cnt >= TOPK, cand, base)
        return lax.fori_loop(0, 16, rnd, jnp.full((1, Q_BLOCK), -HALF_RANGE, jnp.int32))

    top = search16(hi_ref)
    top16 = jnp.broadcast_to(top, (16, Q_BLOCK)).astype(jnp.int16)
    for r in range(n_pack):
        rows = slice(16 * r, 16 * (r + 1))
        hi = hi_ref[rows, :]
        lo_ref[rows, :] = jnp.where(hi > top16, jnp.int16(HALF_RANGE - 1),
                                    jnp.where(hi == top16, lo_ref[rows, :], jnp.int16(-HALF_RANGE)))
    thr = (top << 16) | (search16(lo_ref) + HALF_RANGE)
    thr8 = jnp.broadcast_to(thr, (8, Q_BLOCK))
    cnt_gt = count(lambda kk: kk > thr8)
    cnt_ge = count(lambda kk: kk >= thr8)
    tie_cols = (cnt_ge > TOPK) & (thr > NEG_INF_KEY)
    has_tie = jnp.max(jnp.where(tie_cols, 1.0, 0.0)) > 0.0

    for kb in range(n_blk):
        kk = key_ref[kb * Q_BLOCK:(kb + 1) * Q_BLOCK, :]
        bias_ref[kb * Q_BLOCK:(kb + 1) * Q_BLOCK, :] = jnp.where((kk >= thr) & (kk > NEG_INF_KEY), 0.0, -jnp.inf).T

    @pl.when(has_tie)
    def _():
        need = TOPK - cnt_gt
        row = lax.broadcasted_iota(jnp.int32, (Q_BLOCK, Q_BLOCK), 0)
        col = lax.broadcasted_iota(jnp.int32, (Q_BLOCK, Q_BLOCK), 1)
        lower = jnp.where(col < row, 1.0, 0.0).astype(BF16)

        def tie_block(kb, seen):
            k0 = pl.multiple_of(kb * Q_BLOCK, Q_BLOCK)
            kk = key_ref[pl.ds(k0, Q_BLOCK), :]
            eq = jnp.where(kk == thr, 1.0, 0.0)
            before = jnp.dot(lower, eq.astype(BF16), preferred_element_type=F32) + seen
            keep = (kk > thr) | ((kk == thr) & (before < need))
            bias_ref[pl.ds(k0, Q_BLOCK), :] = jnp.where(keep & (kk > NEG_INF_KEY), 0.0, -jnp.inf).T
            return seen + jnp.sum(eq, axis=0, keepdims=True)

        lax.fori_loop(0, n_blk, tie_block, jnp.zeros((1, Q_BLOCK), F32))


def _dsa_attend(q_ref, kv_ref, bias_ref, o_ref, n_keys):
    q = q_ref[...]
    k = kv_ref[0:n_keys, 0:A_HEAD_DIM]
    v = kv_ref[0:n_keys, A_HEAD_DIM:2 * A_HEAD_DIM]
    bias = jnp.concatenate([bias_ref[c * Q_BLOCK:(c + 1) * Q_BLOCK, :] for c in range(n_keys // Q_BLOCK)],
                           axis=1)
    for h in range(A_HEADS):
        logits = _nt_dot(q[:, h * A_HEAD_DIM:(h + 1) * A_HEAD_DIM], k) * (A_HEAD_DIM ** -0.5) + bias
        m = jnp.max(logits, axis=-1, keepdims=True)
        p = jnp.exp(logits - m)
        l = jnp.sum(p, axis=-1, keepdims=True)
        o = jnp.dot(p.astype(BF16), v, preferred_element_type=F32) / l
        o_ref[:, h * A_HEAD_DIM:(h + 1) * A_HEAD_DIM] = o.astype(o_ref.dtype)


def _dsa_kernel(q_ref, kv_ref, iq_ref, ikw_ref, o_ref, key_ref, hi_ref, lo_ref, bias_ref):
    j = pl.program_id(1)
    for v in range(KV_VARIANTS):
        @pl.when(j // (N_QB // KV_VARIANTS) == v)
        def _(v=v):
            n_keys = KV_STEP * (v + 1)
            n_scored = -(-n_keys // SEARCH_ROWS) * SEARCH_ROWS
            _dsa_select(iq_ref, ikw_ref, key_ref, hi_ref, lo_ref, bias_ref, n_scored)
            _dsa_attend(q_ref, kv_ref, bias_ref, o_ref, n_keys)


def _dsa_attention(za, zi, bsz):
    return pl.pallas_call(
        _dsa_kernel,
        grid=(bsz, N_QB),
        in_specs=[
            pl.BlockSpec((None, Q_BLOCK, 512), lambda b, j: (b, j, 0)),
            pl.BlockSpec((None, SEQ, 256), lambda b, j: (b, 0, 2)),
            pl.BlockSpec((None, Q_BLOCK, 512), lambda b, j: (b, j, 0)),
            pl.BlockSpec((None, SEQ, LANES), lambda b, j: (b, 0, 4)),
        ],
        out_specs=pl.BlockSpec((None, Q_BLOCK, 512), lambda b, j: (b, j, 0)),
        out_shape=jax.ShapeDtypeStruct((bsz, SEQ, 512), BF16),
        scratch_shapes=[pltpu.VMEM((SEQ, Q_BLOCK), jnp.int32),
                        pltpu.VMEM((SEQ, Q_BLOCK), jnp.int16),
                        pltpu.VMEM((SEQ, Q_BLOCK), jnp.int16),
                        pltpu.VMEM((SEQ, Q_BLOCK), F32)],
        compiler_params=_cparams("parallel", "arbitrary"),
    )(za, za, zi, zi)


def _mla_body(q_ref, kv_ref, kr_ref, o_ref, n_keys):
    j = pl.program_id(1)
    mask = _chunk_causal_mask(j, n_keys)
    kr = kr_ref[0:n_keys, 0:B_ROPE]
    scale = (B_NOPE + B_ROPE) ** -0.5
    for h in range(B_HEADS):
        qcat = jnp.concatenate(
            [q_ref[:, h * B_NOPE:(h + 1) * B_NOPE],
             q_ref[:, B_HEADS * B_NOPE + h * B_ROPE:B_HEADS * B_NOPE + (h + 1) * B_ROPE]], axis=1)
        kcat = jnp.concatenate([kv_ref[0:n_keys, h * B_NOPE:(h + 1) * B_NOPE], kr], axis=1)
        vv = kv_ref[0:n_keys, B_HEADS * B_NOPE + h * B_V:B_HEADS * B_NOPE + (h + 1) * B_V]
        s = _nt_dot(qcat, kcat) * scale
        s = jnp.where(mask, s, -jnp.inf)
        m = jnp.max(s, axis=-1, keepdims=True)
        p = jnp.exp(s - m)
        l = jnp.sum(p, axis=-1, keepdims=True)
        o = jnp.dot(p.astype(BF16), vv, preferred_element_type=F32) / l
        o_ref[:, h * B_V:(h + 1) * B_V] = o.astype(o_ref.dtype)


def _mla_kernel(q_ref, kv_ref, kr_ref, o_ref):
    j = pl.program_id(1)
    for v in range(KV_VARIANTS):
        @pl.when(j // (N_QB // KV_VARIANTS) == v)
        def _(v=v):
            _mla_body(q_ref, kv_ref, kr_ref, o_ref, KV_STEP * (v + 1))


def _mla_attention(qb, kvb, kr, bsz):
    return pl.pallas_call(
        _mla_kernel,
        grid=(bsz, N_QB),
        in_specs=[
            pl.BlockSpec((None, Q_BLOCK, 768), lambda b, j: (b, j, 0)),
            pl.BlockSpec((None, SEQ, 1024), lambda b, j: (b, 0, 0)),
            pl.BlockSpec((None, SEQ, LANES), lambda b, j: (b, 0, 0)),
        ],
        out_specs=pl.BlockSpec((None, Q_BLOCK, 512), lambda b, j: (b, j, 0)),
        out_shape=jax.ShapeDtypeStruct((bsz, SEQ, 512), BF16),
        compiler_params=_cparams("parallel", "arbitrary"),
    )(qb, kvb, kr)


def _band_bias(rel_table):
    n = 2 * REL_CLIP + 1
    period = 2 * n - 1
    heads = rel_table.shape[0]
    ext = jnp.concatenate([rel_table, jnp.broadcast_to(rel_table[:, n - 1:n], (heads, n - 1))], axis=1)
    kj = np.arange(Q_BLOCK)[:, None]
    qi = np.arange(Q_BLOCK)[None, :]
    out = []
    for d in range(C_KEY_BLOCKS):
        base = d * Q_BLOCK + REL_CLIP
        if base - (Q_BLOCK - 1) >= n - 1:
            bias = jnp.broadcast_to(rel_table[:, n - 1][:, None, None], (heads, Q_BLOCK, Q_BLOCK))
        else:
            shifted = jnp.roll(ext, -base, axis=1)
            bias = jnp.tile(shifted, (1, Q_BLOCK))[:, :Q_BLOCK * (period - 1)]
            bias = bias.reshape(heads, Q_BLOCK, period - 1)[:, :, :Q_BLOCK]
        cdiff = 2 * d + qi // CHUNK - kj // CHUNK
        valid = (cdiff >= 0) & (cdiff <= C_LEFT_CHUNKS)
        out.append(jnp.where(valid[None], bias.astype(F32), -jnp.inf))
    return jnp.swapaxes(jnp.stack(out, axis=1), 2, 3)


BAND_QB = 4


def _band_kernel(q_ref, kv_ref, bias_ref, o_ref):
    scale = C_HEAD_DIM ** -0.5
    window = C_KEY_BLOCKS * Q_BLOCK
    for sb in range(BAND_QB):
        j = pl.program_id(1) * BAND_QB + sb
        rows = slice(sb * Q_BLOCK, (sb + 1) * Q_BLOCK)
        first = jnp.maximum(j - (C_KEY_BLOCKS - 1), 0)
        win = pl.ds(pl.multiple_of(first * Q_BLOCK, Q_BLOCK), window)
        for h in range(C_HEADS):
            q = q_ref[rows, h * C_HEAD_DIM:(h + 1) * C_HEAD_DIM]
            k = kv_ref[win, h * C_HEAD_DIM:(h + 1) * C_HEAD_DIM]
            v = kv_ref[win, (C_HEADS + h) * C_HEAD_DIM:(C_HEADS + h + 1) * C_HEAD_DIM]
            s = _nt_dot(q, k) * scale
            parts = []
            for c in range(C_KEY_BLOCKS):
                d = j - (first + c)
                bias = bias_ref[h, jnp.clip(d, 0, C_KEY_BLOCKS - 1)]
                parts.append(jnp.where(d >= 0, s[:, c * Q_BLOCK:(c + 1) * Q_BLOCK] + bias, -jnp.inf))
            s = jnp.concatenate(parts, axis=1)
            m = jnp.max(s, axis=-1, keepdims=True)
            p = jnp.exp(s - m)
            l = jnp.sum(p, axis=-1, keepdims=True)
            o = jnp.dot(p.astype(BF16), v, preferred_element_type=F32) / l
            o_ref[rows, h * C_HEAD_DIM:(h + 1) * C_HEAD_DIM] = o.astype(o_ref.dtype)


def _band_attention(zc, bias, bsz):
    rows = BAND_QB * Q_BLOCK
    return pl.pallas_call(
        _band_kernel,
        grid=(bsz, SEQ // rows),
        in_specs=[
            pl.BlockSpec((None, rows, 512), lambda b, j: (b, j, 2)),
            pl.BlockSpec((None, SEQ, 1024), lambda b, j: (b, 0, 0)),
            pl.BlockSpec((C_HEADS, C_KEY_BLOCKS, Q_BLOCK, Q_BLOCK), lambda b, j: (0, 0, 0, 0)),
        ],
        out_specs=pl.BlockSpec((None, rows, 512), lambda b, j: (b, j, 0)),
        out_shape=jax.ShapeDtypeStruct((bsz, SEQ, 512), BF16),
        compiler_params=_cparams("parallel", "arbitrary"),
    )(zc, zc, bias)


def _merge_kernel(x_ref, g_ref, wgl_ref, oa_ref, ob_ref, oc_ref, wpa_ref, wpb_ref, wpc_ref, wout_ref, o_ref):
    x = x_ref[...]
    ms = jnp.mean(x * x, axis=-1, keepdims=True)
    h = (x * lax.rsqrt(ms + EPS) * g_ref[...]).astype(BF16)
    mix = jnp.zeros(x.shape, F32)
    for i, (o_in, wp) in enumerate(((oa_ref, wpa_ref), (ob_ref, wpb_ref), (oc_ref, wpc_ref))):
        gl = jnp.dot(h, wgl_ref[:, i * D_MODEL:(i + 1) * D_MODEL], preferred_element_type=F32)
        gate = jax.nn.sigmoid(gl)
        mix = mix + gate * jnp.dot(o_in[...], wp[...], preferred_element_type=F32)
    o_ref[...] = x + jnp.dot(mix.astype(BF16), wout_ref[...], preferred_element_type=F32)


def _merge(x, g, wgl, oa, ob, oc, wpa, wpb, wpc, wout, *, tm):
    t, d = x.shape
    row = lambda i: (i, 0)
    fixed = lambda i: (0, 0)
    return pl.pallas_call(
        _merge_kernel,
        grid=(t // tm,),
        in_specs=[
            pl.BlockSpec((tm, d), row),
            pl.BlockSpec((1, d), fixed),
            pl.BlockSpec((d, 3 * d), fixed),
            pl.BlockSpec((tm, 512), row),
            pl.BlockSpec((tm, 512), row),
            pl.BlockSpec((tm, 512), row),
            pl.BlockSpec((512, d), fixed),
            pl.BlockSpec((512, d), fixed),
            pl.BlockSpec((512, d), fixed),
            pl.BlockSpec((d, d), fixed),
        ],
        out_specs=pl.BlockSpec((tm, d), row),
        out_shape=jax.ShapeDtypeStruct((t, d), F32),
        compiler_params=_cparams("parallel"),
    )(x, g.reshape(1, d), wgl, oa, ob, oc, wpa, wpb, wpc, wout)


def _first_argmax(vals, lane):
    m = jnp.max(vals, axis=-1, keepdims=True)
    idx = jnp.min(jnp.where(vals == m, lane, LANES), axis=-1, keepdims=True)
    return m, idx


def _pack_bf16_pairs(h):
    n = h.shape[1] // 2
    bits = lax.bitcast_convert_type(h.astype(jnp.bfloat16).astype(F32), jnp.int32)
    return lax.shift_right_logical(bits[:, :n], 16) | bits[:, n:]


def _unpack_bf16_pairs(w):
    lo = lax.bitcast_convert_type(w << 16, F32).astype(BF16)
    hi = lax.bitcast_convert_type(w & HIGH_HALF, F32).astype(BF16)
    return lo, hi


def _router_kernel(x_ref, g_ref, w_ref, b_ref, hp_ref, meta_ref, seg_ref):
    x = x_ref[...]
    ms = jnp.mean(x * x, axis=-1, keepdims=True)
    h = x * lax.rsqrt(ms + EPS) * g_ref[...]
    hp_ref[...] = _pack_bf16_pairs(h)
    logits = jnp.dot(h, w_ref[...], preferred_element_type=F32, precision=lax.Precision.HIGHEST) + b_ref[...]
    lane = lax.broadcasted_iota(jnp.int32, logits.shape, 1)
    is_grp = (lane >= N_EXPERTS) & (lane < N_EXPERTS + N_GROUPS)
    gl = jnp.where(is_grp, logits, -jnp.inf)
    gmax, gidx = _first_argmax(gl, lane)
    pg = 1.0 / jnp.sum(jnp.exp(gl - gmax), axis=-1, keepdims=True)
    gsel = gidx - N_EXPERTS
    in_grp = (lane >> GROUP_SHIFT) == gsel
    el = jnp.where(in_grp, logits, -jnp.inf)
    m1, i1 = _first_argmax(el, lane)
    z = jnp.sum(jnp.exp(el - m1), axis=-1, keepdims=True)
    el2 = jnp.where(lane == i1, -jnp.inf, el)
    m2, i2 = _first_argmax(el2, lane)
    pe1 = 1.0 / z
    pe2 = jnp.exp(m2 - m1) / z
    den = pe1 + pe2
    w1 = pg * pe1 / den
    w2 = pg * pe2 / den

    sel1 = lane == i1
    sel2 = lane == i2
    onehot = jnp.where(sel1 | sel2, 1.0, 0.0)
    a = lax.broadcasted_iota(jnp.int32, (LANES, LANES), 0)
    b = lax.broadcasted_iota(jnp.int32, (LANES, LANES), 1)
    lower = jnp.where(b < a, 1.0, 0.0).astype(BF16)
    carry = jnp.zeros((1, LANES), F32)
    ranks = []
    for c in range(MOE_TILE // LANES):
        blk = onehot[c * LANES:(c + 1) * LANES]
        ranks.append(jnp.dot(lower, blk.astype(BF16), preferred_element_type=F32) + carry)
        carry = carry + jnp.sum(blk, axis=0, keepdims=True)
    rank = jnp.concatenate(ranks, axis=0)
    cnt = jnp.broadcast_to(carry, (8, LANES))
    seg = jnp.floor((cnt + 7.0) * 0.125) * 8.0
    lane8 = lax.broadcasted_iota(jnp.int32, (8, LANES), 1)
    scan = seg
    for k in (1, 2, 4, 8, 16, 32, 64):
        scan = scan + jnp.where(lane8 >= k, pltpu.roll(scan, k, 1), 0.0)
    off = scan - seg
    where_row = rank + off[0:1]
    pos1 = jnp.sum(jnp.where(sel1, where_row, 0.0), axis=-1, keepdims=True)
    pos2 = jnp.sum(jnp.where(sel2, where_row, 0.0), axis=-1, keepdims=True)
    meta_ref[...] = (jnp.where(lane == 0, pos1, 0.0) + jnp.where(lane == 1, pos2, 0.0)
                     + jnp.where(lane == 2, w1, 0.0) + jnp.where(lane == 3, w2, 0.0))
    row8 = lax.broadcasted_iota(jnp.int32, (8, LANES), 0)
    seg_ref[...] = jnp.where(row8 == 0, cnt, jnp.where(row8 == 1, off, 0.0)).astype(jnp.int32)


def _router(x, g, w, b):
    t, d = x.shape
    nt = t // MOE_TILE
    return pl.pallas_call(
        _router_kernel,
        grid=(nt,),
        in_specs=[
            pl.BlockSpec((MOE_TILE, d), lambda i: (i, 0)),
            pl.BlockSpec((1, d), lambda i: (0, 0)),
            pl.BlockSpec((d, LANES), lambda i: (0, 0)),
            pl.BlockSpec((1, LANES), lambda i: (0, 0)),
        ],
        out_specs=[
            pl.BlockSpec((MOE_TILE, d // 2), lambda i: (i, 0)),
            pl.BlockSpec((MOE_TILE, LANES), lambda i: (i, 0)),
            pl.BlockSpec((None, 8, LANES), lambda i: (i, 0, 0)),
        ],
        out_shape=[
            jax.ShapeDtypeStruct((t, d // 2), jnp.int32),
            jax.ShapeDtypeStruct((t, LANES), F32),
            jax.ShapeDtypeStruct((nt, 8, LANES), jnp.int32),
        ],
        compiler_params=_cparams("parallel"),
    )(x, g.reshape(1, d), w, b)


def _scatter_kernel(pos_ref, hp_ref, xs_ref):
    xs_ref[...] = jnp.zeros_like(xs_ref)

    def body(t, carry):
        row = hp_ref[pl.ds(t, 1), :]
        xs_ref[pl.ds(pos_ref[0, t], 1), :] = row
        xs_ref[pl.ds(pos_ref[0, MOE_TILE + t], 1), :] = row
        return carry

    lax.fori_loop(0, MOE_TILE, body, 0, unroll=8)


def _scatter_rows(pos, hp):
    nt = pos.shape[0]
    return pl.pallas_call(
        _scatter_kernel,
        grid=(nt,),
        in_specs=[
            pl.BlockSpec((None, 1, 2 * MOE_TILE), lambda i: (i, 0, 0), memory_space=pltpu.SMEM),
            pl.BlockSpec((MOE_TILE, hp.shape[1]), lambda i: (i, 0)),
        ],
        out_specs=pl.BlockSpec((None, XS_ROWS, hp.shape[1]), lambda i: (i, 0, 0)),
        out_shape=jax.ShapeDtypeStruct((nt, XS_ROWS, hp.shape[1]), jnp.int32),
        compiler_params=_cparams("parallel"),
    )(pos, hp)


CAST_EXPERTS = 4


def _cast_kernel(w_ref, o_ref):
    o_ref[...] = w_ref[...].astype(o_ref.dtype)


def _to_bf16(w):
    n_l, n_e, a, b = w.shape
    spec = pl.BlockSpec((None, CAST_EXPERTS, a, b), lambda l, e: (l, e, 0, 0))
    return pl.pallas_call(
        _cast_kernel,
        grid=(n_l, n_e // CAST_EXPERTS),
        in_specs=[spec],
        out_specs=spec,
        out_shape=jax.ShapeDtypeStruct(w.shape, BF16),
        compiler_params=_cparams("parallel", "parallel"),
    )(w)


def _experts_kernel(cnt_ref, off_ref, xs_ref, wg_ref, wu_ref, wd_ref, ys_ref):
    i = pl.program_id(0)
    e = pl.program_id(1)
    half = D_MODEL // 2

    @pl.when(e == 0)
    def _():
        ys_ref[2 * MOE_TILE:XS_ROWS, :] = jnp.zeros((XS_ROWS - 2 * MOE_TILE, D_MODEL), F32)

    n = cnt_ref[i, e]
    off = off_ref[i, e]

    def body(c, carry):
        start = pl.multiple_of(off + c * EXPERT_CHUNK, 8)
        lo, hi = _unpack_bf16_pairs(xs_ref[pl.ds(start, EXPERT_CHUNK), :])
        a = (jnp.dot(lo, wg_ref[0:half], preferred_element_type=F32)
             + jnp.dot(hi, wg_ref[half:D_MODEL], preferred_element_type=F32))
        u = (jnp.dot(lo, wu_ref[0:half], preferred_element_type=F32)
             + jnp.dot(hi, wu_ref[half:D_MODEL], preferred_element_type=F32))
        hh = (a * jax.nn.sigmoid(a)) * u
        ys_ref[pl.ds(start, EXPERT_CHUNK), :] = jnp.dot(hh.astype(BF16), wd_ref[...], preferred_element_type=F32)
        return carry

    lax.fori_loop(0, (n + EXPERT_CHUNK - 1) // EXPERT_CHUNK, body, 0)


def _experts(cnt, off, xs, wg, wu, wd, layer):
    nt = xs.shape[0]
    d = D_MODEL
    grid_spec = pltpu.PrefetchScalarGridSpec(
        num_scalar_prefetch=2,
        grid=(nt, N_EXPERTS),
        in_specs=[
            pl.BlockSpec((None, XS_ROWS, d // 2), lambda i, e, c, o: (i, 0, 0), pipeline_mode=pl.Buffered(1)),
            pl.BlockSpec((None, None, d, D_EXPERT), lambda i, e, c, o: (layer, e, 0, 0)),
            pl.BlockSpec((None, None, d, D_EXPERT), lambda i, e, c, o: (layer, e, 0, 0)),
            pl.BlockSpec((None, None, D_EXPERT, d), lambda i, e, c, o: (layer, e, 0, 0)),
        ],
        out_specs=pl.BlockSpec((None, XS_ROWS, d), lambda i, e, c, o: (i, 0, 0)),
    )
    return pl.pallas_call(
        _experts_kernel,
        grid_spec=grid_spec,
        out_shape=jax.ShapeDtypeStruct((nt, XS_ROWS, d), F32),
        compiler_params=_cparams("parallel", "arbitrary"),
    )(cnt, off, xs, wg, wu, wd)


def _combine_kernel(pos_ref, wt_ref, x_ref, ys_ref, g_ref, o_ref, *, final_norm):
    s = pl.program_id(1)

    def body(tl, carry):
        t = s * COMBINE_ROWS + tl
        y = (ys_ref[pl.ds(pos_ref[0, t], 1), :] * wt_ref[0, t]
             + ys_ref[pl.ds(pos_ref[0, MOE_TILE + t], 1), :] * wt_ref[0, MOE_TILE + t])
        o_ref[pl.ds(tl, 1), :] = x_ref[pl.ds(tl, 1), :] + y
        return carry

    lax.fori_loop(0, COMBINE_ROWS, body, 0, unroll=8)
    if final_norm:
        o_ref[...] = _rms(o_ref[...], g_ref[...])


def _combine(pos, wt, x, ys, final_g):
    t, d = x.shape
    nt = pos.shape[0]
    sub = MOE_TILE // COMBINE_ROWS
    g = jnp.ones((d,), F32) if final_g is None else final_g
    return pl.pallas_call(
        functools.partial(_combine_kernel, final_norm=final_g is not None),
        grid=(nt, sub),
        in_specs=[
            pl.BlockSpec((None, 1, 2 * MOE_TILE), lambda i, s: (i, 0, 0), memory_space=pltpu.SMEM),
            pl.BlockSpec((None, 1, 2 * MOE_TILE), lambda i, s: (i, 0, 0), memory_space=pltpu.SMEM),
            pl.BlockSpec((COMBINE_ROWS, d), lambda i, s: (i * sub + s, 0)),
            pl.BlockSpec((None, XS_ROWS, d), lambda i, s: (i, 0, 0)),
            pl.BlockSpec((1, d), lambda i, s: (0, 0)),
        ],
        out_specs=pl.BlockSpec((COMBINE_ROWS, d), lambda i, s: (i * sub + s, 0)),
        out_shape=jax.ShapeDtypeStruct((t, d), F32),
        compiler_params=_cparams("parallel", "arbitrary"),
    )(pos, wt, x, ys, g.reshape(1, d))


def _moe(x, g, w_r, b_r, wg, wu, wd, layer, final_g=None):
    t = x.shape[0]
    nt = t // MOE_TILE
    hp, meta, seg = _router(x, g, w_r, b_r)
    pair_major = lambda m: m.reshape(nt, MOE_TILE, 2).transpose(0, 2, 1).reshape(nt, 1, 2 * MOE_TILE)
    pos = pair_major(meta[:, 0:2].astype(jnp.int32))
    wt = pair_major(meta[:, 2:4])
    xs = _scatter_rows(pos, hp)
    ys = _experts(seg[:, 0, :N_EXPERTS], seg[:, 1, :N_EXPERTS], xs, wg, wu, wd, layer)
    return _combine(pos, wt, x, ys, final_g)


def _pad_cols(w, n):
    return jnp.pad(w, ((0, 0), (0, n - w.shape[1])))


def kernel(x, attn_norm_g, w_in, b_q_norm_g, b_w_uq, b_kv_norm_g, b_w_ukv, c_rel_bias, w_proj_a, w_proj_b, w_proj_c, w_out, ffn_norm_g, w_group, b_group, w_router, b_router, w_gate, w_up, w_down, final_norm_g):
    bsz, seq, d = x.shape
    assert (seq, d) == (SEQ, D_MODEL)
    t = bsz * seq
    depth = w_in.shape[0]
    tm = 512

    tab_a = _rope_table(A_ROT, A_HEAD_DIM)
    tab_i = _rope_table(IDX_ROT, IDX_DIM)
    tab_i_half = _rope_table(IDX_ROT, IDX_DIM, active_lanes=IDX_DIM)
    tab_b = _rope_table(B_ROPE, B_ROPE)
    tab_b_half = _rope_table(B_ROPE, B_ROPE, active_lanes=B_ROPE)

    tabs = (tab_a, tab_i, tab_i_half, tab_b, tab_b_half)
    w_gate, w_up, w_down = _to_bf16(w_gate), _to_bf16(w_up), _to_bf16(w_down)
    xf = x.reshape(t, d)
    for l in range(depth):
        w = w_in[l]
        w_a = w[:, 0:768].astype(BF16)
        w_i = _pad_cols(w[:, 768:1352], 640).astype(BF16)
        w_b = _pad_cols(w[:, 1352:1800], 512).astype(BF16)
        w_c = jnp.concatenate([w[:, 2312:3336], w[:, 1800:2312]], axis=1).astype(BF16)
        w_g = w[:, 3336:6408].astype(BF16)
        g_attn = attn_norm_g[l]
        w_uq = b_w_uq[l].reshape(B_Q_RANK, B_HEADS, B_NOPE + B_ROPE)
        w_uq = jnp.concatenate([w_uq[:, :, :B_NOPE].reshape(B_Q_RANK, -1),
                                w_uq[:, :, B_NOPE:].reshape(B_Q_RANK, -1)], axis=1).astype(BF16)
        w_ukv = b_w_ukv[l].reshape(B_KV_RANK, B_HEADS, B_NOPE + B_V)
        w_ukv = jnp.concatenate([w_ukv[:, :, :B_NOPE].reshape(B_KV_RANK, -1),
                                 w_ukv[:, :, B_NOPE:].reshape(B_KV_RANK, -1)], axis=1).astype(BF16)

        za, zi, qb, kvb, kr, zc = _in_proj(xf, g_attn, w_a, w_i, w_b, w_c, b_q_norm_g[l], w_uq,
                                           b_kv_norm_g[l], w_ukv, tabs, tm=tm)
        per_seq = lambda z: z.reshape(bsz, seq, -1)
        o_a = _dsa_attention(per_seq(za), per_seq(zi), bsz)
        o_b = _mla_attention(per_seq(qb), per_seq(kvb), per_seq(kr), bsz)
        o_c = _band_attention(per_seq(zc), _band_bias(c_rel_bias[l]), bsz)

        xf = _merge(xf, g_attn, w_g, o_a.reshape(t, -1), o_b.reshape(t, -1), o_c.reshape(t, -1),
                    w_proj_a[l].astype(BF16), w_proj_b[l].astype(BF16), w_proj_c[l].astype(BF16),
                    w_out[l].astype(BF16), tm=2 * tm)

        w_r = _pad_cols(jnp.concatenate([w_router[l], w_group[l]], axis=1), LANES)
        b_r = _pad_cols(jnp.concatenate([b_router[l], b_group[l]])[None, :], LANES)
        xf = _moe(xf, ffn_norm_g[l], w_r, b_r, w_gate, w_up, w_down, l,
                  final_g=final_norm_g if l == depth - 1 else None)

    return xf.reshape(bsz, seq, d)
```

```python
import functools

import numpy as np
import jax
import jax.numpy as jnp
from jax import lax
from jax.experimental import pallas as pl
from jax.experimental.pallas import tpu as pltpu

F32 = jnp.float32
BF16 = jnp.bfloat16

LANES = 128
D_MODEL = 1024
SEQ = 2048
CHUNK = 64
Q_BLOCK = 128
ROPE_THETA = 500000.0
EPS = 1e-6

A_HEADS = 4
A_HEAD_DIM = 128
A_ROT = 32
IDX_HEADS = 8
IDX_DIM = 64
IDX_ROT = 16
TOPK = 256
B_HEADS = 4
B_NOPE = 128
B_ROPE = 64
B_V = 128
B_Q_RANK = 256
B_KV_RANK = 128
C_HEADS = 4
C_HEAD_DIM = 128
C_LEFT_CHUNKS = 8
REL_CLIP = 128
N_GROUPS = 4
EXPERTS_PER_GROUP = 8
N_EXPERTS = 32
D_EXPERT = 256

C_KEY_BLOCKS = C_LEFT_CHUNKS * CHUNK // Q_BLOCK + 1
N_QB = SEQ // Q_BLOCK
KV_VARIANTS = 8
KV_STEP = SEQ // KV_VARIANTS

VMEM_LIMIT = 56 * 1024 * 1024

MOE_TILE = 2048
EXPERT_CHUNK = 256
XS_ROWS = 2 * MOE_TILE + 2 * EXPERT_CHUNK
assert XS_ROWS >= 2 * MOE_TILE + N_EXPERTS * 7 + EXPERT_CHUNK - 1
COMBINE_ROWS = 512
HIGH_HALF = -65536

CHUNK_SHIFT = CHUNK.bit_length() - 1
GROUP_SHIFT = EXPERTS_PER_GROUP.bit_length() - 1
assert (1 << CHUNK_SHIFT, 1 << GROUP_SHIFT) == (CHUNK, EXPERTS_PER_GROUP)
NEG_INF_KEY = int(np.array(0x807FFFFF, np.uint32).view(np.int32))

NT_DIMS = (((1,), (1,)), ((), ()))


def _nt_dot(a, b):
    return lax.dot_general(a, b, NT_DIMS, preferred_element_type=F32)


def _cparams(*sem):
    return pltpu.CompilerParams(dimension_semantics=sem, vmem_limit_bytes=VMEM_LIMIT)


def _rope_table(rot, period, active_lanes=LANES):
    half = rot // 2
    lane = np.arange(LANES)
    p = lane % period
    first = (p < half) & (lane < active_lanes)
    second = (p >= half) & (p < rot) & (lane < active_lanes)
    idx = np.where(first, p, np.where(second, p - half, 0))
    pos = jnp.arange(SEQ, dtype=F32)
    inv = ROPE_THETA ** (-jnp.arange(0, rot, 2, dtype=F32) / rot)
    ang = pos[:, None] * inv[idx][None, :]
    cos, sin = jnp.cos(ang), jnp.sin(ang)
    c = jnp.where(first | second, cos, 1.0)
    s_prev = jnp.where(second, sin, 0.0)
    s_next = jnp.where(first, -sin, 0.0)
    return jnp.stack([c, s_prev, s_next]).astype(F32)


def _rms(x, g):
    ms = jnp.mean(x * x, axis=-1, keepdims=True)
    return x * lax.rsqrt(ms + EPS) * g


def _rope_tiles(z, tile_tab, tabs, halves):
    out = []
    for c, t in enumerate(tile_tab):
        zt = z[:, c * LANES:(c + 1) * LANES]
        if t >= 0:
            tab, half = tabs[t], halves[t]
            zt = (zt * tab[0] + pltpu.roll(zt, half, 1) * tab[1]
                  + pltpu.roll(zt, LANES - half, 1) * tab[2])
        out.append(zt)
    return out


def _store_tiles(o_ref, tiles):
    for c, zt in enumerate(tiles):
        o_ref[:, c * LANES:(c + 1) * LANES] = zt.astype(o_ref.dtype)


ROPE_HALVES = (A_ROT // 2, IDX_ROT // 2, IDX_ROT // 2, B_ROPE // 2, B_ROPE // 2)


def _in_proj_kernel(x_ref, g_ref, wa_ref, wi_ref, wb_ref, wc_ref, gq_ref, wuq_ref, gkv_ref, wukv_ref,
                    ta_ref, ti_ref, tih_ref, tb_ref, tbh_ref,
                    za_ref, zi_ref, qb_ref, kvb_ref, kr_ref, zc_ref):
    tabs = (ta_ref, ti_ref, tih_ref, tb_ref, tbh_ref)
    rope = functools.partial(_rope_tiles, tabs=tabs, halves=ROPE_HALVES)
    h = _rms(x_ref[...], g_ref[...]).astype(BF16)
    dot = functools.partial(jnp.dot, preferred_element_type=F32)
    _store_tiles(za_ref, rope(dot(h, wa_ref[...]), (0, 0, 0, 0, 0, -1)))
    _store_tiles(zi_ref, rope(dot(h, wi_ref[...]), (1, 1, 1, 1, 2)))
    _store_tiles(zc_ref, rope(dot(h, wc_ref[...]), (-1,) * (3 * C_HEADS)))
    zb = dot(h, wb_ref[...])
    _store_tiles(kr_ref, rope(zb[:, B_Q_RANK + B_KV_RANK:], (4,)))
    cq = _rms(zb[:, :B_Q_RANK], gq_ref[...]).astype(BF16)
    _store_tiles(qb_ref, rope(dot(cq, wuq_ref[...]), (-1, -1, -1, -1, 3, 3)))
    ckv = _rms(zb[:, B_Q_RANK:B_Q_RANK + B_KV_RANK], gkv_ref[...]).astype(BF16)
    _store_tiles(kvb_ref, rope(dot(ckv, wukv_ref[...]), (-1,) * (2 * B_HEADS)))


def _in_proj(x, g, w_a, w_i, w_b, w_c, g_q, w_uq, g_kv, w_ukv, tabs, *, tm):
    t, d = x.shape
    seq_tiles = SEQ // tm
    fixed = lambda i: (0, 0)
    row = lambda i: (i, 0)
    weights = (w_a, w_i, w_b, w_c)
    outs = ((768, BF16), (640, F32), (768, BF16), (1024, BF16), (LANES, BF16), (1536, BF16))
    return pl.pallas_call(
        _in_proj_kernel,
        grid=(t // tm,),
        in_specs=[pl.BlockSpec((tm, d), row), pl.BlockSpec((1, d), fixed)]
        + [pl.BlockSpec(w.shape, fixed) for w in weights]
        + [pl.BlockSpec((1, B_Q_RANK), fixed), pl.BlockSpec(w_uq.shape, fixed),
           pl.BlockSpec((1, B_KV_RANK), fixed), pl.BlockSpec(w_ukv.shape, fixed)]
        + [pl.BlockSpec((3, tm, LANES), lambda i: (0, i % seq_tiles, 0)) for _ in tabs],
        out_specs=[pl.BlockSpec((tm, n), row) for n, _ in outs],
        out_shape=[jax.ShapeDtypeStruct((t, n), dt) for n, dt in outs],
        compiler_params=_cparams("parallel"),
    )(x, g.reshape(1, d), *weights, g_q.reshape(1, -1), w_uq, g_kv.reshape(1, -1), w_ukv, *tabs)


def _chunk_causal_mask(j, n_keys):
    qpos = j * Q_BLOCK + lax.broadcasted_iota(jnp.int32, (Q_BLOCK, n_keys), 0)
    kpos = lax.broadcasted_iota(jnp.int32, (Q_BLOCK, n_keys), 1)
    return (kpos >> CHUNK_SHIFT) <= (qpos >> CHUNK_SHIFT)


SEARCH_ROWS = 256
HALF_RANGE = 1 << 15


def _dsa_select(iq_ref, ikw_ref, key_ref, hi_ref, lo_ref, bias_ref, n_keys):
    j = pl.program_id(1)
    n_blk = n_keys // Q_BLOCK
    n_grp = n_keys // SEARCH_ROWS
    n_pack = n_keys // 16
    row0 = pl.multiple_of(j * Q_BLOCK, Q_BLOCK)
    sub = lax.broadcasted_iota(jnp.int32, (SEARCH_ROWS, Q_BLOCK), 0)
    lane = lax.broadcasted_iota(jnp.int32, (SEARCH_ROWS, Q_BLOCK), 1)
    q_chunk = (row0 + lane) >> CHUNK_SHIFT

    iq = iq_ref[...].astype(BF16)
    iq_stack = jnp.concatenate([iq[:, h * IDX_DIM:(h + 1) * IDX_DIM] for h in range(IDX_HEADS)], axis=0)
    iw_t = ikw_ref[pl.ds(row0, Q_BLOCK), :].T * (IDX_HEADS ** -0.5)

    for g in range(n_grp):
        k0 = g * SEARCH_ROWS
        ik = ikw_ref[k0:k0 + SEARCH_ROWS, 0:IDX_DIM].astype(BF16)
        score = jnp.zeros((SEARCH_ROWS, Q_BLOCK), F32)
        for hp in range(IDX_HEADS // 2):
            r = _nt_dot(ik, iq_stack[hp * 2 * Q_BLOCK:(hp + 1) * 2 * Q_BLOCK])
            for u in range(2):
                h = 2 * hp + u
                rel = jnp.maximum(r[:, u * Q_BLOCK:(u + 1) * Q_BLOCK] * (IDX_DIM ** -0.5), 0.0)
                score = score + rel * iw_t[IDX_DIM + h:IDX_DIM + h + 1, :]
        allowed = ((k0 + sub) >> CHUNK_SHIFT) <= q_chunk
        score = jnp.where(score == 0.0, 0.0, score)
        score = jnp.where(allowed, score, -jnp.inf)
        bits = lax.bitcast_convert_type(score, jnp.int32)
        key = bits ^ ((bits >> 31) & 0x7FFFFFFF)
        key_ref[k0:k0 + SEARCH_ROWS, :] = key
        hi_ref[k0:k0 + SEARCH_ROWS, :] = (key >> 16).astype(jnp.int16)
        lo_ref[k0:k0 + SEARCH_ROWS, :] = ((key & 0xFFFF) - HALF_RANGE).astype(jnp.int16)

    def count(pred):
        accs = [jnp.zeros((8, Q_BLOCK), F32)] * 4
        for r in range(n_keys // 8):
            accs[r % 4] = accs[r % 4] + jnp.where(pred(key_ref[8 * r:8 * (r + 1), :]), 1.0, 0.0)
        return jnp.sum(accs[0] + accs[1] + accs[2] + accs[3], axis=0, keepdims=True)

    one16 = jnp.ones((16, Q_BLOCK), jnp.int16)
    zero16 = jnp.zeros((16, Q_BLOCK), jnp.int16)

    def search16(ref):
        def rnd(i, base):
            cand = base + jnp.left_shift(jnp.int32(1), 15 - i)
            cand16 = jnp.broadcast_to(cand, (16, Q_BLOCK)).astype(jnp.int16)
            accs = [zero16] * 4
            for r in range(n_pack):
                accs[r % 4] = accs[r % 4] + jnp.where(ref[16 * r:16 * (r + 1), :] >= cand16, one16, zero16)
            total = (accs[0] + accs[1]) + (accs[2] + accs[3])
            cnt = jnp.sum(total.astype(jnp.int32), axis=0, keepdims=True)
            return jnp.where(cnt >= TOPK, cand, base)
        return lax.fori_loop(0, 16, rnd, jnp.full((1, Q_BLOCK), -HALF_RANGE, jnp.int32))

    top = search16(hi_ref)
    top16 = jnp.broadcast_to(top, (16, Q_BLOCK)).astype(jnp.int16)
    for r in range(n_pack):
        rows = slice(16 * r, 16 * (r + 1))
        hi = hi_ref[rows, :]
        lo_ref[rows, :] = jnp.where(hi > top16, jnp.int16(HALF_RANGE - 1),
                                    jnp.where(hi == top16, lo_ref[rows, :], jnp.int16(-HALF_RANGE)))
    thr = (top << 16) | (search16(lo_ref) + HALF_RANGE)
    thr8 = jnp.broadcast_to(thr, (8, Q_BLOCK))
    cnt_gt = count(lambda kk: kk > thr8)
    cnt_ge = count(lambda kk: kk >= thr8)
    tie_cols = (cnt_ge > TOPK) & (thr > NEG_INF_KEY)
    has_tie = jnp.max(jnp.where(tie_cols, 1.0, 0.0)) > 0.0

    for kb in range(n_blk):
        kk = key_ref[kb * Q_BLOCK:(kb + 1) * Q_BLOCK, :]
        bias_ref[kb * Q_BLOCK:(kb + 1) * Q_BLOCK, :] = jnp.where((kk >= thr) & (kk > NEG_INF_KEY), 0.0, -jnp.inf).T

    @pl.when(has_tie)
    def _():
        need = TOPK - cnt_gt
        row = lax.broadcasted_iota(jnp.int32, (Q_BLOCK, Q_BLOCK), 0)
        col = lax.broadcasted_iota(jnp.int32, (Q_BLOCK, Q_BLOCK), 1)
        lower = jnp.where(col < row, 1.0, 0.0).astype(BF16)

        def tie_block(kb, seen):
            k0 = pl.multiple_of(kb * Q_BLOCK, Q_BLOCK)
            kk = key_ref[pl.ds(k0, Q_BLOCK), :]
            eq = jnp.where(kk == thr, 1.0, 0.0)
            before = jnp.dot(lower, eq.astype(BF16), preferred_element_type=F32) + seen
            keep = (kk > thr) | ((kk == thr) & (before < need))
            bias_ref[pl.ds(k0, Q_BLOCK), :] = jnp.where(keep & (kk > NEG_INF_KEY), 0.0, -jnp.inf).T
            return seen + jnp.sum(eq, axis=0, keepdims=True)

        lax.fori_loop(0, n_blk, tie_block, jnp.zeros((1, Q_BLOCK), F32))


def _dsa_attend(q_ref, kv_ref, bias_ref, o_ref, n_keys):
    q = q_ref[...]
    k = kv_ref[0:n_keys, 0:A_HEAD_DIM]
    v = kv_ref[0:n_keys, A_HEAD_DIM:2 * A_HEAD_DIM]
    bias = jnp.concatenate([bias_ref[c * Q_BLOCK:(c + 1) * Q_BLOCK, :] for c in range(n_keys // Q_BLOCK)],
                           axis=1)
    for h in range(A_HEADS):
        logits = _nt_dot(q[:, h * A_HEAD_DIM:(h + 1) * A_HEAD_DIM], k) * (A_HEAD_DIM ** -0.5) + bias
        m = jnp.max(logits, axis=-1, keepdims=True)
        p = jnp.exp(logits - m)
        l = jnp.sum(p, axis=-1, keepdims=True)
        o = jnp.dot(p.astype(BF16), v, preferred_element_type=F32) / l
        o_ref[:, h * A_HEAD_DIM:(h + 1) * A_HEAD_DIM] = o.astype(o_ref.dtype)


def _dsa_kernel(q_ref, kv_ref, iq_ref, ikw_ref, o_ref, key_ref, hi_ref, lo_ref, bias_ref):
    j = pl.program_id(1)
    for v in range(KV_VARIANTS):
        @pl.when(j // (N_QB // KV_VARIANTS) == v)
        def _(v=v):
            n_keys = KV_STEP * (v + 1)
            n_scored = -(-n_keys // SEARCH_ROWS) * SEARCH_ROWS
            _dsa_select(iq_ref, ikw_ref, key_ref, hi_ref, lo_ref, bias_ref, n_scored)
            _dsa_attend(q_ref, kv_ref, bias_ref, o_ref, n_keys)


def _dsa_attention(za, zi, bsz):
    return pl.pallas_call(
        _dsa_kernel,
        grid=(bsz, N_QB),
        in_specs=[
            pl.BlockSpec((None, Q_BLOCK, 512), lambda b, j: (b, j, 0)),
            pl.BlockSpec((None, SEQ, 256), lambda b, j: (b, 0, 2)),
            pl.BlockSpec((None, Q_BLOCK, 512), lambda b, j: (b, j, 0)),
            pl.BlockSpec((None, SEQ, LANES), lambda b, j: (b, 0, 4)),
        ],
        out_specs=pl.BlockSpec((None, Q_BLOCK, 512), lambda b, j: (b, j, 0)),
        out_shape=jax.ShapeDtypeStruct((bsz, SEQ, 512), BF16),
        scratch_shapes=[pltpu.VMEM((SEQ, Q_BLOCK), jnp.int32),
                        pltpu.VMEM((SEQ, Q_BLOCK), jnp.int16),
                        pltpu.VMEM((SEQ, Q_BLOCK), jnp.int16),
                        pltpu.VMEM((SEQ, Q_BLOCK), F32)],
        compiler_params=_cparams("parallel", "arbitrary"),
    )(za, za, zi, zi)


def _mla_body(q_ref, kv_ref, kr_ref, o_ref, n_keys):
    j = pl.program_id(1)
    mask = _chunk_causal_mask(j, n_keys)
    kr = kr_ref[0:n_keys, 0:B_ROPE]
    scale = (B_NOPE + B_ROPE) ** -0.5
    for h in range(B_HEADS):
        qn = q_ref[:, h * B_NOPE:(h + 1) * B_NOPE]
        qr = q_ref[:, B_HEADS * B_NOPE + h * B_ROPE:B_HEADS * B_NOPE + (h + 1) * B_ROPE]
        kn = kv_ref[0:n_keys, h * B_NOPE:(h + 1) * B_NOPE]
        vv = kv_ref[0:n_keys, B_HEADS * B_NOPE + h * B_V:B_HEADS * B_NOPE + (h + 1) * B_V]
        s = (_nt_dot(qn, kn) + _nt_dot(qr, kr)) * scale
        s = jnp.where(mask, s, -jnp.inf)
        m = jnp.max(s, axis=-1, keepdims=True)
        p = jnp.exp(s - m)
        l = jnp.sum(p, axis=-1, keepdims=True)
        o = jnp.dot(p.astype(BF16), vv, preferred_element_type=F32) / l
        o_ref[:, h * B_V:(h + 1) * B_V] = o.astype(o_ref.dtype)


def _mla_kernel(q_ref, kv_ref, kr_ref, o_ref):
    j = pl.program_id(1)
    for v in range(KV_VARIANTS):
        @pl.when(j // (N_QB // KV_VARIANTS) == v)
        def _(v=v):
            _mla_body(q_ref, kv_ref, kr_ref, o_ref, KV_STEP * (v + 1))


def _mla_attention(qb, kvb, kr, bsz):
    return pl.pallas_call(
        _mla_kernel,
        grid=(bsz, N_QB),
        in_specs=[
            pl.BlockSpec((None, Q_BLOCK, 768), lambda b, j: (b, j, 0)),
            pl.BlockSpec((None, SEQ, 1024), lambda b, j: (b, 0, 0)),
            pl.BlockSpec((None, SEQ, LANES), lambda b, j: (b, 0, 0)),
        ],
        out_specs=pl.BlockSpec((None, Q_BLOCK, 512), lambda b, j: (b, j, 0)),
        out_shape=jax.ShapeDtypeStruct((bsz, SEQ, 512), BF16),
        compiler_params=_cparams("parallel", "arbitrary"),
    )(qb, kvb, kr)


def _band_bias(rel_table):
    n = 2 * REL_CLIP + 1
    period = 2 * n - 1
    heads = rel_table.shape[0]
    ext = jnp.concatenate([rel_table, jnp.broadcast_to(rel_table[:, n - 1:n], (heads, n - 1))], axis=1)
    kj = np.arange(Q_BLOCK)[:, None]
    qi = np.arange(Q_BLOCK)[None, :]
    out = []
    for d in range(C_KEY_BLOCKS):
        base = d * Q_BLOCK + REL_CLIP
        if base - (Q_BLOCK - 1) >= n - 1:
            bias = jnp.broadcast_to(rel_table[:, n - 1][:, None, None], (heads, Q_BLOCK, Q_BLOCK))
        else:
            shifted = jnp.roll(ext, -base, axis=1)
            bias = jnp.tile(shifted, (1, Q_BLOCK))[:, :Q_BLOCK * (period - 1)]
            bias = bias.reshape(heads, Q_BLOCK, period - 1)[:, :, :Q_BLOCK]
        cdiff = 2 * d + qi // CHUNK - kj // CHUNK
        valid = (cdiff >= 0) & (cdiff <= C_LEFT_CHUNKS)
        out.append(jnp.where(valid[None], bias.astype(F32), -jnp.inf))
    return jnp.swapaxes(jnp.stack(out, axis=1), 2, 3)


BAND_QB = 4


def _band_kernel(q_ref, kv_ref, bias_ref, o_ref):
    scale = C_HEAD_DIM ** -0.5
    window = C_KEY_BLOCKS * Q_BLOCK
    for sb in range(BAND_QB):
        j = pl.program_id(1) * BAND_QB + sb
        rows = slice(sb * Q_BLOCK, (sb + 1) * Q_BLOCK)
        first = jnp.maximum(j - (C_KEY_BLOCKS - 1), 0)
        win = pl.ds(pl.multiple_of(first * Q_BLOCK, Q_BLOCK), window)
        for h in range(C_HEADS):
            q = q_ref[rows, h * C_HEAD_DIM:(h + 1) * C_HEAD_DIM]
            k = kv_ref[win, h * C_HEAD_DIM:(h + 1) * C_HEAD_DIM]
            v = kv_ref[win, (C_HEADS + h) * C_HEAD_DIM:(C_HEADS + h + 1) * C_HEAD_DIM]
            s = _nt_dot(q, k) * scale
            parts = []
            for c in range(C_KEY_BLOCKS):
                d = j - (first + c)
                bias = bias_ref[h, jnp.clip(d, 0, C_KEY_BLOCKS - 1)]
                parts.append(jnp.where(d >= 0, s[:, c * Q_BLOCK:(c + 1) * Q_BLOCK] + bias, -jnp.inf))
            s = jnp.concatenate(parts, axis=1)
            m = jnp.max(s, axis=-1, keepdims=True)
            p = jnp.exp(s - m)
            l = jnp.sum(p, axis=-1, keepdims=True)
            o = jnp.dot(p.astype(BF16), v, preferred_element_type=F32) / l
            o_ref[rows, h * C_HEAD_DIM:(h + 1) * C_HEAD_DIM] = o.astype(o_ref.dtype)


def _band_attention(zc, bias, bsz):
    rows = BAND_QB * Q_BLOCK
    return pl.pallas_call(
        _band_kernel,
        grid=(bsz, SEQ // rows),
        in_specs=[
            pl.BlockSpec((None, rows, 512), lambda b, j: (b, j, 2)),
            pl.BlockSpec((None, SEQ, 1024), lambda b, j: (b, 0, 0)),
            pl.BlockSpec((C_HEADS, C_KEY_BLOCKS, Q_BLOCK, Q_BLOCK), lambda b, j: (0, 0, 0, 0)),
        ],
        out_specs=pl.BlockSpec((None, rows, 512), lambda b, j: (b, j, 0)),
        out_shape=jax.ShapeDtypeStruct((bsz, SEQ, 512), BF16),
        compiler_params=_cparams("parallel", "arbitrary"),
    )(zc, zc, bias)


def _merge_kernel(x_ref, g_ref, wgl_ref, oa_ref, ob_ref, oc_ref, wpa_ref, wpb_ref, wpc_ref, wout_ref, o_ref):
    x = x_ref[...]
    ms = jnp.mean(x * x, axis=-1, keepdims=True)
    h = (x * lax.rsqrt(ms + EPS) * g_ref[...]).astype(BF16)
    mix = jnp.zeros(x.shape, F32)
    for i, (o_in, wp) in enumerate(((oa_ref, wpa_ref), (ob_ref, wpb_ref), (oc_ref, wpc_ref))):
        gl = jnp.dot(h, wgl_ref[:, i * D_MODEL:(i + 1) * D_MODEL], preferred_element_type=F32)
        gate = jax.nn.sigmoid(gl)
        mix = mix + gate * jnp.dot(o_in[...], wp[...], preferred_element_type=F32)
    o_ref[...] = x + jnp.dot(mix.astype(BF16), wout_ref[...], preferred_element_type=F32)


def _merge(x, g, wgl, oa, ob, oc, wpa, wpb, wpc, wout, *, tm):
    t, d = x.shape
    row = lambda i: (i, 0)
    fixed = lambda i: (0, 0)
    return pl.pallas_call(
        _merge_kernel,
        grid=(t // tm,),
        in_specs=[
            pl.BlockSpec((tm, d), row),
            pl.BlockSpec((1, d), fixed),
            pl.BlockSpec((d, 3 * d), fixed),
            pl.BlockSpec((tm, 512), row),
            pl.BlockSpec((tm, 512), row),
            pl.BlockSpec((tm, 512), row),
            pl.BlockSpec((512, d), fixed),
            pl.BlockSpec((512, d), fixed),
            pl.BlockSpec((512, d), fixed),
            pl.BlockSpec((d, d), fixed),
        ],
        out_specs=pl.BlockSpec((tm, d), row),
        out_shape=jax.ShapeDtypeStruct((t, d), F32),
        compiler_params=_cparams("parallel"),
    )(x, g.reshape(1, d), wgl, oa, ob, oc, wpa, wpb, wpc, wout)


def _first_argmax(vals, lane):
    m = jnp.max(vals, axis=-1, keepdims=True)
    idx = jnp.min(jnp.where(vals == m, lane, LANES), axis=-1, keepdims=True)
    return m, idx


def _pack_bf16_pairs(h):
    n = h.shape[1] // 2
    bits = lax.bitcast_convert_type(h.astype(jnp.bfloat16).astype(F32), jnp.int32)
    return lax.shift_right_logical(bits[:, :n], 16) | bits[:, n:]


def _unpack_bf16_pairs(w):
    lo = lax.bitcast_convert_type(w << 16, F32).astype(BF16)
    hi = lax.bitcast_convert_type(w & HIGH_HALF, F32).astype(BF16)
    return lo, hi


def _router_kernel(x_ref, g_ref, w_ref, b_ref, hp_ref, meta_ref, seg_ref):
    x = x_ref[...]
    ms = jnp.mean(x * x, axis=-1, keepdims=True)
    h = x * lax.rsqrt(ms + EPS) * g_ref[...]
    hp_ref[...] = _pack_bf16_pairs(h)
    logits = jnp.dot(h, w_ref[...], preferred_element_type=F32, precision=lax.Precision.HIGHEST) + b_ref[...]
    lane = lax.broadcasted_iota(jnp.int32, logits.shape, 1)
    is_grp = (lane >= N_EXPERTS) & (lane < N_EXPERTS + N_GROUPS)
    gl = jnp.where(is_grp, logits, -jnp.inf)
    gmax, gidx = _first_argmax(gl, lane)
    pg = 1.0 / jnp.sum(jnp.exp(gl - gmax), axis=-1, keepdims=True)
    gsel = gidx - N_EXPERTS
    in_grp = (lane >> GROUP_SHIFT) == gsel
    el = jnp.where(in_grp, logits, -jnp.inf)
    m1, i1 = _first_argmax(el, lane)
    z = jnp.sum(jnp.exp(el - m1), axis=-1, keepdims=True)
    el2 = jnp.where(lane == i1, -jnp.inf, el)
    m2, i2 = _first_argmax(el2, lane)
    pe1 = 1.0 / z
    pe2 = jnp.exp(m2 - m1) / z
    den = pe1 + pe2
    w1 = pg * pe1 / den
    w2 = pg * pe2 / den

    sel1 = lane == i1
    sel2 = lane == i2
    onehot = jnp.where(sel1 | sel2, 1.0, 0.0)
    a = lax.broadcasted_iota(jnp.int32, (LANES, LANES), 0)
    b = lax.broadcasted_iota(jnp.int32, (LANES, LANES), 1)
    lower = jnp.where(b < a, 1.0, 0.0).astype(BF16)
    carry = jnp.zeros((1, LANES), F32)
    ranks = []
    for c in range(MOE_TILE // LANES):
        blk = onehot[c * LANES:(c + 1) * LANES]
        ranks.append(jnp.dot(lower, blk.astype(BF16), preferred_element_type=F32) + carry)
        carry = carry + jnp.sum(blk, axis=0, keepdims=True)
    rank = jnp.concatenate(ranks, axis=0)
    cnt = jnp.broadcast_to(carry, (8, LANES))
    seg = jnp.floor((cnt + 7.0) * 0.125) * 8.0
    lane8 = lax.broadcasted_iota(jnp.int32, (8, LANES), 1)
    scan = seg
    for k in (1, 2, 4, 8, 16, 32, 64):
        scan = scan + jnp.where(lane8 >= k, pltpu.roll(scan, k, 1), 0.0)
    off = scan - seg
    where_row = rank + off[0:1]
    pos1 = jnp.sum(jnp.where(sel1, where_row, 0.0), axis=-1, keepdims=True)
    pos2 = jnp.sum(jnp.where(sel2, where_row, 0.0), axis=-1, keepdims=True)
    meta_ref[...] = (jnp.where(lane == 0, pos1, 0.0) + jnp.where(lane == 1, pos2, 0.0)
                     + jnp.where(lane == 2, w1, 0.0) + jnp.where(lane == 3, w2, 0.0))
    row8 = lax.broadcasted_iota(jnp.int32, (8, LANES), 0)
    seg_ref[...] = jnp.where(row8 == 0, cnt, jnp.where(row8 == 1, off, 0.0)).astype(jnp.int32)


def _router(x, g, w, b):
    t, d = x.shape
    nt = t // MOE_TILE
    return pl.pallas_call(
        _router_kernel,
        grid=(nt,),
        in_specs=[
            pl.BlockSpec((MOE_TILE, d), lambda i: (i, 0)),
            pl.BlockSpec((1, d), lambda i: (0, 0)),
            pl.BlockSpec((d, LANES), lambda i: (0, 0)),
            pl.BlockSpec((1, LANES), lambda i: (0, 0)),
        ],
        out_specs=[
            pl.BlockSpec((MOE_TILE, d // 2), lambda i: (i, 0)),
            pl.BlockSpec((MOE_TILE, LANES), lambda i: (i, 0)),
            pl.BlockSpec((None, 8, LANES), lambda i: (i, 0, 0)),
        ],
        out_shape=[
            jax.ShapeDtypeStruct((t, d // 2), jnp.int32),
            jax.ShapeDtypeStruct((t, LANES), F32),
            jax.ShapeDtypeStruct((nt, 8, LANES), jnp.int32),
        ],
        compiler_params=_cparams("parallel"),
    )(x, g.reshape(1, d), w, b)


def _scatter_kernel(pos_ref, hp_ref, xs_ref):
    xs_ref[...] = jnp.zeros_like(xs_ref)

    def body(t, carry):
        row = hp_ref[pl.ds(t, 1), :]
        xs_ref[pl.ds(pos_ref[0, t], 1), :] = row
        xs_ref[pl.ds(pos_ref[0, MOE_TILE + t], 1), :] = row
        return carry

    lax.fori_loop(0, MOE_TILE, body, 0, unroll=8)


def _scatter_rows(pos, hp):
    nt = pos.shape[0]
    return pl.pallas_call(
        _scatter_kernel,
        grid=(nt,),
        in_specs=[
            pl.BlockSpec((None, 1, 2 * MOE_TILE), lambda i: (i, 0, 0), memory_space=pltpu.SMEM),
            pl.BlockSpec((MOE_TILE, hp.shape[1]), lambda i: (i, 0)),
        ],
        out_specs=pl.BlockSpec((None, XS_ROWS, hp.shape[1]), lambda i: (i, 0, 0)),
        out_shape=jax.ShapeDtypeStruct((nt, XS_ROWS, hp.shape[1]), jnp.int32),
        compiler_params=_cparams("parallel"),
    )(pos, hp)


CAST_EXPERTS = 4


def _cast_kernel(w_ref, o_ref):
    o_ref[...] = w_ref[...].astype(o_ref.dtype)


def _to_bf16(w):
    n_l, n_e, a, b = w.shape
    spec = pl.BlockSpec((None, CAST_EXPERTS, a, b), lambda l, e: (l, e, 0, 0))
    return pl.pallas_call(
        _cast_kernel,
        grid=(n_l, n_e // CAST_EXPERTS),
        in_specs=[spec],
        out_specs=spec,
        out_shape=jax.ShapeDtypeStruct(w.shape, BF16),
        compiler_params=_cparams("parallel", "parallel"),
    )(w)


def _experts_kernel(cnt_ref, off_ref, xs_ref, wg_ref, wu_ref, wd_ref, ys_ref):
    i = pl.program_id(0)
    e = pl.program_id(1)
    half = D_MODEL // 2

    @pl.when(e == 0)
    def _():
        ys_ref[2 * MOE_TILE:XS_ROWS, :] = jnp.zeros((XS_ROWS - 2 * MOE_TILE, D_MODEL), F32)

    n = cnt_ref[i, e]
    off = off_ref[i, e]

    def body(c, carry):
        start = pl.multiple_of(off + c * EXPERT_CHUNK, 8)
        lo, hi = _unpack_bf16_pairs(xs_ref[pl.ds(start, EXPERT_CHUNK), :])
        a = (jnp.dot(lo, wg_ref[0:half], preferred_element_type=F32)
             + jnp.dot(hi, wg_ref[half:D_MODEL], preferred_element_type=F32))
        u = (jnp.dot(lo, wu_ref[0:half], preferred_element_type=F32)
             + jnp.dot(hi, wu_ref[half:D_MODEL], preferred_element_type=F32))
        hh = (a * jax.nn.sigmoid(a)) * u
        ys_ref[pl.ds(start, EXPERT_CHUNK), :] = jnp.dot(hh.astype(BF16), wd_ref[...], preferred_element_type=F32)
        return carry

    lax.fori_loop(0, (n + EXPERT_CHUNK - 1) // EXPERT_CHUNK, body, 0)


def _experts(cnt, off, xs, wg, wu, wd, layer):
    nt = xs.shape[0]
    d = D_MODEL
    grid_spec = pltpu.PrefetchScalarGridSpec(
        num_scalar_prefetch=2,
        grid=(nt, N_EXPERTS),
        in_specs=[
            pl.BlockSpec((None, XS_ROWS, d // 2), lambda i, e, c, o: (i, 0, 0), pipeline_mode=pl.Buffered(1)),
            pl.BlockSpec((None, None, d, D_EXPERT), lambda i, e, c, o: (layer, e, 0, 0)),
            pl.BlockSpec((None, None, d, D_EXPERT), lambda i, e, c, o: (layer, e, 0, 0)),
            pl.BlockSpec((None, None, D_EXPERT, d), lambda i, e, c, o: (layer, e, 0, 0)),
        ],
        out_specs=pl.BlockSpec((None, XS_ROWS, d), lambda i, e, c, o: (i, 0, 0)),
    )
    return pl.pallas_call(
        _experts_kernel,
        grid_spec=grid_spec,
        out_shape=jax.ShapeDtypeStruct((nt, XS_ROWS, d), F32),
        compiler_params=_cparams("parallel", "arbitrary"),
    )(cnt, off, xs, wg, wu, wd)


def _combine_kernel(pos_ref, wt_ref, x_ref, ys_ref, g_ref, o_ref, *, final_norm):
    s = pl.program_id(1)

    def body(tl, carry):
        t = s * COMBINE_ROWS + tl
        y = (ys_ref[pl.ds(pos_ref[0, t], 1), :] * wt_ref[0, t]
             + ys_ref[pl.ds(pos_ref[0, MOE_TILE + t], 1), :] * wt_ref[0, MOE_TILE + t])
        o_ref[pl.ds(tl, 1), :] = x_ref[pl.ds(tl, 1), :] + y
        return carry

    lax.fori_loop(0, COMBINE_ROWS, body, 0, unroll=8)
    if final_norm:
        o_ref[...] = _rms(o_ref[...], g_ref[...])


def _combine(pos, wt, x, ys, final_g):
    t, d = x.shape
    nt = pos.shape[0]
    sub = MOE_TILE // COMBINE_ROWS
    g = jnp.ones((d,), F32) if final_g is None else final_g
    return pl.pallas_call(
        functools.partial(_combine_kernel, final_norm=final_g is not None),
        grid=(nt, sub),
        in_specs=[
            pl.BlockSpec((None, 1, 2 * MOE_TILE), lambda i, s: (i, 0, 0), memory_space=pltpu.SMEM),
            pl.BlockSpec((None, 1, 2 * MOE_TILE), lambda i, s: (i, 0, 0), memory_space=pltpu.SMEM),
            pl.BlockSpec((COMBINE_ROWS, d), lambda i, s: (i * sub + s, 0)),
            pl.BlockSpec((None, XS_ROWS, d), lambda i, s: (i, 0, 0)),
            pl.BlockSpec((1, d), lambda i, s: (0, 0)),
        ],
        out_specs=pl.BlockSpec((COMBINE_ROWS, d), lambda i, s: (i * sub + s, 0)),
        out_shape=jax.ShapeDtypeStruct((t, d), F32),
        compiler_params=_cparams("parallel", "arbitrary"),
    )(pos, wt, x, ys, g.reshape(1, d))


def _moe(x, g, w_r, b_r, wg, wu, wd, layer, final_g=None):
    t = x.shape[0]
    nt = t // MOE_TILE
    hp, meta, seg = _router(x, g, w_r, b_r)
    pair_major = lambda m: m.reshape(nt, MOE_TILE, 2).transpose(0, 2, 1).reshape(nt, 1, 2 * MOE_TILE)
    pos = pair_major(meta[:, 0:2].astype(jnp.int32))
    wt = pair_major(meta[:, 2:4])
    xs = _scatter_rows(pos, hp)
    ys = _experts(seg[:, 0, :N_EXPERTS], seg[:, 1, :N_EXPERTS], xs, wg, wu, wd, layer)
    return _combine(pos, wt, x, ys, final_g)


def _pad_cols(w, n):
    return jnp.pad(w, ((0, 0), (0, n - w.shape[1])))


def kernel(x, attn_norm_g, w_in, b_q_norm_g, b_w_uq, b_kv_norm_g, b_w_ukv, c_rel_bias, w_proj_a, w_proj_b, w_proj_c, w_out, ffn_norm_g, w_group, b_group, w_router, b_router, w_gate, w_up, w_down, final_norm_g):
    bsz, seq, d = x.shape
    assert (seq, d) == (SEQ, D_MODEL)
    t = bsz * seq
    depth = w_in.shape[0]
    tm = 512

    tab_a = _rope_table(A_ROT, A_HEAD_DIM)
    tab_i = _rope_table(IDX_ROT, IDX_DIM)
    tab_i_half = _rope_table(IDX_ROT, IDX_DIM, active_lanes=IDX_DIM)
    tab_b = _rope_table(B_ROPE, B_ROPE)
    tab_b_half = _rope_table(B_ROPE, B_ROPE, active_lanes=B_ROPE)

    tabs = (tab_a, tab_i, tab_i_half, tab_b, tab_b_half)
    w_gate, w_up, w_down = _to_bf16(w_gate), _to_bf16(w_up), _to_bf16(w_down)
    xf = x.reshape(t, d)
    for l in range(depth):
        w = w_in[l]
        w_a = w[:, 0:768].astype(BF16)
        w_i = _pad_cols(w[:, 768:1352], 640).astype(BF16)
        w_b = _pad_cols(w[:, 1352:1800], 512).astype(BF16)
        w_c = jnp.concatenate([w[:, 2312:3336], w[:, 1800:2312]], axis=1).astype(BF16)
        w_g = w[:, 3336:6408].astype(BF16)
        g_attn = attn_norm_g[l]
        w_uq = b_w_uq[l].reshape(B_Q_RANK, B_HEADS, B_NOPE + B_ROPE)
        w_uq = jnp.concatenate([w_uq[:, :, :B_NOPE].reshape(B_Q_RANK, -1),
                                w_uq[:, :, B_NOPE:].reshape(B_Q_RANK, -1)], axis=1).astype(BF16)
        w_ukv = b_w_ukv[l].reshape(B_KV_RANK, B_HEADS, B_NOPE + B_V)
        w_ukv = jnp.concatenate([w_ukv[:, :, :B_NOPE].reshape(B_KV_RANK, -1),
                                 w_ukv[:, :, B_NOPE:].reshape(B_KV_RANK, -1)], axis=1).astype(BF16)

        za, zi, qb, kvb, kr, zc = _in_proj(xf, g_attn, w_a, w_i, w_b, w_c, b_q_norm_g[l], w_uq,
                                           b_kv_norm_g[l], w_ukv, tabs, tm=tm)
        per_seq = lambda z: z.reshape(bsz, seq, -1)
        o_a = _dsa_attention(per_seq(za), per_seq(zi), bsz)
        o_b = _mla_attention(per_seq(qb), per_seq(kvb), per_seq(kr), bsz)
        o_c = _band_attention(per_seq(zc), _band_bias(c_rel_bias[l]), bsz)

        xf = _merge(xf, g_attn, w_g, o_a.reshape(t, -1), o_b.reshape(t, -1), o_c.reshape(t, -1),
                    w_proj_a[l].astype(BF16), w_proj_b[l].astype(BF16), w_proj_c[l].astype(BF16),
                    w_out[l].astype(BF16), tm=2 * tm)

        w_r = _pad_cols(jnp.concatenate([w_router[l], w_group[l]], axis=1), LANES)
        b_r = _pad_cols(jnp.concatenate([b_router[l], b_group[l]])[None, :], LANES)
        xf = _moe(xf, ffn_norm_g[l], w_r, b_r, w_gate, w_up, w_down, l,
                  final_g=final_norm_g if l == depth - 1 else None)

    return xf.reshape(bsz, seq, d)
```

```python
import functools

import numpy as np
import jax
import jax.numpy as jnp
from jax import lax
from jax.experimental import pallas as pl
from jax.experimental.pallas import tpu as pltpu

F32 = jnp.float32
BF16 = jnp.bfloat16

LANES = 128
D_MODEL = 1024
SEQ = 2048
CHUNK = 64
Q_BLOCK = 128
ROPE_THETA = 500000.0
EPS = 1e-6

A_HEADS = 4
A_HEAD_DIM = 128
A_ROT = 32
IDX_HEADS = 8
IDX_DIM = 64
IDX_ROT = 16
TOPK = 256
B_HEADS = 4
B_NOPE = 128
B_ROPE = 64
B_V = 128
B_Q_RANK = 256
B_KV_RANK = 128
C_HEADS = 4
C_HEAD_DIM = 128
C_LEFT_CHUNKS = 8
REL_CLIP = 128
N_GROUPS = 4
EXPERTS_PER_GROUP = 8
N_EXPERTS = 32
D_EXPERT = 256

C_KEY_BLOCKS = C_LEFT_CHUNKS * CHUNK // Q_BLOCK + 1
N_QB = SEQ // Q_BLOCK
KV_VARIANTS = 8
KV_STEP = SEQ // KV_VARIANTS

VMEM_LIMIT = 56 * 1024 * 1024

MOE_TILE = 2048
EXPERT_CHUNK = 256
XS_ROWS = 2 * MOE_TILE + 2 * EXPERT_CHUNK
assert XS_ROWS >= 2 * MOE_TILE + N_EXPERTS * 7 + EXPERT_CHUNK - 1
COMBINE_ROWS = 512
HIGH_HALF = -65536

CHUNK_SHIFT = CHUNK.bit_length() - 1
GROUP_SHIFT = EXPERTS_PER_GROUP.bit_length() - 1
assert (1 << CHUNK_SHIFT, 1 << GROUP_SHIFT) == (CHUNK, EXPERTS_PER_GROUP)
NEG_INF_KEY = int(np.array(0x807FFFFF, np.uint32).view(np.int32))

NT_DIMS = (((1,), (1,)), ((), ()))


def _nt_dot(a, b):
    return lax.dot_general(a, b, NT_DIMS, preferred_element_type=F32)


def _cparams(*sem):
    return pltpu.CompilerParams(dimension_semantics=sem, vmem_limit_bytes=VMEM_LIMIT)


def _rope_table(rot, period, active_lanes=LANES):
    half = rot // 2
    lane = np.arange(LANES)
    p = lane % period
    first = (p < half) & (lane < active_lanes)
    second = (p >= half) & (p < rot) & (lane < active_lanes)
    idx = np.where(first, p, np.where(second, p - half, 0))
    pos = jnp.arange(SEQ, dtype=F32)
    inv = ROPE_THETA ** (-jnp.arange(0, rot, 2, dtype=F32) / rot)
    ang = pos[:, None] * inv[idx][None, :]
    cos, sin = jnp.cos(ang), jnp.sin(ang)
    c = jnp.where(first | second, cos, 1.0)
    s_prev = jnp.where(second, sin, 0.0)
    s_next = jnp.where(first, -sin, 0.0)
    return jnp.stack([c, s_prev, s_next]).astype(F32)


def _rms(x, g):
    ms = jnp.mean(x * x, axis=-1, keepdims=True)
    return x * lax.rsqrt(ms + EPS) * g


def _rope_tiles(z, tile_tab, tabs, halves):
    out = []
    for c, t in enumerate(tile_tab):
        zt = z[:, c * LANES:(c + 1) * LANES]
        if t >= 0:
            tab, half = tabs[t], halves[t]
            zt = (zt * tab[0] + pltpu.roll(zt, half, 1) * tab[1]
                  + pltpu.roll(zt, LANES - half, 1) * tab[2])
        out.append(zt)
    return out


def _store_tiles(o_ref, tiles):
    for c, zt in enumerate(tiles):
        o_ref[:, c * LANES:(c + 1) * LANES] = zt.astype(o_ref.dtype)


ROPE_HALVES = (A_ROT // 2, IDX_ROT // 2, IDX_ROT // 2, B_ROPE // 2, B_ROPE // 2)


def _in_proj_kernel(x_ref, g_ref, wa_ref, wi_ref, wb_ref, wc_ref, gq_ref, wuq_ref, gkv_ref, wukv_ref,
                    ta_ref, ti_ref, tih_ref, tb_ref, tbh_ref,
                    za_ref, zi_ref, qb_ref, kvb_ref, kr_ref, zc_ref):
    tabs = (ta_ref, ti_ref, tih_ref, tb_ref, tbh_ref)
    rope = functools.partial(_rope_tiles, tabs=tabs, halves=ROPE_HALVES)
    h = _rms(x_ref[...], g_ref[...]).astype(BF16)
    dot = functools.partial(jnp.dot, preferred_element_type=F32)
    _store_tiles(za_ref, rope(dot(h, wa_ref[...]), (0, 0, 0, 0, 0, -1)))
    _store_tiles(zi_ref, rope(dot(h, wi_ref[...]), (1, 1, 1, 1, 2)))
    _store_tiles(zc_ref, rope(dot(h, wc_ref[...]), (-1,) * (3 * C_HEADS)))
    zb = dot(h, wb_ref[...])
    _store_tiles(kr_ref, rope(zb[:, B_Q_RANK + B_KV_RANK:], (4,)))
    cq = _rms(zb[:, :B_Q_RANK], gq_ref[...]).astype(BF16)
    _store_tiles(qb_ref, rope(dot(cq, wuq_ref[...]), (-1, -1, -1, -1, 3, 3)))
    ckv = _rms(zb[:, B_Q_RANK:B_Q_RANK + B_KV_RANK], gkv_ref[...]).astype(BF16)
    _store_tiles(kvb_ref, rope(dot(ckv, wukv_ref[...]), (-1,) * (2 * B_HEADS)))


def _in_proj(x, g, w_a, w_i, w_b, w_c, g_q, w_uq, g_kv, w_ukv, tabs, *, tm):
    t, d = x.shape
    seq_tiles = SEQ // tm
    fixed = lambda i: (0, 0)
    row = lambda i: (i, 0)
    weights = (w_a, w_i, w_b, w_c)
    outs = ((768, BF16), (640, F32), (768, BF16), (1024, BF16), (LANES, BF16), (1536, BF16))
    return pl.pallas_call(
        _in_proj_kernel,
        grid=(t // tm,),
        in_specs=[pl.BlockSpec((tm, d), row), pl.BlockSpec((1, d), fixed)]
        + [pl.BlockSpec(w.shape, fixed) for w in weights]
        + [pl.BlockSpec((1, B_Q_RANK), fixed), pl.BlockSpec(w_uq.shape, fixed),
           pl.BlockSpec((1, B_KV_RANK), fixed), pl.BlockSpec(w_ukv.shape, fixed)]
        + [pl.BlockSpec((3, tm, LANES), lambda i: (0, i % seq_tiles, 0)) for _ in tabs],
        out_specs=[pl.BlockSpec((tm, n), row) for n, _ in outs],
        out_shape=[jax.ShapeDtypeStruct((t, n), dt) for n, dt in outs],
        compiler_params=_cparams("parallel"),
    )(x, g.reshape(1, d), *weights, g_q.reshape(1, -1), w_uq, g_kv.reshape(1, -1), w_ukv, *tabs)


def _chunk_causal_mask(j, n_keys):
    qpos = j * Q_BLOCK + lax.broadcasted_iota(jnp.int32, (Q_BLOCK, n_keys), 0)
    kpos = lax.broadcasted_iota(jnp.int32, (Q_BLOCK, n_keys), 1)
    return (kpos >> CHUNK_SHIFT) <= (qpos >> CHUNK_SHIFT)


SEARCH_ROWS = 256
HALF_RANGE = 1 << 15


def _dsa_select(iq_ref, ikw_ref, key_ref, hi_ref, lo_ref, bias_ref, n_keys):
    j = pl.program_id(1)
    n_blk = n_keys // Q_BLOCK
    n_grp = n_keys // SEARCH_ROWS
    n_pack = n_keys // 16
    row0 = pl.multiple_of(j * Q_BLOCK, Q_BLOCK)
    sub = lax.broadcasted_iota(jnp.int32, (SEARCH_ROWS, Q_BLOCK), 0)
    lane = lax.broadcasted_iota(jnp.int32, (SEARCH_ROWS, Q_BLOCK), 1)
    q_chunk = (row0 + lane) >> CHUNK_SHIFT

    iq = iq_ref[...].astype(BF16)
    iq_stack = jnp.concatenate([iq[:, h * IDX_DIM:(h + 1) * IDX_DIM] for h in range(IDX_HEADS)], axis=0)
    iw_t = ikw_ref[pl.ds(row0, Q_BLOCK), :].T * (IDX_HEADS ** -0.5)

    for g in range(n_grp):
        k0 = g * SEARCH_ROWS
        ik = ikw_ref[k0:k0 + SEARCH_ROWS, 0:IDX_DIM].astype(BF16)
        score = jnp.zeros((SEARCH_ROWS, Q_BLOCK), F32)
        for hp in range(IDX_HEADS // 2):
            r = _nt_dot(ik, iq_stack[hp * 2 * Q_BLOCK:(hp + 1) * 2 * Q_BLOCK])
            for u in range(2):
                h = 2 * hp + u
                rel = jnp.maximum(r[:, u * Q_BLOCK:(u + 1) * Q_BLOCK] * (IDX_DIM ** -0.5), 0.0)
                score = score + rel * iw_t[IDX_DIM + h:IDX_DIM + h + 1, :]
        allowed = ((k0 + sub) >> CHUNK_SHIFT) <= q_chunk
        score = jnp.where(score == 0.0, 0.0, score)
        score = jnp.where(allowed, score, -jnp.inf)
        bits = lax.bitcast_convert_type(score, jnp.int32)
        key = bits ^ ((bits >> 31) & 0x7FFFFFFF)
        key_ref[k0:k0 + SEARCH_ROWS, :] = key
        hi_ref[k0:k0 + SEARCH_ROWS, :] = (key >> 16).astype(jnp.int16)
        lo_ref[k0:k0 + SEARCH_ROWS, :] = ((key & 0xFFFF) - HALF_RANGE).astype(jnp.int16)

    def count(pred):
        accs = [jnp.zeros((8, Q_BLOCK), F32)] * 4
        for r in range(n_keys // 8):
            accs[r % 4] = accs[r % 4] + jnp.where(pred(key_ref[8 * r:8 * (r + 1), :]), 1.0, 0.0)
        return jnp.sum(accs[0] + accs[1] + accs[2] + accs[3], axis=0, keepdims=True)

    one16 = jnp.ones((16, Q_BLOCK), jnp.int16)
    zero16 = jnp.zeros((16, Q_BLOCK), jnp.int16)

    def search16(ref):
        def rnd(i, base):
            cand = base + jnp.left_shift(jnp.int32(1), 15 - i)
            cand16 = jnp.broadcast_to(cand, (16, Q_BLOCK)).astype(jnp.int16)
            accs = [zero16] * 4
            for r in range(n_pack):
                accs[r % 4] = accs[r % 4] + jnp.where(ref[16 * r:16 * (r + 1), :] >= cand16, one16, zero16)
            total = (accs[0] + accs[1]) + (accs[2] + accs[3])
            cnt = jnp.sum(total.astype(jnp.int32), axis=0, keepdims=True)
            return jnp.where(cnt >= TOPK, cand, base)
        return lax.fori_loop(0, 16, rnd, jnp.full((1, Q_BLOCK), -HALF_RANGE, jnp.int32))

    top = search16(hi_ref)
    top16 = jnp.broadcast_to(top, (16, Q_BLOCK)).astype(jnp.int16)
    for r in range(n_pack):
        rows = slice(16 * r, 16 * (r + 1))
        hi = hi_ref[rows, :]
        lo_ref[rows, :] = jnp.where(hi > top16, jnp.int16(HALF_RANGE - 1),
                                    jnp.where(hi == top16, lo_ref[rows, :], jnp.int16(-HALF_RANGE)))
    thr = (top << 16) | (search16(lo_ref) + HALF_RANGE)
    thr8 = jnp.broadcast_to(thr, (8, Q_BLOCK))
    cnt_gt = count(lambda kk: kk > thr8)
    cnt_ge = count(lambda kk: kk >= thr8)
    tie_cols = (cnt_ge > TOPK) & (thr > NEG_INF_KEY)
    has_tie = jnp.max(jnp.where(tie_cols, 1.0, 0.0)) > 0.0

    for kb in range(n_blk):
        kk = key_ref[kb * Q_BLOCK:(kb + 1) * Q_BLOCK, :]
        bias_ref[kb * Q_BLOCK:(kb + 1) * Q_BLOCK, :] = jnp.where((kk >= thr) & (kk > NEG_INF_KEY), 0.0, -jnp.inf).T

    @pl.when(has_tie)
    def _():
        need = TOPK - cnt_gt
        row = lax.broadcasted_iota(jnp.int32, (Q_BLOCK, Q_BLOCK), 0)
        col = lax.broadcasted_iota(jnp.int32, (Q_BLOCK, Q_BLOCK), 1)
        lower = jnp.where(col < row, 1.0, 0.0).astype(BF16)

        def tie_block(kb, seen):
            k0 = pl.multiple_of(kb * Q_BLOCK, Q_BLOCK)
            kk = key_ref[pl.ds(k0, Q_BLOCK), :]
            eq = jnp.where(kk == thr, 1.0, 0.0)
            before = jnp.dot(lower, eq.astype(BF16), preferred_element_type=F32) + seen
            keep = (kk > thr) | ((kk == thr) & (before < need))
            bias_ref[pl.ds(k0, Q_BLOCK), :] = jnp.where(keep & (kk > NEG_INF_KEY), 0.0, -jnp.inf).T
            return seen + jnp.sum(eq, axis=0, keepdims=True)

        lax.fori_loop(0, n_blk, tie_block, jnp.zeros((1, Q_BLOCK), F32))


def _dsa_attend(q_ref, kv_ref, bias_ref, o_ref, n_keys):
    q = q_ref[...]
    k = kv_ref[0:n_keys, 0:A_HEAD_DIM]
    v = kv_ref[0:n_keys, A_HEAD_DIM:2 * A_HEAD_DIM]
    bias = jnp.concatenate([bias_ref[c * Q_BLOCK:(c + 1) * Q_BLOCK, :] for c in range(n_keys // Q_BLOCK)],
                           axis=1)
    qs = jnp.concatenate([q[:, h * A_HEAD_DIM:(h + 1) * A_HEAD_DIM] for h in range(A_HEADS)], axis=0)
    logits_all = _nt_dot(qs, k) * (A_HEAD_DIM ** -0.5)
    for h in range(A_HEADS):
        logits = logits_all[h * Q_BLOCK:(h + 1) * Q_BLOCK] + bias
        m = jnp.max(logits, axis=-1, keepdims=True)
        p = jnp.exp(logits - m)
        l = jnp.sum(p, axis=-1, keepdims=True)
        o = jnp.dot(p.astype(BF16), v, preferred_element_type=F32) / l
        o_ref[:, h * A_HEAD_DIM:(h + 1) * A_HEAD_DIM] = o.astype(o_ref.dtype)


def _dsa_kernel(q_ref, kv_ref, iq_ref, ikw_ref, o_ref, key_ref, hi_ref, lo_ref, bias_ref):
    j = pl.program_id(1)
    for v in range(KV_VARIANTS):
        @pl.when(j // (N_QB // KV_VARIANTS) == v)
        def _(v=v):
            n_keys = KV_STEP * (v + 1)
            n_scored = -(-n_keys // SEARCH_ROWS) * SEARCH_ROWS
            _dsa_select(iq_ref, ikw_ref, key_ref, hi_ref, lo_ref, bias_ref, n_scored)
            _dsa_attend(q_ref, kv_ref, bias_ref, o_ref, n_keys)


def _dsa_attention(za, zi, bsz):
    return pl.pallas_call(
        _dsa_kernel,
        grid=(bsz, N_QB),
        in_specs=[
            pl.BlockSpec((None, Q_BLOCK, 512), lambda b, j: (b, j, 0)),
            pl.BlockSpec((None, SEQ, 256), lambda b, j: (b, 0, 2)),
            pl.BlockSpec((None, Q_BLOCK, 512), lambda b, j: (b, j, 0)),
            pl.BlockSpec((None, SEQ, LANES), lambda b, j: (b, 0, 4)),
        ],
        out_specs=pl.BlockSpec((None, Q_BLOCK, 512), lambda b, j: (b, j, 0)),
        out_shape=jax.ShapeDtypeStruct((bsz, SEQ, 512), BF16),
        scratch_shapes=[pltpu.VMEM((SEQ, Q_BLOCK), jnp.int32),
                        pltpu.VMEM((SEQ, Q_BLOCK), jnp.int16),
                        pltpu.VMEM((SEQ, Q_BLOCK), jnp.int16),
                        pltpu.VMEM((SEQ, Q_BLOCK), F32)],
        compiler_params=_cparams("parallel", "arbitrary"),
    )(za, za, zi, zi)


def _mla_body(q_ref, kv_ref, kr_ref, o_ref, n_keys):
    j = pl.program_id(1)
    mask = _chunk_causal_mask(j, n_keys)
    kr = kr_ref[0:n_keys, 0:B_ROPE]
    scale = (B_NOPE + B_ROPE) ** -0.5
    for h in range(B_HEADS):
        qn = q_ref[:, h * B_NOPE:(h + 1) * B_NOPE]
        qr = q_ref[:, B_HEADS * B_NOPE + h * B_ROPE:B_HEADS * B_NOPE + (h + 1) * B_ROPE]
        kn = kv_ref[0:n_keys, h * B_NOPE:(h + 1) * B_NOPE]
        vv = kv_ref[0:n_keys, B_HEADS * B_NOPE + h * B_V:B_HEADS * B_NOPE + (h + 1) * B_V]
        s = (_nt_dot(qn, kn) + _nt_dot(qr, kr)) * scale
        s = jnp.where(mask, s, -jnp.inf)
        m = jnp.max(s, axis=-1, keepdims=True)
        p = jnp.exp(s - m)
        l = jnp.sum(p, axis=-1, keepdims=True)
        o = jnp.dot(p.astype(BF16), vv, preferred_element_type=F32) / l
        o_ref[:, h * B_V:(h + 1) * B_V] = o.astype(o_ref.dtype)


def _mla_kernel(q_ref, kv_ref, kr_ref, o_ref):
    j = pl.program_id(1)
    for v in range(KV_VARIANTS):
        @pl.when(j // (N_QB // KV_VARIANTS) == v)
        def _(v=v):
            _mla_body(q_ref, kv_ref, kr_ref, o_ref, KV_STEP * (v + 1))


def _mla_attention(qb, kvb, kr, bsz):
    return pl.pallas_call(
        _mla_kernel,
        grid=(bsz, N_QB),
        in_specs=[
            pl.BlockSpec((None, Q_BLOCK, 768), lambda b, j: (b, j, 0)),
            pl.BlockSpec((None, SEQ, 1024), lambda b, j: (b, 0, 0)),
            pl.BlockSpec((None, SEQ, LANES), lambda b, j: (b, 0, 0)),
        ],
        out_specs=pl.BlockSpec((None, Q_BLOCK, 512), lambda b, j: (b, j, 0)),
        out_shape=jax.ShapeDtypeStruct((bsz, SEQ, 512), BF16),
        compiler_params=_cparams("parallel", "arbitrary"),
    )(qb, kvb, kr)


def _band_bias(rel_table):
    n = 2 * REL_CLIP + 1
    period = 2 * n - 1
    heads = rel_table.shape[0]
    ext = jnp.concatenate([rel_table, jnp.broadcast_to(rel_table[:, n - 1:n], (heads, n - 1))], axis=1)
    kj = np.arange(Q_BLOCK)[:, None]
    qi = np.arange(Q_BLOCK)[None, :]
    out = []
    for d in range(C_KEY_BLOCKS):
        base = d * Q_BLOCK + REL_CLIP
        if base - (Q_BLOCK - 1) >= n - 1:
            bias = jnp.broadcast_to(rel_table[:, n - 1][:, None, None], (heads, Q_BLOCK, Q_BLOCK))
        else:
            shifted = jnp.roll(ext, -base, axis=1)
            bias = jnp.tile(shifted, (1, Q_BLOCK))[:, :Q_BLOCK * (period - 1)]
            bias = bias.reshape(heads, Q_BLOCK, period - 1)[:, :, :Q_BLOCK]
        cdiff = 2 * d + qi // CHUNK - kj // CHUNK
        valid = (cdiff >= 0) & (cdiff <= C_LEFT_CHUNKS)
        out.append(jnp.where(valid[None], bias.astype(F32), -jnp.inf))
    return jnp.swapaxes(jnp.stack(out, axis=1), 2, 3)


BAND_QB = 4


def _band_kernel(q_ref, kv_ref, bias_ref, o_ref):
    scale = C_HEAD_DIM ** -0.5
    window = C_KEY_BLOCKS * Q_BLOCK
    for sb in range(BAND_QB):
        j = pl.program_id(1) * BAND_QB + sb
        rows = slice(sb * Q_BLOCK, (sb + 1) * Q_BLOCK)
        first = jnp.maximum(j - (C_KEY_BLOCKS - 1), 0)
        win = pl.ds(pl.multiple_of(first * Q_BLOCK, Q_BLOCK), window)
        for h in range(C_HEADS):
            q = q_ref[rows, h * C_HEAD_DIM:(h + 1) * C_HEAD_DIM]
            k = kv_ref[win, h * C_HEAD_DIM:(h + 1) * C_HEAD_DIM]
            v = kv_ref[win, (C_HEADS + h) * C_HEAD_DIM:(C_HEADS + h + 1) * C_HEAD_DIM]
            s = _nt_dot(q, k) * scale
            parts = []
            for c in range(C_KEY_BLOCKS):
                d = j - (first + c)
                bias = bias_ref[h, jnp.clip(d, 0, C_KEY_BLOCKS - 1)]
                parts.append(jnp.where(d >= 0, s[:, c * Q_BLOCK:(c + 1) * Q_BLOCK] + bias, -jnp.inf))
            s = jnp.concatenate(parts, axis=1)
            m = jnp.max(s, axis=-1, keepdims=True)
            p = jnp.exp(s - m)
            l = jnp.sum(p, axis=-1, keepdims=True)
            o = jnp.dot(p.astype(BF16), v, preferred_element_type=F32) / l
            o_ref[rows, h * C_HEAD_DIM:(h + 1) * C_HEAD_DIM] = o.astype(o_ref.dtype)


def _band_attention(zc, bias, bsz):
    rows = BAND_QB * Q_BLOCK
    return pl.pallas_call(
        _band_kernel,
        grid=(bsz, SEQ // rows),
        in_specs=[
            pl.BlockSpec((None, rows, 512), lambda b, j: (b, j, 2)),
            pl.BlockSpec((None, SEQ, 1024), lambda b, j: (b, 0, 0)),
            pl.BlockSpec((C_HEADS, C_KEY_BLOCKS, Q_BLOCK, Q_BLOCK), lambda b, j: (0, 0, 0, 0)),
        ],
        out_specs=pl.BlockSpec((None, rows, 512), lambda b, j: (b, j, 0)),
        out_shape=jax.ShapeDtypeStruct((bsz, SEQ, 512), BF16),
        compiler_params=_cparams("parallel", "arbitrary"),
    )(zc, zc, bias)


def _merge_kernel(x_ref, g_ref, wgl_ref, oa_ref, ob_ref, oc_ref, wpa_ref, wpb_ref, wpc_ref, wout_ref, o_ref):
    x = x_ref[...]
    ms = jnp.mean(x * x, axis=-1, keepdims=True)
    h = (x * lax.rsqrt(ms + EPS) * g_ref[...]).astype(BF16)
    mix = jnp.zeros(x.shape, F32)
    for i, (o_in, wp) in enumerate(((oa_ref, wpa_ref), (ob_ref, wpb_ref), (oc_ref, wpc_ref))):
        gl = jnp.dot(h, wgl_ref[:, i * D_MODEL:(i + 1) * D_MODEL], preferred_element_type=F32)
        gate = jax.nn.sigmoid(gl)
        mix = mix + gate * jnp.dot(o_in[...], wp[...], preferred_element_type=F32)
    o_ref[...] = x + jnp.dot(mix.astype(BF16), wout_ref[...], preferred_element_type=F32)


def _merge(x, g, wgl, oa, ob, oc, wpa, wpb, wpc, wout, *, tm):
    t, d = x.shape
    row = lambda i: (i, 0)
    fixed = lambda i: (0, 0)
    return pl.pallas_call(
        _merge_kernel,
        grid=(t // tm,),
        in_specs=[
            pl.BlockSpec((tm, d), row),
            pl.BlockSpec((1, d), fixed),
            pl.BlockSpec((d, 3 * d), fixed),
            pl.BlockSpec((tm, 512), row),
            pl.BlockSpec((tm, 512), row),
            pl.BlockSpec((tm, 512), row),
            pl.BlockSpec((512, d), fixed),
            pl.BlockSpec((512, d), fixed),
            pl.BlockSpec((512, d), fixed),
            pl.BlockSpec((d, d), fixed),
        ],
        out_specs=pl.BlockSpec((tm, d), row),
        out_shape=jax.ShapeDtypeStruct((t, d), F32),
        compiler_params=_cparams("parallel"),
    )(x, g.reshape(1, d), wgl, oa, ob, oc, wpa, wpb, wpc, wout)


def _first_argmax(vals, lane):
    m = jnp.max(vals, axis=-1, keepdims=True)
    idx = jnp.min(jnp.where(vals == m, lane, LANES), axis=-1, keepdims=True)
    return m, idx


def _pack_bf16_pairs(h):
    n = h.shape[1] // 2
    bits = lax.bitcast_convert_type(h.astype(jnp.bfloat16).astype(F32), jnp.int32)
    return lax.shift_right_logical(bits[:, :n], 16) | bits[:, n:]


def _unpack_bf16_pairs(w):
    lo = lax.bitcast_convert_type(w << 16, F32).astype(BF16)
    hi = lax.bitcast_convert_type(w & HIGH_HALF, F32).astype(BF16)
    return lo, hi


def _router_kernel(x_ref, g_ref, w_ref, b_ref, hp_ref, meta_ref, seg_ref):
    x = x_ref[...]
    ms = jnp.mean(x * x, axis=-1, keepdims=True)
    h = x * lax.rsqrt(ms + EPS) * g_ref[...]
    hp_ref[...] = _pack_bf16_pairs(h)
    logits = jnp.dot(h, w_ref[...], preferred_element_type=F32, precision=lax.Precision.HIGHEST) + b_ref[...]
    lane = lax.broadcasted_iota(jnp.int32, logits.shape, 1)
    is_grp = (lane >= N_EXPERTS) & (lane < N_EXPERTS + N_GROUPS)
    gl = jnp.where(is_grp, logits, -jnp.inf)
    gmax, gidx = _first_argmax(gl, lane)
    pg = 1.0 / jnp.sum(jnp.exp(gl - gmax), axis=-1, keepdims=True)
    gsel = gidx - N_EXPERTS
    in_grp = (lane >> GROUP_SHIFT) == gsel
    el = jnp.where(in_grp, logits, -jnp.inf)
    m1, i1 = _first_argmax(el, lane)
    z = jnp.sum(jnp.exp(el - m1), axis=-1, keepdims=True)
    el2 = jnp.where(lane == i1, -jnp.inf, el)
    m2, i2 = _first_argmax(el2, lane)
    pe1 = 1.0 / z
    pe2 = jnp.exp(m2 - m1) / z
    den = pe1 + pe2
    w1 = pg * pe1 / den
    w2 = pg * pe2 / den

    sel1 = lane == i1
    sel2 = lane == i2
    onehot = jnp.where(sel1 | sel2, 1.0, 0.0)
    a = lax.broadcasted_iota(jnp.int32, (LANES, LANES), 0)
    b = lax.broadcasted_iota(jnp.int32, (LANES, LANES), 1)
    lower = jnp.where(b < a, 1.0, 0.0).astype(BF16)
    carry = jnp.zeros((1, LANES), F32)
    ranks = []
    for c in range(MOE_TILE // LANES):
        blk = onehot[c * LANES:(c + 1) * LANES]
        ranks.append(jnp.dot(lower, blk.astype(BF16), preferred_element_type=F32) + carry)
        carry = carry + jnp.sum(blk, axis=0, keepdims=True)
    rank = jnp.concatenate(ranks, axis=0)
    cnt = jnp.broadcast_to(carry, (8, LANES))
    seg = jnp.floor((cnt + 7.0) * 0.125) * 8.0
    lane8 = lax.broadcasted_iota(jnp.int32, (8, LANES), 1)
    scan = seg
    for k in (1, 2, 4, 8, 16, 32, 64):
        scan = scan + jnp.where(lane8 >= k, pltpu.roll(scan, k, 1), 0.0)
    off = scan - seg
    where_row = rank + off[0:1]
    pos1 = jnp.sum(jnp.where(sel1, where_row, 0.0), axis=-1, keepdims=True)
    pos2 = jnp.sum(jnp.where(sel2, where_row, 0.0), axis=-1, keepdims=True)
    meta_ref[...] = (jnp.where(lane == 0, pos1, 0.0) + jnp.where(lane == 1, pos2, 0.0)
                     + jnp.where(lane == 2, w1, 0.0) + jnp.where(lane == 3, w2, 0.0))
    row8 = lax.broadcasted_iota(jnp.int32, (8, LANES), 0)
    seg_ref[...] = jnp.where(row8 == 0, cnt, jnp.where(row8 == 1, off, 0.0)).astype(jnp.int32)


def _router(x, g, w, b):
    t, d = x.shape
    nt = t // MOE_TILE
    return pl.pallas_call(
        _router_kernel,
        grid=(nt,),
        in_specs=[
            pl.BlockSpec((MOE_TILE, d), lambda i: (i, 0)),
            pl.BlockSpec((1, d), lambda i: (0, 0)),
            pl.BlockSpec((d, LANES), lambda i: (0, 0)),
            pl.BlockSpec((1, LANES), lambda i: (0, 0)),
        ],
        out_specs=[
            pl.BlockSpec((MOE_TILE, d // 2), lambda i: (i, 0)),
            pl.BlockSpec((MOE_TILE, LANES), lambda i: (i, 0)),
            pl.BlockSpec((None, 8, LANES), lambda i: (i, 0, 0)),
        ],
        out_shape=[
            jax.ShapeDtypeStruct((t, d // 2), jnp.int32),
            jax.ShapeDtypeStruct((t, LANES), F32),
            jax.ShapeDtypeStruct((nt, 8, LANES), jnp.int32),
        ],
        compiler_params=_cparams("parallel"),
    )(x, g.reshape(1, d), w, b)


def _scatter_kernel(pos_ref, hp_ref, xs_ref):
    xs_ref[...] = jnp.zeros_like(xs_ref)

    def body(t, carry):
        row = hp_ref[pl.ds(t, 1), :]
        xs_ref[pl.ds(pos_ref[0, t], 1), :] = row
        xs_ref[pl.ds(pos_ref[0, MOE_TILE + t], 1), :] = row
        return carry

    lax.fori_loop(0, MOE_TILE, body, 0, unroll=8)


def _scatter_rows(pos, hp):
    nt = pos.shape[0]
    return pl.pallas_call(
        _scatter_kernel,
        grid=(nt,),
        in_specs=[
            pl.BlockSpec((None, 1, 2 * MOE_TILE), lambda i: (i, 0, 0), memory_space=pltpu.SMEM),
            pl.BlockSpec((MOE_TILE, hp.shape[1]), lambda i: (i, 0)),
        ],
        out_specs=pl.BlockSpec((None, XS_ROWS, hp.shape[1]), lambda i: (i, 0, 0)),
        out_shape=jax.ShapeDtypeStruct((nt, XS_ROWS, hp.shape[1]), jnp.int32),
        compiler_params=_cparams("parallel"),
    )(pos, hp)


CAST_EXPERTS = 4


def _cast_kernel(w_ref, o_ref):
    o_ref[...] = w_ref[...].astype(o_ref.dtype)


def _to_bf16(w):
    n_l, n_e, a, b = w.shape
    spec = pl.BlockSpec((None, CAST_EXPERTS, a, b), lambda l, e: (l, e, 0, 0))
    return pl.pallas_call(
        _cast_kernel,
        grid=(n_l, n_e // CAST_EXPERTS),
        in_specs=[spec],
        out_specs=spec,
        out_shape=jax.ShapeDtypeStruct(w.shape, BF16),
        compiler_params=_cparams("parallel", "parallel"),
    )(w)


def _experts_kernel(cnt_ref, off_ref, xs_ref, wg_ref, wu_ref, wd_ref, ys_ref):
    i = pl.program_id(0)
    e = pl.program_id(1)
    half = D_MODEL // 2

    @pl.when(e == 0)
    def _():
        ys_ref[2 * MOE_TILE:XS_ROWS, :] = jnp.zeros((XS_ROWS - 2 * MOE_TILE, D_MODEL), F32)

    n = cnt_ref[i, e]
    off = off_ref[i, e]

    def body(c, carry):
        start = pl.multiple_of(off + c * EXPERT_CHUNK, 8)
        lo, hi = _unpack_bf16_pairs(xs_ref[pl.ds(start, EXPERT_CHUNK), :])
        a = (jnp.dot(lo, wg_ref[0:half], preferred_element_type=F32)
             + jnp.dot(hi, wg_ref[half:D_MODEL], preferred_element_type=F32))
        u = (jnp.dot(lo, wu_ref[0:half], preferred_element_type=F32)
             + jnp.dot(hi, wu_ref[half:D_MODEL], preferred_element_type=F32))
        hh = (a * jax.nn.sigmoid(a)) * u
        ys_ref[pl.ds(start, EXPERT_CHUNK), :] = jnp.dot(hh.astype(BF16), wd_ref[...], preferred_element_type=F32)
        return carry

    lax.fori_loop(0, (n + EXPERT_CHUNK - 1) // EXPERT_CHUNK, body, 0)


def _experts(cnt, off, xs, wg, wu, wd, layer):
    nt = xs.shape[0]
    d = D_MODEL
    grid_spec = pltpu.PrefetchScalarGridSpec(
        num_scalar_prefetch=2,
        grid=(nt, N_EXPERTS),
        in_specs=[
            pl.BlockSpec((None, XS_ROWS, d // 2), lambda i, e, c, o: (i, 0, 0), pipeline_mode=pl.Buffered(1)),
            pl.BlockSpec((None, None, d, D_EXPERT), lambda i, e, c, o: (layer, e, 0, 0)),
            pl.BlockSpec((None, None, d, D_EXPERT), lambda i, e, c, o: (layer, e, 0, 0)),
            pl.BlockSpec((None, None, D_EXPERT, d), lambda i, e, c, o: (layer, e, 0, 0)),
        ],
        out_specs=pl.BlockSpec((None, XS_ROWS, d), lambda i, e, c, o: (i, 0, 0)),
    )
    return pl.pallas_call(
        _experts_kernel,
        grid_spec=grid_spec,
        out_shape=jax.ShapeDtypeStruct((nt, XS_ROWS, d), F32),
        compiler_params=_cparams("parallel", "arbitrary"),
    )(cnt, off, xs, wg, wu, wd)


def _combine_kernel(pos_ref, wt_ref, x_ref, ys_ref, g_ref, o_ref, *, final_norm):
    s = pl.program_id(1)

    def body(tl, carry):
        t = s * COMBINE_ROWS + tl
        y = (ys_ref[pl.ds(pos_ref[0, t], 1), :] * wt_ref[0, t]
             + ys_ref[pl.ds(pos_ref[0, MOE_TILE + t], 1), :] * wt_ref[0, MOE_TILE + t])
        o_ref[pl.ds(tl, 1), :] = x_ref[pl.ds(tl, 1), :] + y
        return carry

    lax.fori_loop(0, COMBINE_ROWS, body, 0, unroll=8)
    if final_norm:
        o_ref[...] = _rms(o_ref[...], g_ref[...])


def _combine(pos, wt, x, ys, final_g):
    t, d = x.shape
    nt = pos.shape[0]
    sub = MOE_TILE // COMBINE_ROWS
    g = jnp.ones((d,), F32) if final_g is None else final_g
    return pl.pallas_call(
        functools.partial(_combine_kernel, final_norm=final_g is not None),
        grid=(nt, sub),
        in_specs=[
            pl.BlockSpec((None, 1, 2 * MOE_TILE), lambda i, s: (i, 0, 0), memory_space=pltpu.SMEM),
            pl.BlockSpec((None, 1, 2 * MOE_TILE), lambda i, s: (i, 0, 0), memory_space=pltpu.SMEM),
            pl.BlockSpec((COMBINE_ROWS, d), lambda i, s: (i * sub + s, 0)),
            pl.BlockSpec((None, XS_ROWS, d), lambda i, s: (i, 0, 0)),
            pl.BlockSpec((1, d), lambda i, s: (0, 0)),
        ],
        out_specs=pl.BlockSpec((COMBINE_ROWS, d), lambda i, s: (i * sub + s, 0)),
        out_shape=jax.ShapeDtypeStruct((t, d), F32),
        compiler_params=_cparams("parallel", "arbitrary"),
    )(pos, wt, x, ys, g.reshape(1, d))


def _moe(x, g, w_r, b_r, wg, wu, wd, layer, final_g=None):
    t = x.shape[0]
    nt = t // MOE_TILE
    hp, meta, seg = _router(x, g, w_r, b_r)
    pair_major = lambda m: m.reshape(nt, MOE_TILE, 2).transpose(0, 2, 1).reshape(nt, 1, 2 * MOE_TILE)
    pos = pair_major(meta[:, 0:2].astype(jnp.int32))
    wt = pair_major(meta[:, 2:4])
    xs = _scatter_rows(pos, hp)
    ys = _experts(seg[:, 0, :N_EXPERTS], seg[:, 1, :N_EXPERTS], xs, wg, wu, wd, layer)
    return _combine(pos, wt, x, ys, final_g)


def _pad_cols(w, n):
    return jnp.pad(w, ((0, 0), (0, n - w.shape[1])))


def kernel(x, attn_norm_g, w_in, b_q_norm_g, b_w_uq, b_kv_norm_g, b_w_ukv, c_rel_bias, w_proj_a, w_proj_b, w_proj_c, w_out, ffn_norm_g, w_group, b_group, w_router, b_router, w_gate, w_up, w_down, final_norm_g):
    bsz, seq, d = x.shape
    assert (seq, d) == (SEQ, D_MODEL)
    t = bsz * seq
    depth = w_in.shape[0]
    tm = 512

    tab_a = _rope_table(A_ROT, A_HEAD_DIM)
    tab_i = _rope_table(IDX_ROT, IDX_DIM)
    tab_i_half = _rope_table(IDX_ROT, IDX_DIM, active_lanes=IDX_DIM)
    tab_b = _rope_table(B_ROPE, B_ROPE)
    tab_b_half = _rope_table(B_ROPE, B_ROPE, active_lanes=B_ROPE)

    tabs = (tab_a, tab_i, tab_i_half, tab_b, tab_b_half)
    w_gate, w_up, w_down = _to_bf16(w_gate), _to_bf16(w_up), _to_bf16(w_down)
    xf = x.reshape(t, d)
    for l in range(depth):
        w = w_in[l]
        w_a = w[:, 0:768].astype(BF16)
        w_i = _pad_cols(w[:, 768:1352], 640).astype(BF16)
        w_b = _pad_cols(w[:, 1352:1800], 512).astype(BF16)
        w_c = jnp.concatenate([w[:, 2312:3336], w[:, 1800:2312]], axis=1).astype(BF16)
        w_g = w[:, 3336:6408].astype(BF16)
        g_attn = attn_norm_g[l]
        w_uq = b_w_uq[l].reshape(B_Q_RANK, B_HEADS, B_NOPE + B_ROPE)
        w_uq = jnp.concatenate([w_uq[:, :, :B_NOPE].reshape(B_Q_RANK, -1),
                                w_uq[:, :, B_NOPE:].reshape(B_Q_RANK, -1)], axis=1).astype(BF16)
        w_ukv = b_w_ukv[l].reshape(B_KV_RANK, B_HEADS, B_NOPE + B_V)
        w_ukv = jnp.concatenate([w_ukv[:, :, :B_NOPE].reshape(B_KV_RANK, -1),
                                 w_ukv[:, :, B_NOPE:].reshape(B_KV_RANK, -1)], axis=1).astype(BF16)

        za, zi, qb, kvb, kr, zc = _in_proj(xf, g_attn, w_a, w_i, w_b, w_c, b_q_norm_g[l], w_uq,
                                           b_kv_norm_g[l], w_ukv, tabs, tm=tm)
        per_seq = lambda z: z.reshape(bsz, seq, -1)
        o_a = _dsa_attention(per_seq(za), per_seq(zi), bsz)
        o_b = _mla_attention(per_seq(qb), per_seq(kvb), per_seq(kr), bsz)
        o_c = _band_attention(per_seq(zc), _band_bias(c_rel_bias[l]), bsz)

        xf = _merge(xf, g_attn, w_g, o_a.reshape(t, -1), o_b.reshape(t, -1), o_c.reshape(t, -1),
                    w_proj_a[l].astype(BF16), w_proj_b[l].astype(BF16), w_proj_c[l].astype(BF16),
                    w_out[l].astype(BF16), tm=2 * tm)

        w_r = _pad_cols(jnp.concatenate([w_router[l], w_group[l]], axis=1), LANES)
        b_r = _pad_cols(jnp.concatenate([b_router[l], b_group[l]])[None, :], LANES)
        xf = _moe(xf, ffn_norm_g[l], w_r, b_r, w_gate, w_up, w_down, l,
                  final_g=final_norm_g if l == depth - 1 else None)

    return xf.reshape(bsz, seq, d)
```
